```python
import math
import jax, jax.numpy as jnp
from jax import lax
import numpy as np


D_MODEL = 1024
BATCH = 16
SEQ = 4096
DEPTH = 1

HEAD_DIM = 64
A_HEADS = 4
A_VDIM = 2 * HEAD_DIM
B_HEADS = 8
B_KV_HEADS = 2
B_GROUP = B_HEADS // B_KV_HEADS
IDX_HEADS = 8
IDX_DIM = 64
TOPK_MAX = 256
N_BUCKETS = 32
MAX_DISTANCE = 128
D_FF = 2816
CONV_W = 3
Q_BLOCK = 128
EPS = 1e-6
MIX_WIDTH = A_HEADS * A_VDIM + B_HEADS * HEAD_DIM
IN_SIZES = (A_HEADS * 2 * HEAD_DIM, A_HEADS * 2 * HEAD_DIM, A_HEADS * A_VDIM,
            B_HEADS * HEAD_DIM, B_KV_HEADS * HEAD_DIM, B_KV_HEADS * HEAD_DIM,
            IDX_HEADS * IDX_DIM, IDX_DIM, IDX_HEADS)
IN_COLS = sum(IN_SIZES)

kernel_name = 'hybrid_diffattn_dsa_convffn_block'


def rms_norm(x, g):
    xf = x.astype(jnp.float32)
    y = xf * lax.rsqrt(jnp.mean(xf * xf, axis=-1, keepdims=True) + EPS)
    return (y * g.astype(jnp.float32)).astype(x.dtype)


def t5_bucket(rel):
    n = jnp.maximum(rel, 0)
    max_exact = N_BUCKETS // 2
    nf = jnp.maximum(n, 1).astype(jnp.float32)
    large = max_exact + (jnp.log(nf / max_exact) / math.log(MAX_DISTANCE / max_exact)
                         * (N_BUCKETS - max_exact)).astype(jnp.int32)
    large = jnp.minimum(large, N_BUCKETS - 1)
    return jnp.where(n < max_exact, n, large)


def diff_attention(q, k, v, lam, lam_init, subln_g, bias_a):
    B_, S_, H, _, Dh = q.shape
    E = v.shape[-1]
    nb = S_ // Q_BLOCK
    kpos = jnp.arange(S_)
    scale = Dh ** -0.5

    def block(i):
        start = i * Q_BLOCK
        qb = lax.dynamic_slice_in_dim(q, start, Q_BLOCK, axis=1)
        qpos = start + jnp.arange(Q_BLOCK)
        rel = qpos[:, None] - kpos[None, :]
        bias = jnp.transpose(bias_a[t5_bucket(rel)], (2, 0, 1))
        logits = jnp.einsum('bthcd,bshcd->bhcts', qb, k).astype(jnp.float32) * scale
        logits = logits + bias[None, :, None].astype(jnp.float32)
        logits = jnp.where((rel >= 0)[None, None, None], logits, -jnp.inf)
        p = jax.nn.softmax(logits, axis=-1)
        attn = p[:, :, 0] - lam * p[:, :, 1]
        return jnp.einsum('bhts,bshe->bthe', attn.astype(v.dtype), v)

    out = lax.map(block, jnp.arange(nb))
    out = jnp.moveaxis(out, 0, 1).reshape(B_, S_, H, E)
    out = rms_norm(out, subln_g) * (1.0 - lam_init)
    return out.reshape(B_, S_, H * E)


def dsa_attention(q, k, v, iq, ik, iw, bias_b):
    B_, S_, HB, Dh = q.shape
    G = k.shape[2]
    R = HB // G
    topk = min(TOPK_MAX, S_ // 4)
    nb = S_ // Q_BLOCK
    kpos = jnp.arange(S_)
    scale = Dh ** -0.5

    def block(i):
        start = i * Q_BLOCK
        qb = lax.dynamic_slice_in_dim(q, start, Q_BLOCK, axis=1)
        iqb = lax.dynamic_slice_in_dim(iq, start, Q_BLOCK, axis=1)
        iwb = lax.dynamic_slice_in_dim(iw, start, Q_BLOCK, axis=1)
        qpos = start + jnp.arange(Q_BLOCK)
        causal = qpos[:, None] >= kpos[None, :]
        s_h = jax.nn.relu(jnp.einsum('bthd,bsd->bths', iqb, ik).astype(jnp.float32))
        score = jnp.einsum('bth,bths->bts', iwb.astype(jnp.float32), s_h)
        score = jnp.where(causal[None], score, -jnp.inf)
        _, idx = lax.top_k(score, topk)
        valid = idx <= qpos[None, :, None]
        kg = jax.vmap(lambda a, ix: a[ix])(k, idx)
        vg = jax.vmap(lambda a, ix: a[ix])(v, idx)
        qg = qb.reshape(B_, Q_BLOCK, G, R, Dh)
        logits = jnp.einsum('btgrd,btkgd->btgrk', qg, kg).astype(jnp.float32) * scale
        bias = bias_b[t5_bucket(qpos[None, :, None] - idx)]
        bias = jnp.transpose(bias.reshape(B_, Q_BLOCK, topk, G, R), (0, 1, 3, 4, 2))
        logits = jnp.where(valid[:, :, None, None, :], logits + bias.astype(jnp.float32), -jnp.inf)
        p = jax.nn.softmax(logits, axis=-1)
        o = jnp.einsum('btgrk,btkgd->btgrd', p.astype(vg.dtype), vg)
        return o.reshape(B_, Q_BLOCK, HB * Dh)

    out = lax.map(block, jnp.arange(nb))
    return jnp.moveaxis(out, 0, 1).reshape(B_, S_, HB * Dh)


def causal_dwconv(u, w, b):
    S_ = u.shape[1]
    up = jnp.pad(u, ((0, 0), (CONV_W - 1, 0), (0, 0)))
    y = b
    for j in range(CONV_W):
        y = y + up[:, j:j + S_] * w[j]
    return y


def setup_inputs(seed: int = 0) -> dict:
    key = jax.random.key(seed)
    ks = jax.random.split(key, 24)
    f32 = jnp.float32

    def nrm(k, shape, s):
        return jax.random.normal(k, shape, f32) * s

    L = DEPTH
    return {
        'x': nrm(ks[0], (BATCH, SEQ, D_MODEL), 1.0),
        'c': nrm(ks[1], (BATCH, D_MODEL), 1.0),
        'w_ada': nrm(ks[2], (L, D_MODEL, 6 * D_MODEL), 0.5 * D_MODEL ** -0.5),
        'b_ada': nrm(ks[3], (L, 6 * D_MODEL), 0.01),
        'g_attn': 1.0 + nrm(ks[4], (L, D_MODEL), 0.01),
        'w_in': nrm(ks[5], (L, D_MODEL, IN_COLS), D_MODEL ** -0.5),
        'q_norm_a': 1.0 + nrm(ks[6], (L, HEAD_DIM), 0.01),
        'k_norm_a': 1.0 + nrm(ks[7], (L, HEAD_DIM), 0.01),
        'q_norm_b': 1.0 + nrm(ks[8], (L, HEAD_DIM), 0.01),
        'k_norm_b': 1.0 + nrm(ks[9], (L, HEAD_DIM), 0.01),
        'lam_vecs': nrm(ks[10], (L, 4, HEAD_DIM), 0.1),
        'subln_a': 1.0 + nrm(ks[11], (L, A_VDIM), 0.01),
        'w_out': nrm(ks[12], (L, MIX_WIDTH, D_MODEL), MIX_WIDTH ** -0.5),
        'g_ffn': 1.0 + nrm(ks[13], (L, D_MODEL), 0.01),
        'w_up': nrm(ks[14], (L, D_MODEL, 2 * D_FF), D_MODEL ** -0.5),
        'conv_w': nrm(ks[15], (L, CONV_W, 2 * D_FF), CONV_W ** -0.5),
        'conv_b': nrm(ks[16], (L, 2 * D_FF), 0.01),
        'w_down': nrm(ks[17], (L, D_FF, D_MODEL), D_FF ** -0.5),
        'rel_bias': nrm(ks[18], (N_BUCKETS, A_HEADS + B_HEADS), 0.5),
    }


def reference(x, c, w_ada, b_ada, g_attn, w_in, q_norm_a, k_norm_a, q_norm_b, k_norm_b,
              lam_vecs, subln_a, w_out, g_ffn, w_up, conv_w, conv_b, w_down, rel_bias):
    B_, S_, _ = x.shape
    split_points = [int(v) for v in np.cumsum(IN_SIZES)[:-1]]
    bias_a = rel_bias[:, :A_HEADS]
    bias_b = rel_bias[:, A_HEADS:]
    for l in range(DEPTH):
        lam_init = 0.8 - 0.6 * math.exp(-0.3 * l)
        mod = jax.nn.silu(c) @ w_ada[l] + b_ada[l]
        sh_a, sc_a, gt_a, sh_f, sc_f, gt_f = jnp.split(mod, 6, axis=-1)

        h = rms_norm(x, g_attn[l]) * (1.0 + sc_a[:, None]) + sh_a[:, None]
        proj = h @ w_in[l]
        aq, ak, av, bq, bk, bv, iq, ik, iw = jnp.split(proj, split_points, axis=-1)
        aq = rms_norm(aq.reshape(B_, S_, A_HEADS, 2, HEAD_DIM), q_norm_a[l])
        ak = rms_norm(ak.reshape(B_, S_, A_HEADS, 2, HEAD_DIM), k_norm_a[l])
        av = av.reshape(B_, S_, A_HEADS, A_VDIM)
        bq = rms_norm(bq.reshape(B_, S_, B_HEADS, HEAD_DIM), q_norm_b[l])
        bk = rms_norm(bk.reshape(B_, S_, B_KV_HEADS, HEAD_DIM), k_norm_b[l])
        bv = bv.reshape(B_, S_, B_KV_HEADS, HEAD_DIM)
        iq = iq.reshape(B_, S_, IDX_HEADS, IDX_DIM)
        iw = iw * (IDX_HEADS * IDX_DIM) ** -0.5

        lv = lam_vecs[l].astype(jnp.float32)
        lam = jnp.exp(jnp.sum(lv[0] * lv[1])) - jnp.exp(jnp.sum(lv[2] * lv[3])) + lam_init

        o_a = diff_attention(aq, ak, av, lam, lam_init, subln_a[l], bias_a)
        o_b = dsa_attention(bq, bk, bv, iq, ik, iw, bias_b)
        mixed = jnp.concatenate([o_a, o_b], axis=-1) @ w_out[l]
        x = x + gt_a[:, None] * mixed

        h = rms_norm(x, g_ffn[l]) * (1.0 + sc_f[:, None]) + sh_f[:, None]
        u = causal_dwconv(h @ w_up[l], conv_w[l], conv_b[l])
        u_gate, u_val = jnp.split(u, 2, axis=-1)
        y = (jax.nn.silu(u_gate) * u_val) @ w_down[l]
        x = x + gt_f[:, None] * y
    return x
```

```python
import functools
import math

import jax
import jax.numpy as jnp
from jax import lax
from jax.experimental import pallas as pl
from jax.experimental.pallas import tpu as pltpu

F32 = jnp.float32
BF16 = jnp.bfloat16
I32 = jnp.int32

HEAD_DIM = 64
A_HEADS = 4
A_VDIM = 2 * HEAD_DIM
B_HEADS = 8
B_KV_HEADS = 2
B_GROUP = B_HEADS // B_KV_HEADS
IDX_HEADS = 8
IDX_DIM = 64
TOPK_MAX = 256
N_BUCKETS = 32
MAX_DISTANCE = 128
CONV_W = 3
EPS = 1e-6

LANES = 128
ATT_T = 256
MASKED = -1e30
NEG_KEY = -2139095041
INT_MIN = -2147483648
VMEM_LIMIT = 56 * 1024 * 1024


def _dot(a, b):
    return jnp.dot(a, b, preferred_element_type=F32)


def _dot_nt(a, b):
    return lax.dot_general(a, b, (((1,), (1,)), ((), ())), preferred_element_type=F32)


def _split_bf16(a):
    hi = a.astype(BF16)
    lo = (a - hi.astype(F32)).astype(BF16)
    return hi, lo


def _ada_kernel(c_ref, w_ref, b_ref, o_ref):
    c = c_ref[...]
    a = c / (1.0 + jnp.exp(-c))
    a_hi, a_lo = _split_bf16(a)
    w_hi, w_lo = _split_bf16(w_ref[...])
    o_ref[...] = _dot(a_hi, w_hi) + _dot(a_hi, w_lo) + _dot(a_lo, w_hi) + b_ref[...]


def _ada_call(c, w, b):
    bsz, d = c.shape
    n = w.shape[1]
    tn = 1024
    return pl.pallas_call(
        _ada_kernel,
        grid=(n // tn,),
        in_specs=[pl.BlockSpec((bsz, d), lambda j: (0, 0)),
                  pl.BlockSpec((d, tn), lambda j: (0, j)),
                  pl.BlockSpec((1, tn), lambda j: (0, j))],
        out_specs=pl.BlockSpec((bsz, tn), lambda j: (0, j)),
        out_shape=jax.ShapeDtypeStruct((bsz, n), F32),
        compiler_params=pltpu.CompilerParams(dimension_semantics=("arbitrary",),
                                             vmem_limit_bytes=VMEM_LIMIT),
        name="ada_mod",
    )(c, w, b.reshape(1, n))


def _bias_kernel(tab_ref, o_ref, *, t):
    h = pl.program_id(0)
    max_exact = N_BUCKETS // 2
    row = lax.broadcasted_iota(I32, (t, t), 0)
    col = lax.broadcasted_iota(I32, (t, t), 1)
    for d in range(2):
        n = jnp.maximum(row - col + d * t, 0)
        nf = jnp.maximum(n, 1).astype(F32)
        large = max_exact + (jnp.log(nf / max_exact) / math.log(MAX_DISTANCE / max_exact)
                             * (N_BUCKETS - max_exact)).astype(I32)
        large = jnp.minimum(large, N_BUCKETS - 1)
        bucket = jnp.where(n < max_exact, n, large)
        last = tab_ref[N_BUCKETS - 1, h]
        acc = jnp.zeros((t, t), F32)
        for k in range(N_BUCKETS - 1):
            acc = jnp.where(bucket == k, tab_ref[k, h] - last, acc)
        o_ref[0, d] = acc


def _bias_call(rel_bias, t):
    nh = rel_bias.shape[1]
    return pl.pallas_call(
        functools.partial(_bias_kernel, t=t),
        grid=(nh,),
        in_specs=[pl.BlockSpec(memory_space=pltpu.SMEM)],
        out_specs=pl.BlockSpec((1, 2, t, t), lambda h: (h, 0, 0, 0)),
        out_shape=jax.ShapeDtypeStruct((nh, 2, t, t), F32),
        compiler_params=pltpu.CompilerParams(dimension_semantics=("arbitrary",)),
        name="bias_tiles",
    )(rel_bias)


def _modulated_norm(x, g, scale, shift):
    y = x * lax.rsqrt(jnp.mean(x * x, axis=-1, keepdims=True) + EPS)
    return (y * g) * (1.0 + scale) + shift


def _head_norm(y, gain, bd):
    ss = _dot((y * y).astype(BF16), bd)
    return y * lax.rsqrt(ss * (1.0 / HEAD_DIM) + EPS) * gain


def _inproj_kernel(x_ref, mod_ref, g_ref, waq_ref, wak_ref, wav_ref, wbq_ref, wbkv_ref, wiq_ref,
                   wik_ref, gqa_ref, gka_ref, gqb_ref, gkb_ref, bd_ref,
                   aq_ref, ak_ref, av_ref, bq_ref, bkv_ref, iq_ref, ik_ref, iw_ref):
    x = x_ref[0]
    h = _modulated_norm(x, g_ref[...], mod_ref[0, 1:2, :], mod_ref[0, 0:1, :]).astype(BF16)
    bd = bd_ref[...]
    aq_ref[0] = _head_norm(_dot(h, waq_ref[...]), gqa_ref[...], bd).astype(BF16)
    ak_ref[0] = _head_norm(_dot(h, wak_ref[...]), gka_ref[...], bd).astype(BF16)
    av_ref[0] = _dot(h, wav_ref[...]).astype(BF16)
    bq_ref[0] = _head_norm(_dot(h, wbq_ref[...]), gqb_ref[...], bd).astype(BF16)
    kv = _dot(h, wbkv_ref[...])
    nk = B_KV_HEADS * HEAD_DIM
    bk = _head_norm(kv[:, :nk], gkb_ref[...], bd[:nk, :nk])
    bkv_ref[0] = jnp.concatenate([bk, kv[:, nk:]], axis=1).astype(BF16)
    iq_ref[0] = _dot(h, wiq_ref[...]).astype(BF16)
    ikw = _dot(h, wik_ref[...])
    ik_ref[0] = ikw[:, :IDX_DIM].astype(BF16)
    iw_ref[0] = ikw[:, IDX_DIM:IDX_DIM + IDX_HEADS] * ((IDX_HEADS * IDX_DIM) ** -0.5)


def _inproj_call(x, mod, g_attn, w_in, gqa, gka, gqb, gkb, ts):
    bsz, s, d = x.shape
    na = A_HEADS * 2 * HEAD_DIM
    nb = B_HEADS * HEAD_DIM
    nkv = B_KV_HEADS * HEAD_DIM
    ni = IDX_HEADS * IDX_DIM
    o = 0
    parts = []
    for width in (na, na, A_HEADS * A_VDIM, nb, 2 * nkv, ni, IDX_DIM + IDX_HEADS):
        parts.append(w_in[:, o:o + width].astype(BF16))
        o += width
    parts[-1] = jnp.pad(parts[-1], ((0, 0), (0, LANES - parts[-1].shape[1])))
    grp = jnp.arange(na) // HEAD_DIM
    bd = (grp[:, None] == grp[None, :]).astype(BF16)

    def full(a):
        return pl.BlockSpec(a.shape, lambda b, i: (0,) * a.ndim)

    def tok(n):
        return pl.BlockSpec((1, ts, n), lambda b, i: (b, i, 0))

    consts = parts + [gqa, gka, gqb, gkb, bd]
    out_w = [(na, BF16), (na, BF16), (A_HEADS * A_VDIM, BF16), (nb, BF16), (2 * nkv, BF16),
             (ni, BF16), (IDX_DIM, BF16), (IDX_HEADS, F32)]
    return pl.pallas_call(
        _inproj_kernel,
        grid=(bsz, s // ts),
        in_specs=[tok(d), pl.BlockSpec((1, 6, d), lambda b, i: (b, 0, 0)), full(g_attn)]
                 + [full(a) for a in consts],
        out_specs=[tok(n) for n, _ in out_w],
        out_shape=[jax.ShapeDtypeStruct((bsz, s, n), dt) for n, dt in out_w],
        compiler_params=pltpu.CompilerParams(dimension_semantics=("arbitrary", "arbitrary"),
                                             vmem_limit_bytes=VMEM_LIMIT),
        name="in_proj",
    )(x, mod, g_attn, *consts)


def _softmax_step(s, v, m_ref, l_ref, acc_ref):
    m_prev = m_ref[...]
    m_new = jnp.maximum(m_prev, jnp.max(s, axis=1, keepdims=True))
    alpha = jnp.exp(m_prev - m_new)
    p = jnp.exp(s - m_new)
    l_ref[...] = alpha * l_ref[...] + jnp.sum(p, axis=1, keepdims=True)
    acc_ref[...] = alpha * acc_ref[...] + _dot(p.astype(BF16), v)
    m_ref[...] = m_new


def _init_softmax_state(m_ref, l_ref, acc_ref):
    m_ref[...] = jnp.full(m_ref.shape, MASKED, F32)
    l_ref[...] = jnp.zeros(l_ref.shape, F32)
    acc_ref[...] = jnp.zeros(acc_ref.shape, F32)


def _diff_kernel(lam_ref, g_ref, q_ref, k_ref, v_ref, bias_ref, o_ref, m_ref, l_ref, acc_ref, *,
                 t, lam_init):
    i = pl.program_id(2)
    q = q_ref[0]
    lane = lax.broadcasted_iota(I32, q.shape, 1)
    zero = jnp.zeros_like(q)
    qq = jnp.concatenate([jnp.where(lane < HEAD_DIM, q, zero),
                          jnp.where(lane >= HEAD_DIM, q, zero)], axis=0)
    _init_softmax_state(m_ref, l_ref, acc_ref)

    def step(j, bias=None, causal=False):
        k = k_ref[0, pl.ds(pl.multiple_of(j * t, t), t), :]
        v = v_ref[0, pl.ds(pl.multiple_of(j * t, t), t), :]
        s = _dot_nt(qq, k)
        if bias is not None:
            s = s + jnp.concatenate([bias, bias], axis=0)
        if causal:
            row = lax.broadcasted_iota(I32, (2 * t, t), 0) & (t - 1)
            col = lax.broadcasted_iota(I32, (2 * t, t), 1)
            s = jnp.where(row >= col, s, MASKED)
        _softmax_step(s, v, m_ref, l_ref, acc_ref)

    def far(j, carry):
        step(j)
        return carry

    lax.fori_loop(0, i - 1, far, 0)

    @pl.when(i > 0)
    def _():
        step(i - 1, bias=bias_ref[0, 1])

    step(i, bias=bias_ref[0, 0], causal=True)

    lv = lam_ref[...]
    lam = (jnp.exp(jnp.sum(lv[0:1] * lv[1:2], axis=1, keepdims=True))
           - jnp.exp(jnp.sum(lv[2:3] * lv[3:4], axis=1, keepdims=True)) + lam_init)
    o = acc_ref[...] / l_ref[...]
    o = o[:t] - lam * o[t:]
    o = o * lax.rsqrt(jnp.mean(o * o, axis=-1, keepdims=True) + EPS)
    o_ref[0] = (o * g_ref[...] * (1.0 - lam_init)).astype(BF16)


def _diff_call(lam_vecs, subln, aq, ak, av, bias, t, lam_init):
    bsz, s, _ = aq.shape
    return pl.pallas_call(
        functools.partial(_diff_kernel, t=t, lam_init=lam_init),
        grid=(bsz, A_HEADS, s // t),
        in_specs=[pl.BlockSpec(lam_vecs.shape, lambda b, h, i: (0, 0)),
                  pl.BlockSpec(subln.shape, lambda b, h, i: (0, 0)),
                  pl.BlockSpec((1, t, A_VDIM), lambda b, h, i: (b, i, h)),
                  pl.BlockSpec((1, s, A_VDIM), lambda b, h, i: (b, 0, h)),
                  pl.BlockSpec((1, s, A_VDIM), lambda b, h, i: (b, 0, h)),
                  pl.BlockSpec((1, 2, t, t), lambda b, h, i: (h, 0, 0, 0))],
        out_specs=pl.BlockSpec((1, t, A_VDIM), lambda b, h, i: (b, i, h)),
        out_shape=jax.ShapeDtypeStruct((bsz, s, A_HEADS * A_VDIM), BF16),
        scratch_shapes=[pltpu.VMEM((2 * t, 1), F32), pltpu.VMEM((2 * t, 1), F32),
                        pltpu.VMEM((2 * t, A_VDIM), F32)],
        compiler_params=pltpu.CompilerParams(
            dimension_semantics=("arbitrary", "arbitrary", "arbitrary"),
            vmem_limit_bytes=VMEM_LIMIT),
        name="diff_attn",
    )(lam_vecs, subln, aq, ak, av, bias)


def _sortable_key(x):
    b = pltpu.bitcast(x, I32)
    return b ^ ((b >> 31) & 0x7FFFFFFF)


def _dsa_kernel(iq_ref, iw_ref, ik_ref, q_ref, kv_ref, bias_ref, o_ref,
                keys_ref, m_ref, l_ref, acc_ref, *, t, topk):
    i = pl.program_id(1)
    row = lax.broadcasted_iota(I32, (t, t), 0)
    col = lax.broadcasted_iota(I32, (t, t), 1)

    iq = iq_ref[0]
    iw = iw_ref[0]
    iqs = [iq[:, h * IDX_DIM:(h + 1) * IDX_DIM] for h in range(IDX_HEADS)]

    def score_block(j, causal):
        kc = ik_ref[0, pl.ds(pl.multiple_of(j * t, t), t), :]
        acc = jnp.zeros((t, t), F32)
        for h in range(IDX_HEADS):
            acc = acc + iw[:, h:h + 1] * jnp.maximum(_dot_nt(iqs[h], kc), 0.0)
        if causal:
            acc = jnp.where(row >= col, acc, -jnp.inf)
        keys_ref[j] = _sortable_key(acc)

    def score_body(j, carry):
        score_block(j, False)
        return carry

    lax.fori_loop(0, i, score_body, 0)
    score_block(i, True)

    def count_ge(cand):
        def body(j, cnt):
            return cnt + jnp.where(keys_ref[j] >= cand, 1, 0)
        cnt = lax.fori_loop(0, i + 1, body, jnp.zeros((t, t), I32))
        return jnp.sum(cnt, axis=1, keepdims=True)

    def bit_body(it, thr):
        cand = thr + jnp.left_shift(jnp.int32(1), 31 - it)
        return jnp.where(count_ge(cand) >= topk, cand, thr)

    thr = lax.fori_loop(0, 32, bit_body, jnp.full((t, 1), INT_MIN, I32))

    def tie_counts():
        def body(j, c):
            k = keys_ref[j]
            return (c[0] + jnp.where(k > thr, 1, 0), c[1] + jnp.where(k == thr, 1, 0))
        z = jnp.zeros((t, t), I32)
        gt, eq = lax.fori_loop(0, i + 1, body, (z, z))
        return jnp.sum(gt, axis=1, keepdims=True), jnp.sum(eq, axis=1, keepdims=True)

    n_gt, n_eq = tie_counts()
    need = topk - n_gt
    excess = (n_eq > need) & (thr > NEG_KEY)
    any_excess = jnp.max(jnp.where(excess, 1, 0))

    @pl.when(any_excess > 0)
    def _():
        tri = (row <= col).astype(BF16)

        def body(j, seen):
            k = keys_ref[j]
            eq = k == thr
            rank = seen + _dot(eq.astype(BF16), tri).astype(I32)
            drop = eq & (rank > need) & excess
            keys_ref[j] = jnp.where(drop, INT_MIN, k)
            return seen + jnp.sum(jnp.where(eq, 1, 0), axis=1, keepdims=True)

        lax.fori_loop(0, i + 1, body, jnp.zeros((t, 1), I32))

    thr = jnp.maximum(thr, NEG_KEY + 1)

    q = q_ref[0]
    qs = []
    for g in range(B_KV_HEADS):
        parts = []
        for r in range(B_GROUP):
            hq = q[:, (g * B_GROUP + r) * HEAD_DIM:(g * B_GROUP + r + 1) * HEAD_DIM]
            z = jnp.zeros_like(hq)
            parts.append(jnp.concatenate([hq, z] if g == 0 else [z, hq], axis=1))
        qs.append(jnp.concatenate(parts, axis=0))
    _init_softmax_state(m_ref, l_ref, acc_ref)

    def step(j, near=None):
        kvb = kv_ref[0, pl.ds(pl.multiple_of(j * t, t), t), :]
        k = kvb[:, :LANES]
        v = kvb[:, LANES:]
        madd = jnp.where(keys_ref[j] >= thr, 0.0, MASKED)
        for g in range(B_KV_HEADS):
            s = _dot_nt(qs[g], k).reshape(B_GROUP, t, t)
            if near is None:
                s = s + madd[None]
            else:
                s = s + (bias_ref[g * B_GROUP:(g + 1) * B_GROUP, near] + madd[None])
            _softmax_step(s.reshape(B_GROUP * t, t), v, m_ref.at[g], l_ref.at[g], acc_ref.at[g])

    def far(j, carry):
        step(j)
        return carry

    lax.fori_loop(0, i - 1, far, 0)

    @pl.when(i > 0)
    def _():
        step(i - 1, near=1)

    step(i, near=0)

    outs = []
    for g in range(B_KV_HEADS):
        o = acc_ref[g] / l_ref[g]
        for r in range(B_GROUP):
            outs.append(o[r * t:(r + 1) * t, g * HEAD_DIM:(g + 1) * HEAD_DIM])
    o_ref[0] = jnp.concatenate(outs, axis=1).astype(BF16)


def _dsa_call(iq, iw, ik, bq, bkv, bias, t, topk):
    bsz, s, _ = bq.shape
    nq = s // t
    rows = B_GROUP * t
    return pl.pallas_call(
        functools.partial(_dsa_kernel, t=t, topk=topk),
        grid=(bsz, nq),
        in_specs=[pl.BlockSpec((1, t, iq.shape[2]), lambda b, i: (b, i, 0)),
                  pl.BlockSpec((1, t, iw.shape[2]), lambda b, i: (b, i, 0)),
                  pl.BlockSpec((1, s, ik.shape[2]), lambda b, i: (b, 0, 0)),
                  pl.BlockSpec((1, t, bq.shape[2]), lambda b, i: (b, i, 0)),
                  pl.BlockSpec((1, s, bkv.shape[2]), lambda b, i: (b, 0, 0)),
                  pl.BlockSpec(bias.shape, lambda b, i: (0, 0, 0, 0))],
        out_specs=pl.BlockSpec((1, t, B_HEADS * HEAD_DIM), lambda b, i: (b, i, 0)),
        out_shape=jax.ShapeDtypeStruct((bsz, s, B_HEADS * HEAD_DIM), BF16),
        scratch_shapes=[pltpu.VMEM((nq, t, t), I32),
                        pltpu.VMEM((B_KV_HEADS, rows, 1), F32),
                        pltpu.VMEM((B_KV_HEADS, rows, 1), F32),
                        pltpu.VMEM((B_KV_HEADS, rows, LANES), F32)],
        compiler_params=pltpu.CompilerParams(dimension_semantics=("arbitrary", "arbitrary"),
                                             vmem_limit_bytes=VMEM_LIMIT),
        name="dsa_attn",
    )(iq, iw, ik, bq, bkv, bias)


def _shift_rows(u, prev8, k):
    r = pltpu.roll(u, k, axis=0)
    p = pltpu.roll(prev8, k, axis=0)
    sub = lax.broadcasted_iota(I32, p.shape, 0)
    head = jnp.where(sub < k, p, r[:8])
    return jnp.concatenate([head, r[8:]], axis=0)


def _ffn_kernel(x_ref, oa_ref, ob_ref, mod_ref, g_ref, wo_ref, wg_ref, wv_ref, cwg_ref, cwv_ref,
                cbg_ref, cbv_ref, wd_ref, o_ref, carry_ref, h_ref, y_ref, *, nf):
    i = pl.program_id(1)
    x = x_ref[0]
    mixed = _dot(oa_ref[0], wo_ref[0]) + _dot(ob_ref[0], wo_ref[1])
    x1 = x + mod_ref[0, 2:3, :] * mixed
    h_ref[...] = _modulated_norm(x1, g_ref[...], mod_ref[0, 4:5, :], mod_ref[0, 3:4, :]).astype(BF16)
    y_ref[...] = x1

    @pl.when(i == 0)
    def _():
        carry_ref[...] = jnp.zeros(carry_ref.shape, F32)

    def conv(u, prev8, cw, cb):
        return (cb + cw[2:3] * u + cw[1:2] * _shift_rows(u, prev8, 1)
                + cw[0:1] * _shift_rows(u, prev8, 2))

    def body(f, carry):
        h = h_ref[...]
        ug = _dot(h, wg_ref[f])
        uv = _dot(h, wv_ref[f])
        pg = carry_ref[f, 0]
        pv = carry_ref[f, 1]
        carry_ref[f, 0] = ug[-8:]
        carry_ref[f, 1] = uv[-8:]
        yg = conv(ug, pg, cwg_ref[f], cbg_ref[f])
        yv = conv(uv, pv, cwv_ref[f], cbv_ref[f])
        a = (yg / (1.0 + jnp.exp(-yg))) * yv
        y_ref[...] += mod_ref[0, 5:6, :] * _dot(a.astype(BF16), wd_ref[f])
        return carry

    lax.fori_loop(0, nf, body, 0)
    o_ref[0] = y_ref[...]


def _ffn_call(x, oa, ob, mod, g_ffn, w_out, w_up, conv_w, conv_b, w_down, ts, fc):
    bsz, s, d = x.shape
    dff = w_down.shape[0]
    nf = dff // fc
    na = oa.shape[2]
    wo = w_out.astype(BF16).reshape(2, na, d)
    wup = w_up.astype(BF16)
    wg = wup[:, :dff].reshape(d, nf, fc).transpose(1, 0, 2)
    wv = wup[:, dff:].reshape(d, nf, fc).transpose(1, 0, 2)
    cwg = conv_w[:, :dff].reshape(CONV_W, nf, fc).transpose(1, 0, 2)
    cwv = conv_w[:, dff:].reshape(CONV_W, nf, fc).transpose(1, 0, 2)
    cbg = conv_b[:dff].reshape(nf, 1, fc)
    cbv = conv_b[dff:].reshape(nf, 1, fc)
    wd = w_down.astype(BF16).reshape(nf, fc, d)

    def full(a):
        return pl.BlockSpec(a.shape, lambda b, i: (0,) * a.ndim, pipeline_mode=pl.Buffered(1))

    def tok(n):
        return pl.BlockSpec((1, ts, n), lambda b, i: (b, i, 0))

    consts = [wo, wg, wv, cwg, cwv, cbg, cbv, wd]
    return pl.pallas_call(
        functools.partial(_ffn_kernel, nf=nf),
        grid=(bsz, s // ts),
        in_specs=[tok(d), tok(na), tok(ob.shape[2]),
                  pl.BlockSpec((1, 6, d), lambda b, i: (b, 0, 0)), full(g_ffn)]
                 + [full(a) for a in consts],
        out_specs=tok(d),
        out_shape=jax.ShapeDtypeStruct((bsz, s, d), F32),
        scratch_shapes=[pltpu.VMEM((nf, 2, 8, fc), F32), pltpu.VMEM((ts, d), BF16),
                        pltpu.VMEM((ts, d), F32)],
        compiler_params=pltpu.CompilerParams(dimension_semantics=("arbitrary", "arbitrary"),
                                             vmem_limit_bytes=VMEM_LIMIT),
        name="outproj_ffn",
    )(x, oa, ob, mod, g_ffn, *consts)


def kernel(x, c, w_ada, b_ada, g_attn, w_in, q_norm_a, k_norm_a, q_norm_b, k_norm_b, lam_vecs,
           subln_a, w_out, g_ffn, w_up, conv_w, conv_b, w_down, rel_bias):
    bsz, s, d = x.shape
    t = min(ATT_T, s)
    ts = min(512, s)
    topk = min(TOPK_MAX, s // 4)
    scale = HEAD_DIM ** -0.5
    for l in range(w_ada.shape[0]):
        mod = _ada_call(c, w_ada[l], b_ada[l]).reshape(bsz, 6, d)
        bias = _bias_call(rel_bias, t)
        gqa = (jnp.tile(q_norm_a[l], 2 * A_HEADS) * scale)[None]
        gka = jnp.tile(k_norm_a[l], 2 * A_HEADS)[None]
        gqb = (jnp.tile(q_norm_b[l], B_HEADS) * scale)[None]
        gkb = jnp.tile(k_norm_b[l], B_KV_HEADS)[None]
        aq, ak, av, bq, bkv, iq, ik, iw = _inproj_call(
            x, mod, g_attn[l][None], w_in[l], gqa, gka, gqb, gkb, ts)
        lam_init = 0.8 - 0.6 * math.exp(-0.3 * l)
        o_a = _diff_call(lam_vecs[l], subln_a[l][None], aq, ak, av, bias[:A_HEADS], t, lam_init)
        o_b = _dsa_call(iq, iw, ik, bq, bkv, bias[A_HEADS:], t, topk)
        x = _ffn_call(x, o_a, o_b, mod, g_ffn[l][None], w_out[l], w_up[l], conv_w[l], conv_b[l],
                      w_down[l], ts, 256)
    return x
```

```python
import functools
import math

import jax
import jax.numpy as jnp
from jax import lax
from jax.experimental import pallas as pl
from jax.experimental.pallas import tpu as pltpu

F32 = jnp.float32
BF16 = jnp.bfloat16
I32 = jnp.int32

HEAD_DIM = 64
A_HEADS = 4
A_VDIM = 2 * HEAD_DIM
B_HEADS = 8
B_KV_HEADS = 2
B_GROUP = B_HEADS // B_KV_HEADS
IDX_HEADS = 8
IDX_DIM = 64
TOPK_MAX = 256
N_BUCKETS = 32
MAX_DISTANCE = 128
CONV_W = 3
EPS = 1e-6

LANES = 128
ATT_T = 256
FAR_BLOCKS = 4
MASKED = -1e30
LOG2E = math.log2(math.e)
L_TINY = 2.0 ** -60
NEG_KEY = -2139095041
INT_MIN = -2147483648
VMEM_LIMIT = 56 * 1024 * 1024


def _dot(a, b):
    return jnp.dot(a, b, preferred_element_type=F32)


def _dot_nt(a, b):
    return lax.dot_general(a, b, (((1,), (1,)), ((), ())), preferred_element_type=F32)


def _split_bf16(a):
    hi = a.astype(BF16)
    lo = (a - hi.astype(F32)).astype(BF16)
    return hi, lo


def _ada_kernel(c_ref, w_ref, b_ref, o_ref):
    c = c_ref[...]
    a = c / (1.0 + jnp.exp(-c))
    a_hi, a_lo = _split_bf16(a)
    w_hi, w_lo = _split_bf16(w_ref[...])
    o_ref[...] = _dot(a_hi, w_hi) + _dot(a_hi, w_lo) + _dot(a_lo, w_hi) + b_ref[...]


def _ada_call(c, w, b):
    bsz, d = c.shape
    n = w.shape[1]
    tn = 1024
    return pl.pallas_call(
        _ada_kernel,
        grid=(n // tn,),
        in_specs=[pl.BlockSpec((bsz, d), lambda j: (0, 0)),
                  pl.BlockSpec((d, tn), lambda j: (0, j)),
                  pl.BlockSpec((1, tn), lambda j: (0, j))],
        out_specs=pl.BlockSpec((bsz, tn), lambda j: (0, j)),
        out_shape=jax.ShapeDtypeStruct((bsz, n), F32),
        compiler_params=pltpu.CompilerParams(dimension_semantics=("arbitrary",),
                                             vmem_limit_bytes=VMEM_LIMIT),
        name="ada_mod",
    )(c, w, b.reshape(1, n))


def _bias_kernel(tab_ref, tabv_ref, gains_ref, o_ref, bound_ref, *, t):
    h = pl.program_id(0)
    max_exact = N_BUCKETS // 2
    row = lax.broadcasted_iota(I32, (t, t), 0)
    col = lax.broadcasted_iota(I32, (t, t), 1)
    last = tab_ref[N_BUCKETS - 1, h]
    for d in range(2):
        n = jnp.maximum(row - col + d * t, 0)
        nf = jnp.maximum(n, 1).astype(F32)
        large = max_exact + (jnp.log(nf / max_exact) / math.log(MAX_DISTANCE / max_exact)
                             * (N_BUCKETS - max_exact)).astype(I32)
        large = jnp.minimum(large, N_BUCKETS - 1)
        bucket = jnp.where(n < max_exact, n, large)
        acc = jnp.zeros((t, t), F32)
        for k in range(N_BUCKETS - 1):
            acc = jnp.where(bucket == k, (tab_ref[k, h] - last) * LOG2E, acc)
        if d == 0:
            acc = jnp.where(row >= col, acc, MASKED)
        o_ref[0, d] = acc
    tab = tabv_ref[...]
    rel = (tab - tab[N_BUCKETS - 1:N_BUCKETS, :]) * LOG2E
    hcol = lax.broadcasted_iota(I32, rel.shape, 1)
    g = jnp.max(jnp.abs(gains_ref[...]), axis=1, keepdims=True)
    for grp in range(2):
        in_grp = (hcol < A_HEADS) if grp == 0 else (hcol >= A_HEADS)
        bmax = jnp.max(jnp.where(in_grp, rel, 0.0), keepdims=True)
        qk = g[2 * grp:2 * grp + 1] * g[2 * grp + 1:2 * grp + 2] * (HEAD_DIM ** 0.5 * LOG2E)
        bound_ref[grp] = jnp.broadcast_to(qk + bmax, bound_ref.shape[1:])


def _bias_call(rel_bias, gains, t):
    nh = rel_bias.shape[1]
    return pl.pallas_call(
        functools.partial(_bias_kernel, t=t),
        grid=(nh,),
        in_specs=[pl.BlockSpec(memory_space=pltpu.SMEM),
                  pl.BlockSpec(rel_bias.shape, lambda h: (0, 0)),
                  pl.BlockSpec(gains.shape, lambda h: (0, 0))],
        out_specs=[pl.BlockSpec((1, 2, t, t), lambda h: (h, 0, 0, 0)),
                   pl.BlockSpec((2, 8, LANES), lambda h: (0, 0, 0))],
        out_shape=[jax.ShapeDtypeStruct((nh, 2, t, t), F32),
                   jax.ShapeDtypeStruct((2, 8, LANES), F32)],
        compiler_params=pltpu.CompilerParams(dimension_semantics=("arbitrary",)),
        name="bias_tiles",
    )(rel_bias, rel_bias, gains)


def _modulated_norm(x, g, scale, shift):
    y = x * lax.rsqrt(jnp.mean(x * x, axis=-1, keepdims=True) + EPS)
    return (y * g) * (1.0 + scale) + shift


def _head_norm(y, gain, bd):
    ss = _dot((y * y).astype(BF16), bd)
    return y * lax.rsqrt(ss * (1.0 / HEAD_DIM) + EPS) * gain


def _inproj_kernel(x_ref, mod_ref, g_ref, waq_ref, wak_ref, wav_ref, wbq_ref, wbkv_ref, wiq_ref,
                   wik_ref, gqa_ref, gka_ref, gqb_ref, gkb_ref, bd_ref,
                   aq_ref, ak_ref, av_ref, bq_ref, bkv_ref, iq_ref, ik_ref, iw_ref):
    x = x_ref[0]
    h = _modulated_norm(x, g_ref[...], mod_ref[0, 1:2, :], mod_ref[0, 0:1, :]).astype(BF16)
    bd = bd_ref[...]
    aq_ref[0] = _head_norm(_dot(h, waq_ref[...]), gqa_ref[...], bd).astype(BF16)
    ak_ref[0] = _head_norm(_dot(h, wak_ref[...]), gka_ref[...], bd).astype(BF16)
    av_ref[0] = _dot(h, wav_ref[...]).astype(BF16)
    bq_ref[0] = _head_norm(_dot(h, wbq_ref[...]), gqb_ref[...], bd).astype(BF16)
    kv = _dot(h, wbkv_ref[...])
    nk = B_KV_HEADS * HEAD_DIM
    bk = _head_norm(kv[:, :nk], gkb_ref[...], bd[:nk, :nk])
    bkv_ref[0] = jnp.concatenate([bk, kv[:, nk:]], axis=1).astype(BF16)
    iq_ref[0] = _dot(h, wiq_ref[...]).astype(BF16)
    ikw = _dot(h, wik_ref[...])
    ik_ref[0] = ikw[:, :IDX_DIM].astype(BF16)
    iw_ref[0] = ikw[:, IDX_DIM:IDX_DIM + IDX_HEADS] * ((IDX_HEADS * IDX_DIM) ** -0.5)


def _inproj_call(x, mod, g_attn, w_in, gqa, gka, gqb, gkb, ts):
    bsz, s, d = x.shape
    na = A_HEADS * 2 * HEAD_DIM
    nb = B_HEADS * HEAD_DIM
    nkv = B_KV_HEADS * HEAD_DIM
    ni = IDX_HEADS * IDX_DIM
    o = 0
    parts = []
    for width in (na, na, A_HEADS * A_VDIM, nb, 2 * nkv, ni, IDX_DIM + IDX_HEADS):
        parts.append(w_in[:, o:o + width].astype(BF16))
        o += width
    parts[-1] = jnp.pad(parts[-1], ((0, 0), (0, LANES - parts[-1].shape[1])))
    grp = jnp.arange(na) // HEAD_DIM
    bd = (grp[:, None] == grp[None, :]).astype(BF16)

    def full(a):
        return pl.BlockSpec(a.shape, lambda b, i: (0,) * a.ndim)

    def tok(n):
        return pl.BlockSpec((1, ts, n), lambda b, i: (b, i, 0))

    consts = parts + [gqa, gka, gqb, gkb, bd]
    out_w = [(na, BF16), (na, BF16), (A_HEADS * A_VDIM, BF16), (nb, BF16), (2 * nkv, BF16),
             (ni, BF16), (IDX_DIM, BF16), (IDX_HEADS, F32)]
    return pl.pallas_call(
        _inproj_kernel,
        grid=(bsz, s // ts),
        in_specs=[tok(d), pl.BlockSpec((1, 6, d), lambda b, i: (b, 0, 0)), full(g_attn)]
                 + [full(a) for a in consts],
        out_specs=[tok(n) for n, _ in out_w],
        out_shape=[jax.ShapeDtypeStruct((bsz, s, n), dt) for n, dt in out_w],
        compiler_params=pltpu.CompilerParams(dimension_semantics=("arbitrary", "arbitrary"),
                                             vmem_limit_bytes=VMEM_LIMIT),
        name="in_proj",
    )(x, mod, g_attn, *consts)


def _diff_kernel(lam_ref, g_ref, bound_ref, q_ref, k_ref, v_ref, bias_ref, o_ref, acc_ref, *,
                 t, lam_init):
    i = pl.program_id(2)
    q = q_ref[0]
    lane = lax.broadcasted_iota(I32, q.shape, 1)
    zero = jnp.zeros_like(q)
    qq = jnp.concatenate([jnp.where(lane < HEAD_DIM, q, zero),
                          jnp.where(lane >= HEAD_DIM, q, zero)], axis=0)

    def logits(start, width):
        k = k_ref[0, pl.ds(pl.multiple_of(start, t), width), :]
        return _dot_nt(qq, k)

    def v_ones(start, width):
        v = v_ref[0, pl.ds(pl.multiple_of(start, t), width), :]
        return jnp.concatenate([v, jnp.ones((width, LANES), BF16)], axis=1)

    def near_logits():
        b = jnp.concatenate([bias_ref[0, 1], bias_ref[0, 0]], axis=1)
        return logits((i - 1) * t, 2 * t) + jnp.concatenate([b, b], axis=0)

    def first_logits():
        b = bias_ref[0, 0]
        return logits(0, t) + jnp.concatenate([b, b], axis=0)

    def for_far_blocks(fn):
        nfar = jnp.maximum(i - 1, 0)
        nsup = nfar // FAR_BLOCKS

        def sup(j, c):
            fn(j * (FAR_BLOCKS * t), FAR_BLOCKS * t)
            return c

        def single(j, c):
            fn(j * t, t)
            return c

        lax.fori_loop(0, nsup, sup, 0)
        lax.fori_loop(nsup * FAR_BLOCKS, nfar, single, 0)

    def accumulate(m):
        acc_ref[...] = jnp.zeros(acc_ref.shape, F32)

        def far(start, width):
            p = jnp.exp2(logits(start, width) - m).astype(BF16)
            acc_ref[...] += _dot(p, v_ones(start, width))

        for_far_blocks(far)

        @pl.when(i > 0)
        def _():
            p = jnp.exp2(near_logits() - m).astype(BF16)
            acc_ref[...] += _dot(p, v_ones((i - 1) * t, 2 * t))

        @pl.when(i == 0)
        def _():
            p = jnp.exp2(first_logits() - m).astype(BF16)
            acc_ref[...] += _dot(p, v_ones(0, t))

    accumulate(bound_ref[0, 0:1, 0:1])

    @pl.when(jnp.min(acc_ref[:, LANES:LANES + 1]) < L_TINY)
    def _():
        acc_ref[...] = jnp.full(acc_ref.shape, MASKED, F32)

        def far(start, width):
            s = logits(start, width)
            for c in range(width // t):
                acc_ref[...] = jnp.maximum(acc_ref[...], s[:, c * t:(c + 1) * t])

        for_far_blocks(far)

        @pl.when(i > 0)
        def _():
            s = near_logits()
            acc_ref[...] = jnp.maximum(acc_ref[...], jnp.maximum(s[:, :t], s[:, t:]))

        @pl.when(i == 0)
        def _():
            acc_ref[...] = jnp.maximum(acc_ref[...], first_logits())

        accumulate(jnp.max(acc_ref[...], axis=1, keepdims=True))

    lv = lam_ref[...]
    lam = (jnp.exp(jnp.sum(lv[0:1] * lv[1:2], axis=1, keepdims=True))
           - jnp.exp(jnp.sum(lv[2:3] * lv[3:4], axis=1, keepdims=True)) + lam_init)
    acc = acc_ref[...]
    o = acc[:, :LANES] / acc[:, LANES:]
    o = o[:t] - lam * o[t:]
    o = o * lax.rsqrt(jnp.mean(o * o, axis=-1, keepdims=True) + EPS)
    o_ref[0] = (o * g_ref[...] * (1.0 - lam_init)).astype(BF16)


def _diff_call(lam_vecs, subln, bound, aq, ak, av, bias, t, lam_init):
    bsz, s, _ = aq.shape
    assert t == 2 * LANES
    return pl.pallas_call(
        functools.partial(_diff_kernel, t=t, lam_init=lam_init),
        grid=(bsz, A_HEADS, s // t),
        in_specs=[pl.BlockSpec(lam_vecs.shape, lambda b, h, i: (0, 0)),
                  pl.BlockSpec(subln.shape, lambda b, h, i: (0, 0)),
                  pl.BlockSpec((1, 8, LANES), lambda b, h, i: (0, 0, 0)),
                  pl.BlockSpec((1, t, A_VDIM), lambda b, h, i: (b, i, h)),
                  pl.BlockSpec((1, s, A_VDIM), lambda b, h, i: (b, 0, h)),
                  pl.BlockSpec((1, s, A_VDIM), lambda b, h, i: (b, 0, h)),
                  pl.BlockSpec((1, 2, t, t), lambda b, h, i: (h, 0, 0, 0))],
        out_specs=pl.BlockSpec((1, t, A_VDIM), lambda b, h, i: (b, i, h)),
        out_shape=jax.ShapeDtypeStruct((bsz, s, A_HEADS * A_VDIM), BF16),
        scratch_shapes=[pltpu.VMEM((2 * t, 2 * LANES), F32)],
        compiler_params=pltpu.CompilerParams(
            dimension_semantics=("arbitrary", "arbitrary", "arbitrary"),
            vmem_limit_bytes=VMEM_LIMIT),
        name="diff_attn",
    )(lam_vecs, subln, bound, aq, ak, av, bias)


def _sortable_key(x):
    b = pltpu.bitcast(x, I32)
    return b ^ ((b >> 31) & 0x7FFFFFFF)


def _dsa_kernel(bound_ref, iq_ref, iw_ref, ik_ref, q_ref, kv_ref, bias_ref, o_ref,
                keys_ref, wb_ref, thr_ref, acc_ref, *, t, topk):
    i = pl.program_id(1)
    row = lax.broadcasted_iota(I32, (t, t), 0)
    col = lax.broadcasted_iota(I32, (t, t), 1)

    iq = iq_ref[0]
    iw = iw_ref[0]
    iqs = [iq[:, h * IDX_DIM:(h + 1) * IDX_DIM] for h in range(IDX_HEADS)]
    for h in range(IDX_HEADS):
        wb_ref[h] = jnp.broadcast_to(iw[:, h:h + 1], (t, t))

    def score_block(j, causal):
        kc = ik_ref[0, pl.ds(pl.multiple_of(j * t, t), t), :]
        acc = jnp.zeros((t, t), F32)
        for h in range(IDX_HEADS):
            acc = acc + wb_ref[h] * jnp.maximum(_dot_nt(iqs[h], kc), 0.0)
        if causal:
            acc = jnp.where(row >= col, acc, -jnp.inf)
        keys_ref[j] = _sortable_key(acc)

    def score_body(j, carry):
        score_block(j, False)
        return carry

    lax.fori_loop(0, i, score_body, 0)
    score_block(i, True)

    def lane_total(cnt):
        return jnp.broadcast_to(jnp.sum(cnt, axis=1, keepdims=True), cnt.shape)

    def count(pred):
        def body(j, cnt):
            k = keys_ref[j]
            return (cnt + jnp.where(pred(k[:, :LANES]), 1, 0) + jnp.where(pred(k[:, LANES:]), 1, 0))
        return lane_total(lax.fori_loop(0, i + 1, body, jnp.zeros((t, LANES), I32)))

    def bit_body(it, thr):
        cand = thr + jnp.left_shift(jnp.int32(1), 31 - it)
        return jnp.where(count(lambda k: k >= cand) >= topk, cand, thr)

    thr = lax.fori_loop(0, 32, bit_body, jnp.full((t, LANES), INT_MIN, I32))

    n_gt = count(lambda k: k > thr)
    n_eq = count(lambda k: k == thr)
    need = topk - n_gt
    excess = (n_eq > need) & (thr > NEG_KEY)
    thr_ref[...] = jnp.concatenate([thr, thr], axis=1)

    @pl.when(jnp.max(jnp.where(excess, 1, 0)) > 0)
    def _():
        tri = (row <= col).astype(BF16)
        thr2 = thr_ref[...]
        need2 = jnp.concatenate([need, need], axis=1)
        excess2 = jnp.concatenate([excess, excess], axis=1)

        def body(j, seen):
            k = keys_ref[j]
            eq = k == thr2
            eqf = jnp.where(eq, 1.0, 0.0)
            rank = seen + _dot(eqf.astype(BF16), tri).astype(I32)
            keys_ref[j] = jnp.where(eq & (rank > need2) & excess2, INT_MIN, k)
            return seen + jnp.sum(eqf, axis=1, keepdims=True).astype(I32)

        lax.fori_loop(0, i + 1, body, jnp.zeros((t, 1), I32))

    thr_ref[...] = jnp.maximum(thr_ref[...], NEG_KEY + 1)

    q = q_ref[0]
    qs = []
    for g in range(B_KV_HEADS):
        parts = []
        for r in range(B_GROUP):
            hq = q[:, (g * B_GROUP + r) * HEAD_DIM:(g * B_GROUP + r + 1) * HEAD_DIM]
            z = jnp.zeros_like(hq)
            parts.append(jnp.concatenate([hq, z] if g == 0 else [z, hq], axis=1))
        qs.append(jnp.concatenate(parts, axis=0))

    def logits(j, g, near):
        k = kv_ref[0, pl.ds(pl.multiple_of(j * t, t), t), :LANES]
        madd = jnp.where(keys_ref[j] >= thr_ref[...], 0.0, MASKED)
        s = _dot_nt(qs[g], k).reshape(B_GROUP, t, t)
        if near is None:
            s = s + madd[None]
        else:
            s = s + (bias_ref[g * B_GROUP:(g + 1) * B_GROUP, near] + madd[None])
        return s.reshape(B_GROUP * t, t)

    def for_blocks(fn):
        def far(j, c):
            fn(j, None)
            return c

        lax.fori_loop(0, i - 1, far, 0)

        @pl.when(i > 0)
        def _():
            fn(i - 1, 1)

        fn(i, 0)

    def accumulate(ms):
        acc_ref[...] = jnp.zeros(acc_ref.shape, F32)

        def step(j, near):
            v = kv_ref[0, pl.ds(pl.multiple_of(j * t, t), t), LANES:]
            vx = jnp.concatenate([v, jnp.ones((t, LANES), BF16)], axis=1)
            for g in range(B_KV_HEADS):
                p = jnp.exp2(logits(j, g, near) - ms[g]).astype(BF16)
                acc_ref[g] += _dot(p, vx)

        for_blocks(step)

    m0 = bound_ref[0, 0:1, 0:1]
    accumulate([m0, m0])

    @pl.when(jnp.min(acc_ref[:, :, LANES:LANES + 1]) < L_TINY)
    def _():
        acc_ref[...] = jnp.full(acc_ref.shape, MASKED, F32)

        def step(j, near):
            for g in range(B_KV_HEADS):
                acc_ref[g] = jnp.maximum(acc_ref[g], logits(j, g, near))

        for_blocks(step)
        accumulate([jnp.max(acc_ref[g], axis=1, keepdims=True) for g in range(B_KV_HEADS)])

    outs = []
    for g in range(B_KV_HEADS):
        a = acc_ref[g]
        o = a[:, g * HEAD_DIM:(g + 1) * HEAD_DIM] / a[:, LANES:LANES + HEAD_DIM]
        for r in range(B_GROUP):
            outs.append(o[r * t:(r + 1) * t])
    o_ref[0] = jnp.concatenate(outs, axis=1).astype(BF16)


def _dsa_call(bound, iq, iw, ik, bq, bkv, bias, t, topk):
    bsz, s, _ = bq.shape
    nq = s // t
    assert t == 2 * LANES
    return pl.pallas_call(
        functools.partial(_dsa_kernel, t=t, topk=topk),
        grid=(bsz, nq),
        in_specs=[pl.BlockSpec((1, 8, LANES), lambda b, i: (1, 0, 0)),
                  pl.BlockSpec((1, t, iq.shape[2]), lambda b, i: (b, i, 0)),
                  pl.BlockSpec((1, t, iw.shape[2]), lambda b, i: (b, i, 0)),
                  pl.BlockSpec((1, s, ik.shape[2]), lambda b, i: (b, 0, 0)),
                  pl.BlockSpec((1, t, bq.shape[2]), lambda b, i: (b, i, 0)),
                  pl.BlockSpec((1, s, bkv.shape[2]), lambda b, i: (b, 0, 0)),
                  pl.BlockSpec(bias.shape, lambda b, i: (0, 0, 0, 0))],
        out_specs=pl.BlockSpec((1, t, B_HEADS * HEAD_DIM), lambda b, i: (b, i, 0)),
        out_shape=jax.ShapeDtypeStruct((bsz, s, B_HEADS * HEAD_DIM), BF16),
        scratch_shapes=[pltpu.VMEM((nq, t, t), I32),
                        pltpu.VMEM((IDX_HEADS, t, t), F32),
                        pltpu.VMEM((t, t), I32),
                        pltpu.VMEM((B_KV_HEADS, B_GROUP * t, 2 * LANES), F32)],
        compiler_params=pltpu.CompilerParams(dimension_semantics=("arbitrary", "arbitrary"),
                                             vmem_limit_bytes=VMEM_LIMIT),
        name="dsa_attn",
    )(bound, iq, iw, ik, bq, bkv, bias)


def _shift_rows(u, prev8, k):
    r = pltpu.roll(u, k, axis=0)
    p = pltpu.roll(prev8, k, axis=0)
    sub = lax.broadcasted_iota(I32, p.shape, 0)
    head = jnp.where(sub < k, p, r[:8])
    return jnp.concatenate([head, r[8:]], axis=0)


def _ffn_kernel(x_ref, oa_ref, ob_ref, mod_ref, g_ref, wo_ref, wg_ref, wv_ref, cwg_ref, cwv_ref,
                cbg_ref, cbv_ref, wd_ref, o_ref, carry_ref, h_ref, y_ref, *, nf):
    i = pl.program_id(1)
    x = x_ref[0]
    mixed = _dot(oa_ref[0], wo_ref[0]) + _dot(ob_ref[0], wo_ref[1])
    x1 = x + mod_ref[0, 2:3, :] * mixed
    h_ref[...] = _modulated_norm(x1, g_ref[...], mod_ref[0, 4:5, :], mod_ref[0, 3:4, :]).astype(BF16)
    y_ref[...] = x1

    @pl.when(i == 0)
    def _():
        carry_ref[...] = jnp.zeros(carry_ref.shape, F32)

    def conv(u, prev8, cw, cb):
        return (cb + cw[2:3] * u + cw[1:2] * _shift_rows(u, prev8, 1)
                + cw[0:1] * _shift_rows(u, prev8, 2))

    def body(f, carry):
        h = h_ref[...]
        ug = _dot(h, wg_ref[f])
        uv = _dot(h, wv_ref[f])
        pg = carry_ref[f, 0]
        pv = carry_ref[f, 1]
        carry_ref[f, 0] = ug[-8:]
        carry_ref[f, 1] = uv[-8:]
        yg = conv(ug, pg, cwg_ref[f], cbg_ref[f])
        yv = conv(uv, pv, cwv_ref[f], cbv_ref[f])
        a = (yg / (1.0 + jnp.exp(-yg))) * yv
        y_ref[...] += mod_ref[0, 5:6, :] * _dot(a.astype(BF16), wd_ref[f])
        return carry

    lax.fori_loop(0, nf, body, 0)
    o_ref[0] = y_ref[...]


def _ffn_call(x, oa, ob, mod, g_ffn, w_out, w_up, conv_w, conv_b, w_down, ts, fc):
    bsz, s, d = x.shape
    dff = w_down.shape[0]
    nf = dff // fc
    na = oa.shape[2]
    wo = w_out.astype(BF16).reshape(2, na, d)
    wup = w_up.astype(BF16)
    wg = wup[:, :dff].reshape(d, nf, fc).transpose(1, 0, 2)
    wv = wup[:, dff:].reshape(d, nf, fc).transpose(1, 0, 2)
    cwg = conv_w[:, :dff].reshape(CONV_W, nf, fc).transpose(1, 0, 2)
    cwv = conv_w[:, dff:].reshape(CONV_W, nf, fc).transpose(1, 0, 2)
    cbg = conv_b[:dff].reshape(nf, 1, fc)
    cbv = conv_b[dff:].reshape(nf, 1, fc)
    wd = w_down.astype(BF16).reshape(nf, fc, d)

    def full(a):
        return pl.BlockSpec(a.shape, lambda b, i: (0,) * a.ndim, pipeline_mode=pl.Buffered(1))

    def tok(n):
        return pl.BlockSpec((1, ts, n), lambda b, i: (b, i, 0))

    consts = [wo, wg, wv, cwg, cwv, cbg, cbv, wd]
    return pl.pallas_call(
        functools.partial(_ffn_kernel, nf=nf),
        grid=(bsz, s // ts),
        in_specs=[tok(d), tok(na), tok(ob.shape[2]),
                  pl.BlockSpec((1, 6, d), lambda b, i: (b, 0, 0)), full(g_ffn)]
                 + [full(a) for a in consts],
        out_specs=tok(d),
        out_shape=jax.ShapeDtypeStruct((bsz, s, d), F32),
        scratch_shapes=[pltpu.VMEM((nf, 2, 8, fc), F32), pltpu.VMEM((ts, d), BF16),
                        pltpu.VMEM((ts, d), F32)],
        compiler_params=pltpu.CompilerParams(dimension_semantics=("arbitrary", "arbitrary"),
                                             vmem_limit_bytes=VMEM_LIMIT),
        name="outproj_ffn",
    )(x, oa, ob, mod, g_ffn, *consts)


def kernel(x, c, w_ada, b_ada, g_attn, w_in, q_norm_a, k_norm_a, q_norm_b, k_norm_b, lam_vecs,
           subln_a, w_out, g_ffn, w_up, conv_w, conv_b, w_down, rel_bias):
    bsz, s, d = x.shape
    t = min(ATT_T, s)
    ts = min(512, s)
    topk = min(TOPK_MAX, s // 4)
    scale = HEAD_DIM ** -0.5
    for l in range(w_ada.shape[0]):
        mod = _ada_call(c, w_ada[l], b_ada[l]).reshape(bsz, 6, d)
        gains = jnp.stack([q_norm_a[l], k_norm_a[l], q_norm_b[l], k_norm_b[l]])
        bias, bound = _bias_call(rel_bias, gains, t)
        gqa = (jnp.tile(q_norm_a[l], 2 * A_HEADS) * (scale * LOG2E))[None]
        gka = jnp.tile(k_norm_a[l], 2 * A_HEADS)[None]
        gqb = (jnp.tile(q_norm_b[l], B_HEADS) * (scale * LOG2E))[None]
        gkb = jnp.tile(k_norm_b[l], B_KV_HEADS)[None]
        aq, ak, av, bq, bkv, iq, ik, iw = _inproj_call(
            x, mod, g_attn[l][None], w_in[l], gqa, gka, gqb, gkb, ts)
        lam_init = 0.8 - 0.6 * math.exp(-0.3 * l)
        o_a = _diff_call(lam_vecs[l], subln_a[l][None], bound, aq, ak, av, bias[:A_HEADS], t,
                         lam_init)
        o_b = _dsa_call(bound, iq, iw, ik, bq, bkv, bias[A_HEADS:], t, topk)
        x = _ffn_call(x, o_a, o_b, mod, g_ffn[l][None], w_out[l], w_up[l], conv_w[l], conv_b[l],
                      w_down[l], ts, 256)
    return x
```

```python
import functools
import math

import jax
import jax.numpy as jnp
from jax import lax
from jax.experimental import pallas as pl
from jax.experimental.pallas import tpu as pltpu

F32 = jnp.float32
BF16 = jnp.bfloat16
I32 = jnp.int32

HEAD_DIM = 64
A_HEADS = 4
A_VDIM = 2 * HEAD_DIM
B_HEADS = 8
B_KV_HEADS = 2
B_GROUP = B_HEADS // B_KV_HEADS
IDX_HEADS = 8
IDX_DIM = 64
TOPK_MAX = 256
N_BUCKETS = 32
MAX_DISTANCE = 128
CONV_W = 3
EPS = 1e-6

LANES = 128
ATT_T = 256
FAR_BLOCKS = 4
MASKED = -1e30
LOG2E = math.log2(math.e)
L_TINY = 2.0 ** -60
NEG_KEY = -2139095041
INT_MIN = -2147483648
VMEM_LIMIT = 56 * 1024 * 1024


def _dot(a, b):
    return jnp.dot(a, b, preferred_element_type=F32)


def _dot_nt(a, b):
    return lax.dot_general(a, b, (((1,), (1,)), ((), ())), preferred_element_type=F32)


def _split_bf16(a):
    hi = a.astype(BF16)
    lo = (a - hi.astype(F32)).astype(BF16)
    return hi, lo


def _ada_kernel(c_ref, w_ref, b_ref, o_ref):
    c = c_ref[...]
    a = c / (1.0 + jnp.exp(-c))
    a_hi, a_lo = _split_bf16(a)
    w_hi, w_lo = _split_bf16(w_ref[...])
    o_ref[...] = _dot(a_hi, w_hi) + _dot(a_hi, w_lo) + _dot(a_lo, w_hi) + b_ref[...]


def _ada_call(c, w, b):
    bsz, d = c.shape
    n = w.shape[1]
    tn = 1024
    return pl.pallas_call(
        _ada_kernel,
        grid=(n // tn,),
        in_specs=[pl.BlockSpec((bsz, d), lambda j: (0, 0)),
                  pl.BlockSpec((d, tn), lambda j: (0, j)),
                  pl.BlockSpec((1, tn), lambda j: (0, j))],
        out_specs=pl.BlockSpec((bsz, tn), lambda j: (0, j)),
        out_shape=jax.ShapeDtypeStruct((bsz, n), F32),
        compiler_params=pltpu.CompilerParams(dimension_semantics=("arbitrary",),
                                             vmem_limit_bytes=VMEM_LIMIT),
        name="ada_mod",
    )(c, w, b.reshape(1, n))


def _bias_kernel(tab_ref, tabv_ref, gains_ref, o_ref, bound_ref, *, t):
    h = pl.program_id(0)
    max_exact = N_BUCKETS // 2
    row = lax.broadcasted_iota(I32, (t, t), 0)
    col = lax.broadcasted_iota(I32, (t, t), 1)
    last = tab_ref[N_BUCKETS - 1, h]
    for d in range(2):
        n = jnp.maximum(row - col + d * t, 0)
        nf = jnp.maximum(n, 1).astype(F32)
        large = max_exact + (jnp.log(nf / max_exact) / math.log(MAX_DISTANCE / max_exact)
                             * (N_BUCKETS - max_exact)).astype(I32)
        large = jnp.minimum(large, N_BUCKETS - 1)
        bucket = jnp.where(n < max_exact, n, large)
        acc = jnp.zeros((t, t), F32)
        for k in range(N_BUCKETS - 1):
            acc = jnp.where(bucket == k, (tab_ref[k, h] - last) * LOG2E, acc)
        if d == 0:
            acc = jnp.where(row >= col, acc, MASKED)
        o_ref[0, d] = acc
    tab = tabv_ref[...]
    rel = (tab - tab[N_BUCKETS - 1:N_BUCKETS, :]) * LOG2E
    hcol = lax.broadcasted_iota(I32, rel.shape, 1)
    g = jnp.max(jnp.abs(gains_ref[...]), axis=1, keepdims=True)
    for grp in range(2):
        in_grp = (hcol < A_HEADS) if grp == 0 else (hcol >= A_HEADS)
        bmax = jnp.max(jnp.where(in_grp, rel, 0.0), keepdims=True)
        qk = g[2 * grp:2 * grp + 1] * g[2 * grp + 1:2 * grp + 2] * (HEAD_DIM ** 0.5 * LOG2E)
        bound_ref[grp] = jnp.broadcast_to(qk + bmax, bound_ref.shape[1:])


def _bias_call(rel_bias, gains, t):
    nh = rel_bias.shape[1]
    return pl.pallas_call(
        functools.partial(_bias_kernel, t=t),
        grid=(nh,),
        in_specs=[pl.BlockSpec(memory_space=pltpu.SMEM),
                  pl.BlockSpec(rel_bias.shape, lambda h: (0, 0)),
                  pl.BlockSpec(gains.shape, lambda h: (0, 0))],
        out_specs=[pl.BlockSpec((1, 2, t, t), lambda h: (h, 0, 0, 0)),
                   pl.BlockSpec((2, 8, LANES), lambda h: (0, 0, 0))],
        out_shape=[jax.ShapeDtypeStruct((nh, 2, t, t), F32),
                   jax.ShapeDtypeStruct((2, 8, LANES), F32)],
        compiler_params=pltpu.CompilerParams(dimension_semantics=("arbitrary",)),
        name="bias_tiles",
    )(rel_bias, rel_bias, gains)


def _modulated_norm(x, g, scale, shift):
    y = x * lax.rsqrt(jnp.mean(x * x, axis=-1, keepdims=True) + EPS)
    return (y * g) * (1.0 + scale) + shift


def _head_norm(y, gain, bd):
    ss = _dot((y * y).astype(BF16), bd)
    return y * lax.rsqrt(ss * (1.0 / HEAD_DIM) + EPS) * gain


def _inproj_kernel(x_ref, mod_ref, g_ref, waq_ref, wak_ref, wav_ref, wbq_ref, wbkv_ref, wiq_ref,
                   wik_ref, gqa_ref, gka_ref, gqb_ref, gkb_ref, bd_ref,
                   aq_ref, ak_ref, av_ref, bq_ref, bkv_ref, iq_ref, ik_ref, iw_ref):
    x = x_ref[0]
    h = _modulated_norm(x, g_ref[...], mod_ref[0, 1:2, :], mod_ref[0, 0:1, :]).astype(BF16)
    bd = bd_ref[...]
    aq_ref[0] = _head_norm(_dot(h, waq_ref[...]), gqa_ref[...], bd).astype(BF16)
    ak_ref[0] = _head_norm(_dot(h, wak_ref[...]), gka_ref[...], bd).astype(BF16)
    av_ref[0] = _dot(h, wav_ref[...]).astype(BF16)
    bq_ref[0] = _head_norm(_dot(h, wbq_ref[...]), gqb_ref[...], bd).astype(BF16)
    kv = _dot(h, wbkv_ref[...])
    nk = B_KV_HEADS * HEAD_DIM
    bk = _head_norm(kv[:, :nk], gkb_ref[...], bd[:nk, :nk])
    bkv_ref[0] = jnp.concatenate([bk, kv[:, nk:]], axis=1).astype(BF16)
    iq_ref[0] = _dot(h, wiq_ref[...]).astype(BF16)
    ikw = _dot(h, wik_ref[...])
    ik_ref[0] = ikw[:, :2 * IDX_DIM].astype(BF16)
    iw_ref[0] = ikw[:, 2 * IDX_DIM:2 * IDX_DIM + IDX_HEADS] * ((IDX_HEADS * IDX_DIM) ** -0.5)


def _inproj_call(x, mod, g_attn, w_in, gqa, gka, gqb, gkb, ts):
    bsz, s, d = x.shape
    na = A_HEADS * 2 * HEAD_DIM
    nb = B_HEADS * HEAD_DIM
    nkv = B_KV_HEADS * HEAD_DIM
    ni = IDX_HEADS * IDX_DIM
    o = 0
    parts = []
    for width in (na, na, A_HEADS * A_VDIM, nb, 2 * nkv, ni, IDX_DIM + IDX_HEADS):
        parts.append(w_in[:, o:o + width].astype(BF16))
        o += width
    wik = jnp.concatenate([parts[-1][:, :IDX_DIM], parts[-1]], axis=1)
    parts[-1] = jnp.pad(wik, ((0, 0), (0, 2 * LANES - wik.shape[1])))
    grp = jnp.arange(na) // HEAD_DIM
    bd = (grp[:, None] == grp[None, :]).astype(BF16)

    def full(a):
        return pl.BlockSpec(a.shape, lambda b, i: (0,) * a.ndim)

    def tok(n):
        return pl.BlockSpec((1, ts, n), lambda b, i: (b, i, 0))

    consts = parts + [gqa, gka, gqb, gkb, bd]
    out_w = [(na, BF16), (na, BF16), (A_HEADS * A_VDIM, BF16), (nb, BF16), (2 * nkv, BF16),
             (ni, BF16), (2 * IDX_DIM, BF16), (IDX_HEADS, F32)]
    return pl.pallas_call(
        _inproj_kernel,
        grid=(bsz, s // ts),
        in_specs=[tok(d), pl.BlockSpec((1, 6, d), lambda b, i: (b, 0, 0)), full(g_attn)]
                 + [full(a) for a in consts],
        out_specs=[tok(n) for n, _ in out_w],
        out_shape=[jax.ShapeDtypeStruct((bsz, s, n), dt) for n, dt in out_w],
        compiler_params=pltpu.CompilerParams(dimension_semantics=("arbitrary", "arbitrary"),
                                             vmem_limit_bytes=VMEM_LIMIT),
        name="in_proj",
    )(x, mod, g_attn, *consts)


def _diff_kernel(lam_ref, g_ref, bound_ref, q_ref, k_ref, v_ref, bias_ref, o_ref, acc_ref, *,
                 t, lam_init):
    i = pl.program_id(2)
    q = q_ref[0]
    lane = lax.broadcasted_iota(I32, q.shape, 1)
    zero = jnp.zeros_like(q)
    qq = jnp.concatenate([jnp.where(lane < HEAD_DIM, q, zero),
                          jnp.where(lane >= HEAD_DIM, q, zero)], axis=0)

    def logits(start, width):
        k = k_ref[0, pl.ds(pl.multiple_of(start, t), width), :]
        return _dot_nt(qq, k)

    def v_ones(start, width):
        v = v_ref[0, pl.ds(pl.multiple_of(start, t), width), :]
        return jnp.concatenate([v, jnp.ones((width, LANES), BF16)], axis=1)

    def near_logits():
        b = jnp.concatenate([bias_ref[0, 1], bias_ref[0, 0]], axis=1)
        return logits((i - 1) * t, 2 * t) + jnp.concatenate([b, b], axis=0)

    def first_logits():
        b = bias_ref[0, 0]
        return logits(0, t) + jnp.concatenate([b, b], axis=0)

    def for_far_blocks(fn):
        nfar = jnp.maximum(i - 1, 0)
        nsup = nfar // FAR_BLOCKS

        def sup(j, c):
            fn(j * (FAR_BLOCKS * t), FAR_BLOCKS * t)
            return c

        def single(j, c):
            fn(j * t, t)
            return c

        lax.fori_loop(0, nsup, sup, 0)
        lax.fori_loop(nsup * FAR_BLOCKS, nfar, single, 0)

    def accumulate(m):
        acc_ref[...] = jnp.zeros(acc_ref.shape, F32)

        def far(start, width):
            p = jnp.exp2(logits(start, width) - m).astype(BF16)
            acc_ref[...] += _dot(p, v_ones(start, width))

        for_far_blocks(far)

        @pl.when(i > 0)
        def _():
            p = jnp.exp2(near_logits() - m).astype(BF16)
            acc_ref[...] += _dot(p, v_ones((i - 1) * t, 2 * t))

        @pl.when(i == 0)
        def _():
            p = jnp.exp2(first_logits() - m).astype(BF16)
            acc_ref[...] += _dot(p, v_ones(0, t))

    accumulate(bound_ref[0, 0:1, 0:1])

    @pl.when(jnp.min(acc_ref[:, LANES:LANES + 1]) < L_TINY)
    def _():
        acc_ref[...] = jnp.full(acc_ref.shape, MASKED, F32)

        def far(start, width):
            s = logits(start, width)
            for c in range(width // t):
                acc_ref[...] = jnp.maximum(acc_ref[...], s[:, c * t:(c + 1) * t])

        for_far_blocks(far)

        @pl.when(i > 0)
        def _():
            s = near_logits()
            acc_ref[...] = jnp.maximum(acc_ref[...], jnp.maximum(s[:, :t], s[:, t:]))

        @pl.when(i == 0)
        def _():
            acc_ref[...] = jnp.maximum(acc_ref[...], first_logits())

        accumulate(jnp.max(acc_ref[...], axis=1, keepdims=True))

    lv = lam_ref[...]
    lam = (jnp.exp(jnp.sum(lv[0:1] * lv[1:2], axis=1, keepdims=True))
           - jnp.exp(jnp.sum(lv[2:3] * lv[3:4], axis=1, keepdims=True)) + lam_init)
    acc = acc_ref[...]
    o = acc[:, :LANES] / acc[:, LANES:]
    o = o[:t] - lam * o[t:]
    o = o * lax.rsqrt(jnp.mean(o * o, axis=-1, keepdims=True) + EPS)
    o_ref[0] = (o * g_ref[...] * (1.0 - lam_init)).astype(BF16)


def _diff_call(lam_vecs, subln, bound, aq, ak, av, bias, t, lam_init):
    bsz, s, _ = aq.shape
    assert t == 2 * LANES
    return pl.pallas_call(
        functools.partial(_diff_kernel, t=t, lam_init=lam_init),
        grid=(bsz, A_HEADS, s // t),
        in_specs=[pl.BlockSpec(lam_vecs.shape, lambda b, h, i: (0, 0)),
                  pl.BlockSpec(subln.shape, lambda b, h, i: (0, 0)),
                  pl.BlockSpec((1, 8, LANES), lambda b, h, i: (0, 0, 0)),
                  pl.BlockSpec((1, t, A_VDIM), lambda b, h, i: (b, i, h)),
                  pl.BlockSpec((1, s, A_VDIM), lambda b, h, i: (b, 0, h)),
                  pl.BlockSpec((1, s, A_VDIM), lambda b, h, i: (b, 0, h)),
                  pl.BlockSpec((1, 2, t, t), lambda b, h, i: (h, 0, 0, 0))],
        out_specs=pl.BlockSpec((1, t, A_VDIM), lambda b, h, i: (b, i, h)),
        out_shape=jax.ShapeDtypeStruct((bsz, s, A_HEADS * A_VDIM), BF16),
        scratch_shapes=[pltpu.VMEM((2 * t, 2 * LANES), F32)],
        compiler_params=pltpu.CompilerParams(
            dimension_semantics=("arbitrary", "arbitrary", "arbitrary"),
            vmem_limit_bytes=VMEM_LIMIT),
        name="diff_attn",
    )(lam_vecs, subln, bound, aq, ak, av, bias)


def _sortable_key(x):
    b = pltpu.bitcast(x, I32)
    return b ^ ((b >> 31) & 0x7FFFFFFF)


def _dsa_kernel(bound_ref, iq_ref, iw_ref, ik_ref, q_ref, kv_ref, bias_ref, o_ref,
                keys_ref, iqz_ref, wb_ref, thr_ref, acc_ref, *, t, topk):
    i = pl.program_id(1)
    row = lax.broadcasted_iota(I32, (t, t), 0)
    col = lax.broadcasted_iota(I32, (t, t), 1)

    iq = iq_ref[0]
    iw = iw_ref[0]
    lane = lax.broadcasted_iota(I32, (t, LANES), 1)
    for h in range(IDX_HEADS):
        pair = iq[:, (h // 2) * LANES:(h // 2 + 1) * LANES]
        mine = (lane < IDX_DIM) if h % 2 == 0 else (lane >= IDX_DIM)
        iqz_ref[h] = jnp.where(mine, pair, jnp.zeros_like(pair))
        wb_ref[h] = jnp.broadcast_to(iw[:, h:h + 1], (t, LANES))

    def score_block(j, causal):
        kc = ik_ref[0, pl.ds(pl.multiple_of(j * t, t), t), :]
        acc = jnp.zeros((t, t), F32)
        for h in range(IDX_HEADS):
            wb = wb_ref[h]
            wb = jnp.concatenate([wb] * (t // LANES), axis=1)
            acc = acc + wb * jnp.maximum(_dot_nt(iqz_ref[h], kc), 0.0)
        if causal:
            acc = jnp.where(row >= col, acc, -jnp.inf)
        keys_ref[j] = _sortable_key(acc)

    def score_body(j, carry):
        score_block(j, False)
        return carry

    lax.fori_loop(0, i, score_body, 0)
    score_block(i, True)

    half = t // 2

    def count(pred, ref_val):
        cnts = []
        for r in range(2):
            c = ref_val[r * half:(r + 1) * half]

            def body(j, cnt, r=r, c=c):
                k = keys_ref[j, pl.ds(r * half, half), :]
                for w in range(t // LANES):
                    cnt = cnt + jnp.where(pred(k[:, w * LANES:(w + 1) * LANES], c), 1.0, 0.0)
                return cnt

            cnts.append(lax.fori_loop(0, i + 1, body, jnp.zeros((half, LANES), F32)))
        cnt = jnp.concatenate(cnts, axis=0)
        return jnp.broadcast_to(jnp.sum(cnt, axis=1, keepdims=True), cnt.shape)

    def bit_body(it, thr):
        cand = thr + jnp.left_shift(jnp.int32(1), 31 - it)
        return jnp.where(count(lambda k, c: k >= c, cand) >= float(topk), cand, thr)

    thr = lax.fori_loop(0, 32, bit_body, jnp.full((t, LANES), INT_MIN, I32))

    n_gt = count(lambda k, c: k > c, thr)
    n_eq = count(lambda k, c: k == c, thr)
    need = float(topk) - n_gt
    excess = (n_eq > need) & (thr > NEG_KEY)
    thr_ref[...] = jnp.concatenate([thr] * (t // LANES), axis=1)

    @pl.when(jnp.max(jnp.where(excess, 1, 0)) > 0)
    def _():
        tri = (row <= col).astype(BF16)
        thr2 = thr_ref[...]
        need2 = jnp.concatenate([need] * (t // LANES), axis=1)
        excess2 = jnp.concatenate([excess] * (t // LANES), axis=1)

        def body(j, seen):
            k = keys_ref[j]
            eq = k == thr2
            eqf = jnp.where(eq, 1.0, 0.0)
            rank = seen + _dot(eqf.astype(BF16), tri)
            keys_ref[j] = jnp.where(eq & (rank > need2) & excess2, INT_MIN, k)
            return seen + jnp.sum(eqf, axis=1, keepdims=True)

        lax.fori_loop(0, i + 1, body, jnp.zeros((t, 1), F32))

    thr_ref[...] = jnp.maximum(thr_ref[...], NEG_KEY + 1)

    q = q_ref[0]
    qs = []
    for g in range(B_KV_HEADS):
        parts = []
        for r in range(B_GROUP):
            hq = q[:, (g * B_GROUP + r) * HEAD_DIM:(g * B_GROUP + r + 1) * HEAD_DIM]
            z = jnp.zeros_like(hq)
            parts.append(jnp.concatenate([hq, z] if g == 0 else [z, hq], axis=1))
        qs.append(jnp.concatenate(parts, axis=0))

    def logits(j, g, near):
        k = kv_ref[0, pl.ds(pl.multiple_of(j * t, t), t), :LANES]
        madd = jnp.where(keys_ref[j] >= thr_ref[...], 0.0, MASKED)
        s = _dot_nt(qs[g], k).reshape(B_GROUP, t, t)
        if near is None:
            s = s + madd[None]
        else:
            s = s + (bias_ref[g * B_GROUP:(g + 1) * B_GROUP, near] + madd[None])
        return s.reshape(B_GROUP * t, t)

    def for_blocks(fn):
        def far(j, c):
            fn(j, None)
            return c

        lax.fori_loop(0, i - 1, far, 0)

        @pl.when(i > 0)
        def _():
            fn(i - 1, 1)

        fn(i, 0)

    def accumulate(ms):
        acc_ref[...] = jnp.zeros(acc_ref.shape, F32)

        def step(j, near):
            v = kv_ref[0, pl.ds(pl.multiple_of(j * t, t), t), LANES:]
            vx = jnp.concatenate([v, jnp.ones((t, LANES), BF16)], axis=1)
            for g in range(B_KV_HEADS):
                p = jnp.exp2(logits(j, g, near) - ms[g]).astype(BF16)
                acc_ref[g] += _dot(p, vx)

        for_blocks(step)

    m0 = bound_ref[0, 0:1, 0:1]
    accumulate([m0, m0])

    @pl.when(jnp.min(acc_ref[:, :, LANES:LANES + 1]) < L_TINY)
    def _():
        acc_ref[...] = jnp.full(acc_ref.shape, MASKED, F32)

        def step(j, near):
            for g in range(B_KV_HEADS):
                acc_ref[g] = jnp.maximum(acc_ref[g], logits(j, g, near))

        for_blocks(step)
        accumulate([jnp.max(acc_ref[g], axis=1, keepdims=True) for g in range(B_KV_HEADS)])

    outs = []
    for g in range(B_KV_HEADS):
        a = acc_ref[g]
        o = a[:, g * HEAD_DIM:(g + 1) * HEAD_DIM] / a[:, LANES:LANES + HEAD_DIM]
        for r in range(B_GROUP):
            outs.append(o[r * t:(r + 1) * t])
    o_ref[0] = jnp.concatenate(outs, axis=1).astype(BF16)


def _dsa_call(bound, iq, iw, ik, bq, bkv, bias, t, topk):
    bsz, s, _ = bq.shape
    nq = s // t
    assert t == 2 * LANES
    return pl.pallas_call(
        functools.partial(_dsa_kernel, t=t, topk=topk),
        grid=(bsz, nq),
        in_specs=[pl.BlockSpec((1, 8, LANES), lambda b, i: (1, 0, 0)),
                  pl.BlockSpec((1, t, iq.shape[2]), lambda b, i: (b, i, 0)),
                  pl.BlockSpec((1, t, iw.shape[2]), lambda b, i: (b, i, 0)),
                  pl.BlockSpec((1, s, ik.shape[2]), lambda b, i: (b, 0, 0)),
                  pl.BlockSpec((1, t, bq.shape[2]), lambda b, i: (b, i, 0)),
                  pl.BlockSpec((1, s, bkv.shape[2]), lambda b, i: (b, 0, 0)),
                  pl.BlockSpec(bias.shape, lambda b, i: (0, 0, 0, 0))],
        out_specs=pl.BlockSpec((1, t, B_HEADS * HEAD_DIM), lambda b, i: (b, i, 0)),
        out_shape=jax.ShapeDtypeStruct((bsz, s, B_HEADS * HEAD_DIM), BF16),
        scratch_shapes=[pltpu.VMEM((nq, t, t), I32),
                        pltpu.VMEM((IDX_HEADS, t, LANES), BF16),
                        pltpu.VMEM((IDX_HEADS, t, LANES), F32),
                        pltpu.VMEM((t, t), I32),
                        pltpu.VMEM((B_KV_HEADS, B_GROUP * t, 2 * LANES), F32)],
        compiler_params=pltpu.CompilerParams(dimension_semantics=("arbitrary", "arbitrary"),
                                             vmem_limit_bytes=VMEM_LIMIT),
        name="dsa_attn",
    )(bound, iq, iw, ik, bq, bkv, bias)


def _shift_rows(u, prev8, k):
    r = pltpu.roll(u, k, axis=0)
    p = pltpu.roll(prev8, k, axis=0)
    sub = lax.broadcasted_iota(I32, p.shape, 0)
    head = jnp.where(sub < k, p, r[:8])
    return jnp.concatenate([head, r[8:]], axis=0)


def _ffn_kernel(x_ref, oa_ref, ob_ref, mod_ref, g_ref, wo_ref, wg_ref, wv_ref, cwg_ref, cwv_ref,
                cbg_ref, cbv_ref, wd_ref, o_ref, carry_ref, h_ref, y_ref, ua_ref, ub_ref, a_ref, *, nf):
    i = pl.program_id(1)
    x = x_ref[0]
    mixed = _dot(oa_ref[0], wo_ref[0]) + _dot(ob_ref[0], wo_ref[1])
    x1 = x + mod_ref[0, 2:3, :] * mixed
    h_ref[...] = _modulated_norm(x1, g_ref[...], mod_ref[0, 4:5, :], mod_ref[0, 3:4, :]).astype(BF16)
    y_ref[...] = x1

    @pl.when(i == 0)
    def _():
        carry_ref[...] = jnp.zeros(carry_ref.shape, F32)

    def conv(u, prev8, cw, cb):
        return (cb + cw[2:3] * u + cw[1:2] * _shift_rows(u, prev8, 1)
                + cw[0:1] * _shift_rows(u, prev8, 2))

    def up(f):
        h = h_ref[...]
        return _dot(h, wg_ref[f]), _dot(h, wv_ref[f])

    def gate(f, ug, uv):
        pg = carry_ref[f, 0]
        pv = carry_ref[f, 1]
        carry_ref[f, 0] = ug[-8:]
        carry_ref[f, 1] = uv[-8:]
        yg = conv(ug, pg, cwg_ref[f], cbg_ref[f])
        yv = conv(uv, pv, cwv_ref[f], cbv_ref[f])
        a_ref[f] = ((yg / (1.0 + jnp.exp(-yg))) * yv).astype(BF16)

    def stage(f, cur_ref, nxt_ref):
        gate(f, cur_ref[0], cur_ref[1])
        nxt_ref[0], nxt_ref[1] = up(f + 1)

    ua_ref[0], ua_ref[1] = up(0)

    def body(k, carry):
        stage(2 * k, ua_ref, ub_ref)
        stage(2 * k + 1, ub_ref, ua_ref)
        return carry

    assert nf % 2 == 1
    lax.fori_loop(0, (nf - 1) // 2, body, 0)
    gate(nf - 1, ua_ref[0], ua_ref[1])
    a = jnp.concatenate([a_ref[f] for f in range(nf)], axis=1)
    o_ref[0] = y_ref[...] + mod_ref[0, 5:6, :] * _dot(a, wd_ref[...])


def _ffn_call(x, oa, ob, mod, g_ffn, w_out, w_up, conv_w, conv_b, w_down, ts, fc):
    bsz, s, d = x.shape
    dff = w_down.shape[0]
    nf = dff // fc
    na = oa.shape[2]
    wo = w_out.astype(BF16).reshape(2, na, d)
    wup = w_up.astype(BF16)
    wg = wup[:, :dff].reshape(d, nf, fc).transpose(1, 0, 2)
    wv = wup[:, dff:].reshape(d, nf, fc).transpose(1, 0, 2)
    cwg = conv_w[:, :dff].reshape(CONV_W, nf, fc).transpose(1, 0, 2)
    cwv = conv_w[:, dff:].reshape(CONV_W, nf, fc).transpose(1, 0, 2)
    cbg = conv_b[:dff].reshape(nf, 1, fc)
    cbv = conv_b[dff:].reshape(nf, 1, fc)
    wd = w_down.astype(BF16)

    def full(a):
        return pl.BlockSpec(a.shape, lambda b, i: (0,) * a.ndim, pipeline_mode=pl.Buffered(1))

    def tok(n):
        return pl.BlockSpec((1, ts, n), lambda b, i: (b, i, 0))

    consts = [wo, wg, wv, cwg, cwv, cbg, cbv, wd]
    return pl.pallas_call(
        functools.partial(_ffn_kernel, nf=nf),
        grid=(bsz, s // ts),
        in_specs=[tok(d), tok(na), tok(ob.shape[2]),
                  pl.BlockSpec((1, 6, d), lambda b, i: (b, 0, 0)), full(g_ffn)]
                 + [full(a) for a in consts],
        out_specs=tok(d),
        out_shape=jax.ShapeDtypeStruct((bsz, s, d), F32),
        scratch_shapes=[pltpu.VMEM((nf, 2, 8, fc), F32), pltpu.VMEM((ts, d), BF16),
                        pltpu.VMEM((ts, d), F32), pltpu.VMEM((2, ts, fc), F32),
                        pltpu.VMEM((2, ts, fc), F32), pltpu.VMEM((nf, ts, fc), BF16)],
        compiler_params=pltpu.CompilerParams(dimension_semantics=("arbitrary", "arbitrary"),
                                             vmem_limit_bytes=VMEM_LIMIT),
        name="outproj_ffn",
    )(x, oa, ob, mod, g_ffn, *consts)


def kernel(x, c, w_ada, b_ada, g_attn, w_in, q_norm_a, k_norm_a, q_norm_b, k_norm_b, lam_vecs,
           subln_a, w_out, g_ffn, w_up, conv_w, conv_b, w_down, rel_bias):
    bsz, s, d = x.shape
    t = min(ATT_T, s)
    ts = min(512, s)
    topk = min(TOPK_MAX, s // 4)
    scale = HEAD_DIM ** -0.5
    for l in range(w_ada.shape[0]):
        mod = _ada_call(c, w_ada[l], b_ada[l]).reshape(bsz, 6, d)
        gains = jnp.stack([q_norm_a[l], k_norm_a[l], q_norm_b[l], k_norm_b[l]])
        bias, bound = _bias_call(rel_bias, gains, t)
        gqa = (jnp.tile(q_norm_a[l], 2 * A_HEADS) * (scale * LOG2E))[None]
        gka = jnp.tile(k_norm_a[l], 2 * A_HEADS)[None]
        gqb = (jnp.tile(q_norm_b[l], B_HEADS) * (scale * LOG2E))[None]
        gkb = jnp.tile(k_norm_b[l], B_KV_HEADS)[None]
        aq, ak, av, bq, bkv, iq, ik, iw = _inproj_call(
            x, mod, g_attn[l][None], w_in[l], gqa, gka, gqb, gkb, ts)
        lam_init = 0.8 - 0.6 * math.exp(-0.3 * l)
        o_a = _diff_call(lam_vecs[l], subln_a[l][None], bound, aq, ak, av, bias[:A_HEADS], t,
                         lam_init)
        o_b = _dsa_call(bound, iq, iw, ik, bq, bkv, bias[A_HEADS:], t, topk)
        x = _ffn_call(x, o_a, o_b, mod, g_ffn[l][None], w_out[l], w_up[l], conv_w[l], conv_b[l],
                      w_down[l], ts, 256)
    return x
```

```python
import functools
import math

import jax
import jax.numpy as jnp
from jax import lax
from jax.experimental import pallas as pl
from jax.experimental.pallas import tpu as pltpu

F32 = jnp.float32
BF16 = jnp.bfloat16
I32 = jnp.int32

HEAD_DIM = 64
A_HEADS = 4
A_VDIM = 2 * HEAD_DIM
B_HEADS = 8
B_KV_HEADS = 2
B_GROUP = B_HEADS // B_KV_HEADS
IDX_HEADS = 8
IDX_DIM = 64
TOPK_MAX = 256
N_BUCKETS = 32
MAX_DISTANCE = 128
CONV_W = 3
EPS = 1e-6

LANES = 128
ATT_T = 512
SEL_ROWS = 128
DIFF_T = 512
FAR_BLOCKS = 2
MASKED = -1e30
LOG2E = math.log2(math.e)
L_TINY = 2.0 ** -60
NEG_KEY = -2139095041
INT_MIN = -2147483648
VMEM_LIMIT = 56 * 1024 * 1024


def _dot(a, b):
    return jnp.dot(a, b, preferred_element_type=F32)


def _dot_nt(a, b):
    return lax.dot_general(a, b, (((1,), (1,)), ((), ())), preferred_element_type=F32)


def _split_bf16(a):
    hi = a.astype(BF16)
    lo = (a - hi.astype(F32)).astype(BF16)
    return hi, lo


def _ada_kernel(c_ref, w_ref, b_ref, o_ref):
    c = c_ref[...]
    a = c / (1.0 + jnp.exp(-c))
    a_hi, a_lo = _split_bf16(a)
    w_hi, w_lo = _split_bf16(w_ref[...])
    o_ref[...] = _dot(a_hi, w_hi) + _dot(a_hi, w_lo) + _dot(a_lo, w_hi) + b_ref[...]


def _ada_call(c, w, b):
    bsz, d = c.shape
    n = w.shape[1]
    tn = 1024
    return pl.pallas_call(
        _ada_kernel,
        grid=(n // tn,),
        in_specs=[pl.BlockSpec((bsz, d), lambda j: (0, 0)),
                  pl.BlockSpec((d, tn), lambda j: (0, j)),
                  pl.BlockSpec((1, tn), lambda j: (0, j))],
        out_specs=pl.BlockSpec((bsz, tn), lambda j: (0, j)),
        out_shape=jax.ShapeDtypeStruct((bsz, n), F32),
        compiler_params=pltpu.CompilerParams(dimension_semantics=("arbitrary",),
                                             vmem_limit_bytes=VMEM_LIMIT),
        name="ada_mod",
    )(c, w, b.reshape(1, n))


def _bias_kernel(tab_ref, tabv_ref, gains_ref, o_ref, bound_ref, *, t, h0):
    h = pl.program_id(0) + h0
    max_exact = N_BUCKETS // 2
    row = lax.broadcasted_iota(I32, (t, t), 0)
    col = lax.broadcasted_iota(I32, (t, t), 1)
    last = tab_ref[N_BUCKETS - 1, h]
    for d in range(2):
        n = jnp.maximum(row - col + d * t, 0)
        nf = jnp.maximum(n, 1).astype(F32)
        large = max_exact + (jnp.log(nf / max_exact) / math.log(MAX_DISTANCE / max_exact)
                             * (N_BUCKETS - max_exact)).astype(I32)
        large = jnp.minimum(large, N_BUCKETS - 1)
        bucket = jnp.where(n < max_exact, n, large)
        acc = jnp.zeros((t, t), F32)
        for k in range(N_BUCKETS - 1):
            acc = jnp.where(bucket == k, (tab_ref[k, h] - last) * LOG2E, acc)
        if d == 0:
            acc = jnp.where(row >= col, acc, MASKED)
        o_ref[0, d] = acc
    tab = tabv_ref[...]
    rel = (tab - tab[N_BUCKETS - 1:N_BUCKETS, :]) * LOG2E
    hcol = lax.broadcasted_iota(I32, rel.shape, 1)
    g = jnp.max(jnp.abs(gains_ref[...]), axis=1, keepdims=True)
    for grp in range(2):
        in_grp = (hcol < A_HEADS) if grp == 0 else (hcol >= A_HEADS)
        bmax = jnp.max(jnp.where(in_grp, rel, 0.0), keepdims=True)
        qk = g[2 * grp:2 * grp + 1] * g[2 * grp + 1:2 * grp + 2] * (HEAD_DIM ** 0.5 * LOG2E)
        bound_ref[grp] = jnp.broadcast_to(qk + bmax, bound_ref.shape[1:])


def _bias_call(rel_bias, gains, t, h0, nh):
    return pl.pallas_call(
        functools.partial(_bias_kernel, t=t, h0=h0),
        grid=(nh,),
        in_specs=[pl.BlockSpec(memory_space=pltpu.SMEM),
                  pl.BlockSpec(rel_bias.shape, lambda h: (0, 0)),
                  pl.BlockSpec(gains.shape, lambda h: (0, 0))],
        out_specs=[pl.BlockSpec((1, 2, t, t), lambda h: (h, 0, 0, 0)),
                   pl.BlockSpec((2, 8, LANES), lambda h: (0, 0, 0))],
        out_shape=[jax.ShapeDtypeStruct((nh, 2, t, t), F32),
                   jax.ShapeDtypeStruct((2, 8, LANES), F32)],
        compiler_params=pltpu.CompilerParams(dimension_semantics=("arbitrary",)),
        name="bias_tiles",
    )(rel_bias, rel_bias, gains)


def _modulated_norm(x, g, scale, shift):
    y = x * lax.rsqrt(jnp.mean(x * x, axis=-1, keepdims=True) + EPS)
    return (y * g) * (1.0 + scale) + shift


def _head_norm(y, gain, bd):
    ss = _dot((y * y).astype(BF16), bd)
    return y * lax.rsqrt(ss * (1.0 / HEAD_DIM) + EPS) * gain


def _inproj_kernel(x_ref, mod_ref, g_ref, waq_ref, wak_ref, wav_ref, wbq_ref, wbkv_ref, wiq_ref,
                   wik_ref, gqa_ref, gka_ref, gqb_ref, gkb_ref, bd_ref,
                   aq_ref, ak_ref, av_ref, bq_ref, bkv_ref, iq_ref, ik_ref, iw_ref):
    x = x_ref[0]
    h = _modulated_norm(x, g_ref[...], mod_ref[0, 1:2, :], mod_ref[0, 0:1, :]).astype(BF16)
    bd = bd_ref[...]
    aq_ref[0] = _head_norm(_dot(h, waq_ref[...]), gqa_ref[...], bd).astype(BF16)
    ak_ref[0] = _head_norm(_dot(h, wak_ref[...]), gka_ref[...], bd).astype(BF16)
    av_ref[0] = _dot(h, wav_ref[...]).astype(BF16)
    bq_ref[0] = _head_norm(_dot(h, wbq_ref[...]), gqb_ref[...], bd).astype(BF16)
    kv = _dot(h, wbkv_ref[...])
    nk = B_KV_HEADS * HEAD_DIM
    bk = _head_norm(kv[:, :nk], gkb_ref[...], bd[:nk, :nk])
    bkv_ref[0] = jnp.concatenate([bk, kv[:, nk:]], axis=1).astype(BF16)
    iq_ref[0] = _dot(h, wiq_ref[...]).astype(BF16)
    ikw = _dot(h, wik_ref[...])
    ik_ref[0] = ikw[:, :2 * IDX_DIM].astype(BF16)
    iw_ref[0] = ikw[:, 2 * IDX_DIM:2 * IDX_DIM + IDX_HEADS] * ((IDX_HEADS * IDX_DIM) ** -0.5)


def _inproj_call(x, mod, g_attn, w_in, gqa, gka, gqb, gkb, ts):
    bsz, s, d = x.shape
    na = A_HEADS * 2 * HEAD_DIM
    nb = B_HEADS * HEAD_DIM
    nkv = B_KV_HEADS * HEAD_DIM
    ni = IDX_HEADS * IDX_DIM
    o = 0
    parts = []
    for width in (na, na, A_HEADS * A_VDIM, nb, 2 * nkv, ni, IDX_DIM + IDX_HEADS):
        parts.append(w_in[:, o:o + width].astype(BF16))
        o += width
    wik = jnp.concatenate([parts[-1][:, :IDX_DIM], parts[-1]], axis=1)
    parts[-1] = jnp.pad(wik, ((0, 0), (0, 2 * LANES - wik.shape[1])))
    grp = jnp.arange(na) // HEAD_DIM
    bd = (grp[:, None] == grp[None, :]).astype(BF16)

    def full(a):
        return pl.BlockSpec(a.shape, lambda b, i: (0,) * a.ndim)

    def tok(n):
        return pl.BlockSpec((1, ts, n), lambda b, i: (b, i, 0))

    consts = parts + [gqa, gka, gqb, gkb, bd]
    out_w = [(na, BF16), (na, BF16), (A_HEADS * A_VDIM, BF16), (nb, BF16), (2 * nkv, BF16),
             (ni, BF16), (2 * IDX_DIM, BF16), (IDX_HEADS, F32)]
    return pl.pallas_call(
        _inproj_kernel,
        grid=(bsz, s // ts),
        in_specs=[tok(d), pl.BlockSpec((1, 6, d), lambda b, i: (b, 0, 0)), full(g_attn)]
                 + [full(a) for a in consts],
        out_specs=[tok(n) for n, _ in out_w],
        out_shape=[jax.ShapeDtypeStruct((bsz, s, n), dt) for n, dt in out_w],
        compiler_params=pltpu.CompilerParams(dimension_semantics=("arbitrary", "arbitrary"),
                                             vmem_limit_bytes=VMEM_LIMIT),
        name="in_proj",
    )(x, mod, g_attn, *consts)


def _diff_kernel(lam_ref, g_ref, bound_ref, q_ref, k_ref, v_ref, bias_ref, o_ref, acc_ref, *,
                 t, lam_init):
    i = pl.program_id(2)
    q = q_ref[0]
    lane = lax.broadcasted_iota(I32, q.shape, 1)
    zero = jnp.zeros_like(q)
    qq = jnp.concatenate([jnp.where(lane < HEAD_DIM, q, zero),
                          jnp.where(lane >= HEAD_DIM, q, zero)], axis=0)

    def logits(start, width):
        k = k_ref[0, pl.ds(pl.multiple_of(start, t), width), :]
        return _dot_nt(qq, k)

    def v_ones(start, width):
        v = v_ref[0, pl.ds(pl.multiple_of(start, t), width), :]
        return jnp.concatenate([v, jnp.ones((width, LANES), BF16)], axis=1)

    def near_logits():
        b = jnp.concatenate([bias_ref[0, 1], bias_ref[0, 0]], axis=1)
        return logits((i - 1) * t, 2 * t) + jnp.concatenate([b, b], axis=0)

    def first_logits():
        b = bias_ref[0, 0]
        return logits(0, t) + jnp.concatenate([b, b], axis=0)

    def for_far_blocks(fn):
        nfar = jnp.maximum(i - 1, 0)
        nsup = nfar // FAR_BLOCKS

        def sup(j, c):
            fn(j * (FAR_BLOCKS * t), FAR_BLOCKS * t)
            return c

        def single(j, c):
            fn(j * t, t)
            return c

        lax.fori_loop(0, nsup, sup, 0)
        lax.fori_loop(nsup * FAR_BLOCKS, nfar, single, 0)

    def accumulate(m):
        acc_ref[...] = jnp.zeros(acc_ref.shape, F32)

        def far(start, width):
            p = jnp.exp2(logits(start, width) - m).astype(BF16)
            acc_ref[...] += _dot(p, v_ones(start, width))

        for_far_blocks(far)

        @pl.when(i > 0)
        def _():
            p = jnp.exp2(near_logits() - m).astype(BF16)
            acc_ref[...] += _dot(p, v_ones((i - 1) * t, 2 * t))

        @pl.when(i == 0)
        def _():
            p = jnp.exp2(first_logits() - m).astype(BF16)
            acc_ref[...] += _dot(p, v_ones(0, t))

    accumulate(bound_ref[0, 0:1, 0:1])

    @pl.when(jnp.min(acc_ref[:, LANES:LANES + 1]) < L_TINY)
    def _():
        acc_ref[...] = jnp.full(acc_ref.shape, MASKED, F32)

        def fold_max(s):
            w = acc_ref.shape[1]
            for c in range(s.shape[1] // w):
                acc_ref[...] = jnp.maximum(acc_ref[...], s[:, c * w:(c + 1) * w])

        for_far_blocks(lambda start, width: fold_max(logits(start, width)))

        @pl.when(i > 0)
        def _():
            fold_max(near_logits())

        @pl.when(i == 0)
        def _():
            fold_max(first_logits())

        accumulate(jnp.max(acc_ref[...], axis=1, keepdims=True))

    lv = lam_ref[...]
    lam = (jnp.exp(jnp.sum(lv[0:1] * lv[1:2], axis=1, keepdims=True))
           - jnp.exp(jnp.sum(lv[2:3] * lv[3:4], axis=1, keepdims=True)) + lam_init)
    acc = acc_ref[...]
    o = acc[:, :LANES] / acc[:, LANES:]
    o = o[:t] - lam * o[t:]
    o = o * lax.rsqrt(jnp.mean(o * o, axis=-1, keepdims=True) + EPS)
    o_ref[0] = (o * g_ref[...] * (1.0 - lam_init)).astype(BF16)


def _diff_call(lam_vecs, subln, bound, aq, ak, av, bias, t, lam_init):
    bsz, s, _ = aq.shape
    assert t % (2 * LANES) == 0
    return pl.pallas_call(
        functools.partial(_diff_kernel, t=t, lam_init=lam_init),
        grid=(bsz, A_HEADS, s // t),
        in_specs=[pl.BlockSpec(lam_vecs.shape, lambda b, h, i: (0, 0)),
                  pl.BlockSpec(subln.shape, lambda b, h, i: (0, 0)),
                  pl.BlockSpec((1, 8, LANES), lambda b, h, i: (0, 0, 0)),
                  pl.BlockSpec((1, t, A_VDIM), lambda b, h, i: (b, i, h)),
                  pl.BlockSpec((1, s, A_VDIM), lambda b, h, i: (b, 0, h)),
                  pl.BlockSpec((1, s, A_VDIM), lambda b, h, i: (b, 0, h)),
                  pl.BlockSpec((1, 2, t, t), lambda b, h, i: (h, 0, 0, 0))],
        out_specs=pl.BlockSpec((1, t, A_VDIM), lambda b, h, i: (b, i, h)),
        out_shape=jax.ShapeDtypeStruct((bsz, s, A_HEADS * A_VDIM), BF16),
        scratch_shapes=[pltpu.VMEM((2 * t, 2 * LANES), F32)],
        compiler_params=pltpu.CompilerParams(
            dimension_semantics=("arbitrary", "arbitrary", "arbitrary"),
            vmem_limit_bytes=VMEM_LIMIT),
        name="diff_attn",
    )(lam_vecs, subln, bound, aq, ak, av, bias)


def _sortable_key(x):
    b = pltpu.bitcast(x, I32)
    return b ^ ((b >> 31) & 0x7FFFFFFF)


def _dsa_kernel(bound_ref, iq_ref, iw_ref, ik_ref, q_ref, kv_ref, bias_ref, o_ref,
                keys_ref, iqz_ref, wb_ref, thr_ref, acc_ref, *, t, topk):
    i = pl.program_id(1)
    row = lax.broadcasted_iota(I32, (t, t), 0)
    col = lax.broadcasted_iota(I32, (t, t), 1)

    iq = iq_ref[0]
    iw = iw_ref[0]
    lane = lax.broadcasted_iota(I32, (t, LANES), 1)
    for h in range(IDX_HEADS):
        pair = iq[:, (h // 2) * LANES:(h // 2 + 1) * LANES]
        mine = (lane < IDX_DIM) if h % 2 == 0 else (lane >= IDX_DIM)
        iqz_ref[h] = jnp.where(mine, pair, jnp.zeros_like(pair))
        wb_ref[h] = jnp.broadcast_to(iw[:, h:h + 1], (t, LANES))

    def score_block(j, causal):
        kc = ik_ref[0, pl.ds(pl.multiple_of(j * t, t), t), :]
        acc = jnp.zeros((t, t), F32)
        for h in range(IDX_HEADS):
            wb = wb_ref[h]
            wb = jnp.concatenate([wb] * (t // LANES), axis=1)
            acc = acc + wb * jnp.maximum(_dot_nt(iqz_ref[h], kc), 0.0)
        if causal:
            acc = jnp.where(row >= col, acc, -jnp.inf)
        keys_ref[j] = _sortable_key(acc)

    def score_body(j, carry):
        score_block(j, False)
        return carry

    lax.fori_loop(0, i, score_body, 0)
    score_block(i, True)

    def count(pred, ref_val):
        cnts = []
        for r in range(t // SEL_ROWS):
            c = ref_val[r * SEL_ROWS:(r + 1) * SEL_ROWS]

            def body(j, cnt, r=r, c=c):
                k = keys_ref[j, pl.ds(r * SEL_ROWS, SEL_ROWS), :]
                for w in range(t // LANES):
                    cnt = cnt + jnp.where(pred(k[:, w * LANES:(w + 1) * LANES], c), 1.0, 0.0)
                return cnt

            cnts.append(lax.fori_loop(0, i + 1, body, jnp.zeros((SEL_ROWS, LANES), F32)))
        cnt = jnp.concatenate(cnts, axis=0)
        return jnp.broadcast_to(jnp.sum(cnt, axis=1, keepdims=True), cnt.shape)

    def bit_body(it, thr):
        cand = thr + jnp.left_shift(jnp.int32(1), 31 - it)
        return jnp.where(count(lambda k, c: k >= c, cand) >= float(topk), cand, thr)

    thr = lax.fori_loop(0, 32, bit_body, jnp.full((t, LANES), INT_MIN, I32))

    n_gt = count(lambda k, c: k > c, thr)
    n_eq = count(lambda k, c: k == c, thr)
    need = float(topk) - n_gt
    excess = (n_eq > need) & (thr > NEG_KEY)
    thr_ref[...] = jnp.concatenate([thr] * (t // LANES), axis=1)

    @pl.when(jnp.max(jnp.where(excess, 1, 0)) > 0)
    def _():
        tri = (row <= col).astype(BF16)
        thr2 = thr_ref[...]
        need2 = jnp.concatenate([need] * (t // LANES), axis=1)
        excess2 = jnp.concatenate([excess] * (t // LANES), axis=1)

        def body(j, seen):
            k = keys_ref[j]
            eq = k == thr2
            eqf = jnp.where(eq, 1.0, 0.0)
            rank = seen + _dot(eqf.astype(BF16), tri)
            keys_ref[j] = jnp.where(eq & (rank > need2) & excess2, INT_MIN, k)
            return seen + jnp.sum(eqf, axis=1, keepdims=True)

        lax.fori_loop(0, i + 1, body, jnp.zeros((t, 1), F32))

    thr_ref[...] = jnp.maximum(thr_ref[...], NEG_KEY + 1)

    q = q_ref[0]
    qs = []
    for g in range(B_KV_HEADS):
        parts = []
        for r in range(B_GROUP):
            hq = q[:, (g * B_GROUP + r) * HEAD_DIM:(g * B_GROUP + r + 1) * HEAD_DIM]
            z = jnp.zeros_like(hq)
            parts.append(jnp.concatenate([hq, z] if g == 0 else [z, hq], axis=1))
        qs.append(jnp.concatenate(parts, axis=0))

    def logits(j, g, near):
        k = kv_ref[0, pl.ds(pl.multiple_of(j * t, t), t), :LANES]
        madd = jnp.where(keys_ref[j] >= thr_ref[...], 0.0, MASKED)
        s = _dot_nt(qs[g], k).reshape(B_GROUP, t, t)
        if near is None:
            s = s + madd[None]
        else:
            s = s + (bias_ref[g * B_GROUP:(g + 1) * B_GROUP, near] + madd[None])
        return s.reshape(B_GROUP * t, t)

    def for_blocks(fn):
        def far(j, c):
            fn(j, None)
            return c

        lax.fori_loop(0, i - 1, far, 0)

        @pl.when(i > 0)
        def _():
            fn(i - 1, 1)

        fn(i, 0)

    def accumulate(ms):
        acc_ref[...] = jnp.zeros(acc_ref.shape, F32)

        def step(j, near):
            v = kv_ref[0, pl.ds(pl.multiple_of(j * t, t), t), LANES:]
            vx = jnp.concatenate([v, jnp.ones((t, LANES), BF16)], axis=1)
            for g in range(B_KV_HEADS):
                p = jnp.exp2(logits(j, g, near) - ms[g]).astype(BF16)
                acc_ref[g] += _dot(p, vx)

        for_blocks(step)

    m0 = bound_ref[0, 0:1, 0:1]
    accumulate([m0, m0])

    @pl.when(jnp.min(acc_ref[:, :, LANES:LANES + 1]) < L_TINY)
    def _():
        acc_ref[...] = jnp.full(acc_ref.shape, MASKED, F32)

        def step(j, near):
            w = acc_ref.shape[2]
            for g in range(B_KV_HEADS):
                s = logits(j, g, near)
                for c in range(t // w):
                    acc_ref[g] = jnp.maximum(acc_ref[g], s[:, c * w:(c + 1) * w])

        for_blocks(step)
        accumulate([jnp.max(acc_ref[g], axis=1, keepdims=True) for g in range(B_KV_HEADS)])

    outs = []
    for g in range(B_KV_HEADS):
        a = acc_ref[g]
        o = a[:, g * HEAD_DIM:(g + 1) * HEAD_DIM] / a[:, LANES:LANES + HEAD_DIM]
        for r in range(B_GROUP):
            outs.append(o[r * t:(r + 1) * t])
    o_ref[0] = jnp.concatenate(outs, axis=1).astype(BF16)


def _dsa_call(bound, iq, iw, ik, bq, bkv, bias, t, topk):
    bsz, s, _ = bq.shape
    nq = s // t
    assert t % SEL_ROWS == 0 and t % (2 * LANES) == 0
    return pl.pallas_call(
        functools.partial(_dsa_kernel, t=t, topk=topk),
        grid=(bsz, nq),
        in_specs=[pl.BlockSpec((1, 8, LANES), lambda b, i: (1, 0, 0)),
                  pl.BlockSpec((1, t, iq.shape[2]), lambda b, i: (b, i, 0)),
                  pl.BlockSpec((1, t, iw.shape[2]), lambda b, i: (b, i, 0)),
                  pl.BlockSpec((1, s, ik.shape[2]), lambda b, i: (b, 0, 0)),
                  pl.BlockSpec((1, t, bq.shape[2]), lambda b, i: (b, i, 0)),
                  pl.BlockSpec((1, s, bkv.shape[2]), lambda b, i: (b, 0, 0)),
                  pl.BlockSpec(bias.shape, lambda b, i: (0, 0, 0, 0),
                               pipeline_mode=pl.Buffered(1))],
        out_specs=pl.BlockSpec((1, t, B_HEADS * HEAD_DIM), lambda b, i: (b, i, 0)),
        out_shape=jax.ShapeDtypeStruct((bsz, s, B_HEADS * HEAD_DIM), BF16),
        scratch_shapes=[pltpu.VMEM((nq, t, t), I32),
                        pltpu.VMEM((IDX_HEADS, t, LANES), BF16),
                        pltpu.VMEM((IDX_HEADS, t, LANES), F32),
                        pltpu.VMEM((t, t), I32),
                        pltpu.VMEM((B_KV_HEADS, B_GROUP * t, 2 * LANES), F32)],
        compiler_params=pltpu.CompilerParams(dimension_semantics=("arbitrary", "arbitrary"),
                                             vmem_limit_bytes=VMEM_LIMIT),
        name="dsa_attn",
    )(bound, iq, iw, ik, bq, bkv, bias)


def _shift_rows(u, prev8, k):
    r = pltpu.roll(u, k, axis=0)
    p = pltpu.roll(prev8, k, axis=0)
    sub = lax.broadcasted_iota(I32, p.shape, 0)
    head = jnp.where(sub < k, p, r[:8])
    return jnp.concatenate([head, r[8:]], axis=0)


def _ffn_kernel(x_ref, oa_ref, ob_ref, mod_ref, g_ref, wo_ref, wg_ref, wv_ref, cwg_ref, cwv_ref,
                cbg_ref, cbv_ref, wd_ref, o_ref, carry_ref, h_ref, y_ref, ua_ref, ub_ref, a_ref, *, nf):
    i = pl.program_id(1)
    x = x_ref[0]
    mixed = _dot(oa_ref[0], wo_ref[0]) + _dot(ob_ref[0], wo_ref[1])
    x1 = x + mod_ref[0, 2:3, :] * mixed
    h_ref[...] = _modulated_norm(x1, g_ref[...], mod_ref[0, 4:5, :], mod_ref[0, 3:4, :]).astype(BF16)
    y_ref[...] = x1

    @pl.when(i == 0)
    def _():
        carry_ref[...] = jnp.zeros(carry_ref.shape, F32)

    def conv(u, prev8, cw, cb):
        return (cb + cw[2:3] * u + cw[1:2] * _shift_rows(u, prev8, 1)
                + cw[0:1] * _shift_rows(u, prev8, 2))

    def up(f):
        h = h_ref[...]
        return _dot(h, wg_ref[f]), _dot(h, wv_ref[f])

    def gate(f, ug, uv):
        pg = carry_ref[f, 0]
        pv = carry_ref[f, 1]
        carry_ref[f, 0] = ug[-8:]
        carry_ref[f, 1] = uv[-8:]
        yg = conv(ug, pg, cwg_ref[f], cbg_ref[f])
        yv = conv(uv, pv, cwv_ref[f], cbv_ref[f])
        a_ref[f] = ((yg / (1.0 + jnp.exp(-yg))) * yv).astype(BF16)

    def stage(f, cur_ref, nxt_ref):
        gate(f, cur_ref[0], cur_ref[1])
        nxt_ref[0], nxt_ref[1] = up(f + 1)

    ua_ref[0], ua_ref[1] = up(0)

    def body(k, carry):
        stage(2 * k, ua_ref, ub_ref)
        stage(2 * k + 1, ub_ref, ua_ref)
        return carry

    assert nf % 2 == 1
    lax.fori_loop(0, (nf - 1) // 2, body, 0)
    gate(nf - 1, ua_ref[0], ua_ref[1])
    a = jnp.concatenate([a_ref[f] for f in range(nf)], axis=1)
    o_ref[0] = y_ref[...] + mod_ref[0, 5:6, :] * _dot(a, wd_ref[...])


def _ffn_call(x, oa, ob, mod, g_ffn, w_out, w_up, conv_w, conv_b, w_down, ts, fc):
    bsz, s, d = x.shape
    dff = w_down.shape[0]
    nf = dff // fc
    na = oa.shape[2]
    wo = w_out.astype(BF16).reshape(2, na, d)
    wup = w_up.astype(BF16)
    wg = wup[:, :dff].reshape(d, nf, fc).transpose(1, 0, 2)
    wv = wup[:, dff:].reshape(d, nf, fc).transpose(1, 0, 2)
    cwg = conv_w[:, :dff].reshape(CONV_W, nf, fc).transpose(1, 0, 2)
    cwv = conv_w[:, dff:].reshape(CONV_W, nf, fc).transpose(1, 0, 2)
    cbg = conv_b[:dff].reshape(nf, 1, fc)
    cbv = conv_b[dff:].reshape(nf, 1, fc)
    wd = w_down.astype(BF16)

    def full(a):
        return pl.BlockSpec(a.shape, lambda b, i: (0,) * a.ndim, pipeline_mode=pl.Buffered(1))

    def tok(n):
        return pl.BlockSpec((1, ts, n), lambda b, i: (b, i, 0))

    consts = [wo, wg, wv, cwg, cwv, cbg, cbv, wd]
    return pl.pallas_call(
        functools.partial(_ffn_kernel, nf=nf),
        grid=(bsz, s // ts),
        in_specs=[tok(d), tok(na), tok(ob.shape[2]),
                  pl.BlockSpec((1, 6, d), lambda b, i: (b, 0, 0)), full(g_ffn)]
                 + [full(a) for a in consts],
        out_specs=tok(d),
        out_shape=jax.ShapeDtypeStruct((bsz, s, d), F32),
        scratch_shapes=[pltpu.VMEM((nf, 2, 8, fc), F32), pltpu.VMEM((ts, d), BF16),
                        pltpu.VMEM((ts, d), F32), pltpu.VMEM((2, ts, fc), F32),
                        pltpu.VMEM((2, ts, fc), F32), pltpu.VMEM((nf, ts, fc), BF16)],
        compiler_params=pltpu.CompilerParams(dimension_semantics=("arbitrary", "arbitrary"),
                                             vmem_limit_bytes=VMEM_LIMIT),
        name="outproj_ffn",
    )(x, oa, ob, mod, g_ffn, *consts)


def kernel(x, c, w_ada, b_ada, g_attn, w_in, q_norm_a, k_norm_a, q_norm_b, k_norm_b, lam_vecs,
           subln_a, w_out, g_ffn, w_up, conv_w, conv_b, w_down, rel_bias):
    bsz, s, d = x.shape
    t = min(ATT_T, s)
    ta = min(DIFF_T, s)
    ts = min(512, s)
    topk = min(TOPK_MAX, s // 4)
    scale = HEAD_DIM ** -0.5
    for l in range(w_ada.shape[0]):
        mod = _ada_call(c, w_ada[l], b_ada[l]).reshape(bsz, 6, d)
        gains = jnp.stack([q_norm_a[l], k_norm_a[l], q_norm_b[l], k_norm_b[l]])
        bias_a, bound = _bias_call(rel_bias, gains, ta, 0, A_HEADS)
        bias_b, _ = _bias_call(rel_bias, gains, t, A_HEADS, B_HEADS)
        gqa = (jnp.tile(q_norm_a[l], 2 * A_HEADS) * (scale * LOG2E))[None]
        gka = jnp.tile(k_norm_a[l], 2 * A_HEADS)[None]
        gqb = (jnp.tile(q_norm_b[l], B_HEADS) * (scale * LOG2E))[None]
        gkb = jnp.tile(k_norm_b[l], B_KV_HEADS)[None]
        aq, ak, av, bq, bkv, iq, ik, iw = _inproj_call(
            x, mod, g_attn[l][None], w_in[l], gqa, gka, gqb, gkb, ts)
        lam_init = 0.8 - 0.6 * math.exp(-0.3 * l)
        o_a = _diff_call(lam_vecs[l], subln_a[l][None], bound, aq, ak, av, bias_a, ta, lam_init)
        o_b = _dsa_call(bound, iq, iw, ik, bq, bkv, bias_b, t, topk)
        x = _ffn_call(x, o_a, o_b, mod, g_ffn[l][None], w_out[l], w_up[l], conv_w[l], conv_b[l],
                      w_down[l], ts, 256)
    return x
```

```python
import functools
import math

import jax
import jax.numpy as jnp
from jax import lax
from jax.experimental import pallas as pl
from jax.experimental.pallas import tpu as pltpu

F32 = jnp.float32
BF16 = jnp.bfloat16
I32 = jnp.int32

HEAD_DIM = 64
A_HEADS = 4
A_VDIM = 2 * HEAD_DIM
B_HEADS = 8
B_KV_HEADS = 2
B_GROUP = B_HEADS // B_KV_HEADS
IDX_HEADS = 8
IDX_DIM = 64
TOPK_MAX = 256
N_BUCKETS = 32
MAX_DISTANCE = 128
CONV_W = 3
EPS = 1e-6

LANES = 128
ATT_T = 512
SEL_ROWS = 128
DIFF_T = 512
FAR_BLOCKS = 2
MASKED = -1e30
LOG2E = math.log2(math.e)
L_TINY = 2.0 ** -60
NEG_KEY = -2139095041
INT_MIN = -2147483648
VMEM_LIMIT = 56 * 1024 * 1024


def _dot(a, b):
    return jnp.dot(a, b, preferred_element_type=F32)


def _dot_nt(a, b):
    return lax.dot_general(a, b, (((1,), (1,)), ((), ())), preferred_element_type=F32)


def _split_bf16(a):
    hi = a.astype(BF16)
    lo = (a - hi.astype(F32)).astype(BF16)
    return hi, lo


def _ada_kernel(c_ref, w_ref, b_ref, o_ref):
    c = c_ref[...]
    a = c / (1.0 + jnp.exp(-c))
    a_hi, a_lo = _split_bf16(a)
    w_hi, w_lo = _split_bf16(w_ref[...])
    o_ref[...] = _dot(a_hi, w_hi) + _dot(a_hi, w_lo) + _dot(a_lo, w_hi) + b_ref[...]


def _ada_call(c, w, b):
    bsz, d = c.shape
    n = w.shape[1]
    tn = 1024
    return pl.pallas_call(
        _ada_kernel,
        grid=(n // tn,),
        in_specs=[pl.BlockSpec((bsz, d), lambda j: (0, 0)),
                  pl.BlockSpec((d, tn), lambda j: (0, j)),
                  pl.BlockSpec((1, tn), lambda j: (0, j))],
        out_specs=pl.BlockSpec((bsz, tn), lambda j: (0, j)),
        out_shape=jax.ShapeDtypeStruct((bsz, n), F32),
        compiler_params=pltpu.CompilerParams(dimension_semantics=("arbitrary",),
                                             vmem_limit_bytes=VMEM_LIMIT),
        name="ada_mod",
    )(c, w, b.reshape(1, n))


def _bias_kernel(tab_ref, tabv_ref, gains_ref, o_ref, bound_ref, *, t, h0):
    h = pl.program_id(0) + h0
    max_exact = N_BUCKETS // 2
    b = MAX_DISTANCE
    row = lax.broadcasted_iota(I32, (b, b), 0)
    col = lax.broadcasted_iota(I32, (b, b), 1)
    last = tab_ref[N_BUCKETS - 1, h]
    base = []
    for d in range(2):
        n = jnp.maximum(row - col + d * b, 0)
        nf = jnp.maximum(n, 1).astype(F32)
        large = max_exact + (jnp.log(nf / max_exact) / math.log(MAX_DISTANCE / max_exact)
                             * (N_BUCKETS - max_exact)).astype(I32)
        large = jnp.minimum(large, N_BUCKETS - 1)
        bucket = jnp.where(n < max_exact, n, large)
        acc = jnp.zeros((b, b), F32)
        for k in range(N_BUCKETS - 1):
            acc = jnp.where(bucket == k, (tab_ref[k, h] - last) * LOG2E, acc)
        base.append(acc)
    by_distance = {0: jnp.where(row >= col, base[0], MASKED), 1: base[1]}
    masked = jnp.full((b, b), MASKED, F32)
    zero = jnp.zeros((b, b), F32)
    nb = t // b
    for d in range(2):
        o_ref[0, d] = jnp.concatenate(
            [jnp.concatenate([masked if d * nb + a - c < 0 else by_distance.get(d * nb + a - c, zero)
                              for c in range(nb)], axis=1) for a in range(nb)], axis=0)
    tab = tabv_ref[...]
    rel = (tab - tab[N_BUCKETS - 1:N_BUCKETS, :]) * LOG2E
    hcol = lax.broadcasted_iota(I32, rel.shape, 1)
    g = jnp.max(jnp.abs(gains_ref[...]), axis=1, keepdims=True)
    for grp in range(2):
        in_grp = (hcol < A_HEADS) if grp == 0 else (hcol >= A_HEADS)
        bmax = jnp.max(jnp.where(in_grp, rel, 0.0), keepdims=True)
        qk = g[2 * grp:2 * grp + 1] * g[2 * grp + 1:2 * grp + 2] * (HEAD_DIM ** 0.5 * LOG2E)
        bound_ref[grp] = jnp.broadcast_to(qk + bmax, bound_ref.shape[1:])


def _bias_call(rel_bias, gains, t, h0, nh):
    return pl.pallas_call(
        functools.partial(_bias_kernel, t=t, h0=h0),
        grid=(nh,),
        in_specs=[pl.BlockSpec(memory_space=pltpu.SMEM),
                  pl.BlockSpec(rel_bias.shape, lambda h: (0, 0)),
                  pl.BlockSpec(gains.shape, lambda h: (0, 0))],
        out_specs=[pl.BlockSpec((1, 2, t, t), lambda h: (h, 0, 0, 0)),
                   pl.BlockSpec((2, 8, LANES), lambda h: (0, 0, 0))],
        out_shape=[jax.ShapeDtypeStruct((nh, 2, t, t), F32),
                   jax.ShapeDtypeStruct((2, 8, LANES), F32)],
        compiler_params=pltpu.CompilerParams(dimension_semantics=("arbitrary",)),
        name="bias_tiles",
    )(rel_bias, rel_bias, gains)


def _modulated_norm(x, g, scale, shift):
    y = x * lax.rsqrt(jnp.mean(x * x, axis=-1, keepdims=True) + EPS)
    return (y * g) * (1.0 + scale) + shift


def _head_norm(y, gain, bd):
    y2 = (y * y).astype(BF16)
    w = bd.shape[0]
    ss = jnp.concatenate([_dot(y2[:, c:c + w], bd) for c in range(0, y.shape[1], w)], axis=1)
    return y * lax.rsqrt(ss * (1.0 / HEAD_DIM) + EPS) * gain


def _inproj_kernel(x_ref, mod_ref, g_ref, waq_ref, wak_ref, wav_ref, wbq_ref, wbkv_ref, wiq_ref,
                   wik_ref, gqa_ref, gka_ref, gqb_ref, gkb_ref, bd_ref,
                   aq_ref, ak_ref, av_ref, bq_ref, bkv_ref, iq_ref, ik_ref, iw_ref):
    x = x_ref[0]
    h = _modulated_norm(x, g_ref[...], mod_ref[0, 1:2, :], mod_ref[0, 0:1, :]).astype(BF16)
    bd = bd_ref[...]
    aq_ref[0] = _head_norm(_dot(h, waq_ref[...]), gqa_ref[...], bd).astype(BF16)
    ak_ref[0] = _head_norm(_dot(h, wak_ref[...]), gka_ref[...], bd).astype(BF16)
    av_ref[0] = _dot(h, wav_ref[...]).astype(BF16)
    bq_ref[0] = _head_norm(_dot(h, wbq_ref[...]), gqb_ref[...], bd).astype(BF16)
    kv = _dot(h, wbkv_ref[...])
    nk = B_KV_HEADS * HEAD_DIM
    bk = _head_norm(kv[:, :nk], gkb_ref[...], bd[:nk, :nk])
    bkv_ref[0] = jnp.concatenate([bk, kv[:, nk:]], axis=1).astype(BF16)
    iq_ref[0] = _dot(h, wiq_ref[...]).astype(BF16)
    ikw = _dot(h, wik_ref[...])
    ik_ref[0] = ikw[:, :2 * IDX_DIM].astype(BF16)
    iw_ref[0] = ikw[:, 2 * IDX_DIM:2 * IDX_DIM + IDX_HEADS] * ((IDX_HEADS * IDX_DIM) ** -0.5)


def _inproj_call(x, mod, g_attn, w_in, gqa, gka, gqb, gkb, ts):
    bsz, s, d = x.shape
    na = A_HEADS * 2 * HEAD_DIM
    nb = B_HEADS * HEAD_DIM
    nkv = B_KV_HEADS * HEAD_DIM
    ni = IDX_HEADS * IDX_DIM
    o = 0
    parts = []
    for width in (na, na, A_HEADS * A_VDIM, nb, 2 * nkv, ni, IDX_DIM + IDX_HEADS):
        parts.append(w_in[:, o:o + width].astype(BF16))
        o += width
    wik = jnp.concatenate([parts[-1][:, :IDX_DIM], parts[-1]], axis=1)
    parts[-1] = jnp.pad(wik, ((0, 0), (0, 2 * LANES - wik.shape[1])))
    grp = jnp.arange(2 * LANES) // HEAD_DIM
    bd = (grp[:, None] == grp[None, :]).astype(BF16)

    def full(a):
        return pl.BlockSpec(a.shape, lambda b, i: (0,) * a.ndim)

    def tok(n):
        return pl.BlockSpec((1, ts, n), lambda b, i: (b, i, 0))

    consts = parts + [gqa, gka, gqb, gkb, bd]
    out_w = [(na, BF16), (na, BF16), (A_HEADS * A_VDIM, BF16), (nb, BF16), (2 * nkv, BF16),
             (ni, BF16), (2 * IDX_DIM, BF16), (IDX_HEADS, F32)]
    return pl.pallas_call(
        _inproj_kernel,
        grid=(bsz, s // ts),
        in_specs=[tok(d), pl.BlockSpec((1, 6, d), lambda b, i: (b, 0, 0)), full(g_attn)]
                 + [full(a) for a in consts],
        out_specs=[tok(n) for n, _ in out_w],
        out_shape=[jax.ShapeDtypeStruct((bsz, s, n), dt) for n, dt in out_w],
        compiler_params=pltpu.CompilerParams(dimension_semantics=("arbitrary", "arbitrary"),
                                             vmem_limit_bytes=VMEM_LIMIT),
        name="in_proj",
    )(x, mod, g_attn, *consts)


def _diff_kernel(lam_ref, g_ref, bound_ref, q_ref, k_ref, v_ref, bias_ref, o_ref, acc_ref, *,
                 t, lam_init):
    i = pl.program_id(2)
    q = q_ref[0]
    lane = lax.broadcasted_iota(I32, q.shape, 1)
    zero = jnp.zeros_like(q)
    qq = jnp.concatenate([jnp.where(lane < HEAD_DIM, q, zero),
                          jnp.where(lane >= HEAD_DIM, q, zero)], axis=0)

    def logits(start, width):
        k = k_ref[0, pl.ds(pl.multiple_of(start, t), width), :]
        return _dot_nt(qq, k)

    def v_ones(start, width):
        v = v_ref[0, pl.ds(pl.multiple_of(start, t), width), :]
        return jnp.concatenate([v, jnp.ones((width, LANES), BF16)], axis=1)

    def near_logits():
        b = jnp.concatenate([bias_ref[0, 1], bias_ref[0, 0]], axis=1)
        return logits((i - 1) * t, 2 * t) + jnp.concatenate([b, b], axis=0)

    def first_logits():
        b = bias_ref[0, 0]
        return logits(0, t) + jnp.concatenate([b, b], axis=0)

    def for_far_blocks(fn):
        nfar = jnp.maximum(i - 1, 0)
        nsup = nfar // FAR_BLOCKS

        def sup(j, c):
            fn(j * (FAR_BLOCKS * t), FAR_BLOCKS * t)
            return c

        def single(j, c):
            fn(j * t, t)
            return c

        lax.fori_loop(0, nsup, sup, 0)
        lax.fori_loop(nsup * FAR_BLOCKS, nfar, single, 0)

    def accumulate(m):
        acc_ref[...] = jnp.zeros(acc_ref.shape, F32)

        def far(start, width):
            p = jnp.exp2(logits(start, width) - m).astype(BF16)
            acc_ref[...] += _dot(p, v_ones(start, width))

        for_far_blocks(far)

        @pl.when(i > 0)
        def _():
            p = jnp.exp2(near_logits() - m).astype(BF16)
            acc_ref[...] += _dot(p, v_ones((i - 1) * t, 2 * t))

        @pl.when(i == 0)
        def _():
            p = jnp.exp2(first_logits() - m).astype(BF16)
            acc_ref[...] += _dot(p, v_ones(0, t))

    accumulate(bound_ref[0, 0:1, 0:1])

    @pl.when(jnp.min(acc_ref[:, LANES:LANES + 1]) < L_TINY)
    def _():
        acc_ref[...] = jnp.full(acc_ref.shape, MASKED, F32)

        def fold_max(s):
            w = acc_ref.shape[1]
            for c in range(s.shape[1] // w):
                acc_ref[...] = jnp.maximum(acc_ref[...], s[:, c * w:(c + 1) * w])

        for_far_blocks(lambda start, width: fold_max(logits(start, width)))

        @pl.when(i > 0)
        def _():
            fold_max(near_logits())

        @pl.when(i == 0)
        def _():
            fold_max(first_logits())

        accumulate(jnp.max(acc_ref[...], axis=1, keepdims=True))

    lv = lam_ref[...]
    lam = (jnp.exp(jnp.sum(lv[0:1] * lv[1:2], axis=1, keepdims=True))
           - jnp.exp(jnp.sum(lv[2:3] * lv[3:4], axis=1, keepdims=True)) + lam_init)
    acc = acc_ref[...]
    o = acc[:, :LANES] / acc[:, LANES:]
    o = o[:t] - lam * o[t:]
    o = o * lax.rsqrt(jnp.mean(o * o, axis=-1, keepdims=True) + EPS)
    o_ref[0] = (o * g_ref[...] * (1.0 - lam_init)).astype(BF16)


def _diff_call(lam_vecs, subln, bound, aq, ak, av, bias, t, lam_init):
    bsz, s, _ = aq.shape
    assert t % (2 * LANES) == 0
    return pl.pallas_call(
        functools.partial(_diff_kernel, t=t, lam_init=lam_init),
        grid=(bsz, A_HEADS, s // t),
        in_specs=[pl.BlockSpec(lam_vecs.shape, lambda b, h, i: (0, 0)),
                  pl.BlockSpec(subln.shape, lambda b, h, i: (0, 0)),
                  pl.BlockSpec((1, 8, LANES), lambda b, h, i: (0, 0, 0)),
                  pl.BlockSpec((1, t, A_VDIM), lambda b, h, i: (b, i, h)),
                  pl.BlockSpec((1, s, A_VDIM), lambda b, h, i: (b, 0, h)),
                  pl.BlockSpec((1, s, A_VDIM), lambda b, h, i: (b, 0, h)),
                  pl.BlockSpec((1, 2, t, t), lambda b, h, i: (h, 0, 0, 0))],
        out_specs=pl.BlockSpec((1, t, A_VDIM), lambda b, h, i: (b, i, h)),
        out_shape=jax.ShapeDtypeStruct((bsz, s, A_HEADS * A_VDIM), BF16),
        scratch_shapes=[pltpu.VMEM((2 * t, 2 * LANES), F32)],
        compiler_params=pltpu.CompilerParams(
            dimension_semantics=("arbitrary", "arbitrary", "arbitrary"),
            vmem_limit_bytes=VMEM_LIMIT),
        name="diff_attn",
    )(lam_vecs, subln, bound, aq, ak, av, bias)


def _sortable_key(x):
    b = pltpu.bitcast(x, I32)
    return b ^ ((b >> 31) & 0x7FFFFFFF)


def _dsa_kernel(bound_ref, iq_ref, iw_ref, ik_ref, q_ref, kv_ref, bias_ref, o_ref,
                keys_ref, iqz_ref, wb_ref, thr_ref, acc_ref, *, t, topk):
    i = pl.program_id(1)
    row = lax.broadcasted_iota(I32, (t, t), 0)
    col = lax.broadcasted_iota(I32, (t, t), 1)

    iq = iq_ref[0]
    iw = iw_ref[0]
    lane = lax.broadcasted_iota(I32, (t, LANES), 1)
    for h in range(IDX_HEADS):
        pair = iq[:, (h // 2) * LANES:(h // 2 + 1) * LANES]
        mine = (lane < IDX_DIM) if h % 2 == 0 else (lane >= IDX_DIM)
        iqz_ref[h] = jnp.where(mine, pair, jnp.zeros_like(pair))
        wb_ref[h] = jnp.broadcast_to(iw[:, h:h + 1], (t, LANES))

    def score_block(j, causal):
        kc = ik_ref[0, pl.ds(pl.multiple_of(j * t, t), t), :]
        acc = jnp.zeros((t, t), F32)
        for h in range(IDX_HEADS):
            wb = wb_ref[h]
            wb = jnp.concatenate([wb] * (t // LANES), axis=1)
            acc = acc + wb * jnp.maximum(_dot_nt(iqz_ref[h], kc), 0.0)
        if causal:
            acc = jnp.where(row >= col, acc, -jnp.inf)
        keys_ref[j] = _sortable_key(acc)

    def score_body(j, carry):
        score_block(j, False)
        return carry

    lax.fori_loop(0, i, score_body, 0)
    score_block(i, True)

    def count(pred, ref_val):
        cnts = []
        for r in range(t // SEL_ROWS):
            c = ref_val[r * SEL_ROWS:(r + 1) * SEL_ROWS]

            def body(j, cnt, r=r, c=c):
                k = keys_ref[j, pl.ds(r * SEL_ROWS, SEL_ROWS), :]
                for w in range(t // LANES):
                    cnt = cnt + jnp.where(pred(k[:, w * LANES:(w + 1) * LANES], c), 1.0, 0.0)
                return cnt

            cnts.append(lax.fori_loop(0, i + 1, body, jnp.zeros((SEL_ROWS, LANES), F32)))
        cnt = jnp.concatenate(cnts, axis=0)
        return jnp.broadcast_to(jnp.sum(cnt, axis=1, keepdims=True), cnt.shape)

    def bit_body(it, carry):
        thr, n_ge = carry
        cand = thr + jnp.left_shift(jnp.int32(1), 31 - it)
        cnt = count(lambda k, c: k >= c, cand)
        keep = cnt >= float(topk)
        return jnp.where(keep, cand, thr), jnp.where(keep, cnt, n_ge)

    n_all = ((i + 1) * t).astype(F32)
    thr, n_ge = lax.fori_loop(0, 32, bit_body, (jnp.full((t, LANES), INT_MIN, I32),
                                                jnp.full((t, LANES), n_all, F32)))

    excess = (n_ge > float(topk)) & (thr > NEG_KEY)
    thr_ref[...] = jnp.concatenate([thr] * (t // LANES), axis=1)

    @pl.when(jnp.max(jnp.where(excess, 1, 0)) > 0)
    def _():
        tri = (row <= col).astype(BF16)
        thr2 = thr_ref[...]
        need = float(topk) - count(lambda k, c: k > c, thr)
        need2 = jnp.concatenate([need] * (t // LANES), axis=1)
        excess2 = jnp.concatenate([excess] * (t // LANES), axis=1)

        def body(j, seen):
            k = keys_ref[j]
            eq = k == thr2
            eqf = jnp.where(eq, 1.0, 0.0)
            rank = seen + _dot(eqf.astype(BF16), tri)
            keys_ref[j] = jnp.where(eq & (rank > need2) & excess2, INT_MIN, k)
            return seen + jnp.sum(eqf, axis=1, keepdims=True)

        lax.fori_loop(0, i + 1, body, jnp.zeros((t, 1), F32))

    thr_ref[...] = jnp.maximum(thr_ref[...], NEG_KEY + 1)

    q = q_ref[0]
    qs = []
    for g in range(B_KV_HEADS):
        parts = []
        for r in range(B_GROUP):
            hq = q[:, (g * B_GROUP + r) * HEAD_DIM:(g * B_GROUP + r + 1) * HEAD_DIM]
            z = jnp.zeros_like(hq)
            parts.append(jnp.concatenate([hq, z] if g == 0 else [z, hq], axis=1))
        qs.append(jnp.concatenate(parts, axis=0))

    def logits(j, g, near):
        k = kv_ref[0, pl.ds(pl.multiple_of(j * t, t), t), :LANES]
        madd = jnp.where(keys_ref[j] >= thr_ref[...], 0.0, MASKED)
        s = _dot_nt(qs[g], k).reshape(B_GROUP, t, t)
        if near is None:
            s = s + madd[None]
        else:
            s = s + (bias_ref[g * B_GROUP:(g + 1) * B_GROUP, near] + madd[None])
        return s.reshape(B_GROUP * t, t)

    def for_blocks(fn):
        def far(j, c):
            fn(j, None)
            return c

        lax.fori_loop(0, i - 1, far, 0)

        @pl.when(i > 0)
        def _():
            fn(i - 1, 1)

        fn(i, 0)

    def accumulate(ms):
        acc_ref[...] = jnp.zeros(acc_ref.shape, F32)

        def step(j, near):
            v = kv_ref[0, pl.ds(pl.multiple_of(j * t, t), t), LANES:]
            vx = jnp.concatenate([v, jnp.ones((t, LANES), BF16)], axis=1)
            for g in range(B_KV_HEADS):
                p = jnp.exp2(logits(j, g, near) - ms[g]).astype(BF16)
                acc_ref[g] += _dot(p, vx)

        for_blocks(step)

    m0 = bound_ref[0, 0:1, 0:1]
    accumulate([m0, m0])

    @pl.when(jnp.min(acc_ref[:, :, LANES:LANES + 1]) < L_TINY)
    def _():
        acc_ref[...] = jnp.full(acc_ref.shape, MASKED, F32)

        def step(j, near):
            w = acc_ref.shape[2]
            for g in range(B_KV_HEADS):
                s = logits(j, g, near)
                for c in range(t // w):
                    acc_ref[g] = jnp.maximum(acc_ref[g], s[:, c * w:(c + 1) * w])

        for_blocks(step)
        accumulate([jnp.max(acc_ref[g], axis=1, keepdims=True) for g in range(B_KV_HEADS)])

    outs = []
    for g in range(B_KV_HEADS):
        a = acc_ref[g]
        o = a[:, g * HEAD_DIM:(g + 1) * HEAD_DIM] / a[:, LANES:LANES + HEAD_DIM]
        for r in range(B_GROUP):
            outs.append(o[r * t:(r + 1) * t])
    o_ref[0] = jnp.concatenate(outs, axis=1).astype(BF16)


def _dsa_call(bound, iq, iw, ik, bq, bkv, bias, t, topk):
    bsz, s, _ = bq.shape
    nq = s // t
    assert t % SEL_ROWS == 0 and t % (2 * LANES) == 0
    return pl.pallas_call(
        functools.partial(_dsa_kernel, t=t, topk=topk),
        grid=(bsz, nq),
        in_specs=[pl.BlockSpec((1, 8, LANES), lambda b, i: (1, 0, 0)),
                  pl.BlockSpec((1, t, iq.shape[2]), lambda b, i: (b, i, 0)),
                  pl.BlockSpec((1, t, iw.shape[2]), lambda b, i: (b, i, 0)),
                  pl.BlockSpec((1, s, ik.shape[2]), lambda b, i: (b, 0, 0)),
                  pl.BlockSpec((1, t, bq.shape[2]), lambda b, i: (b, i, 0)),
                  pl.BlockSpec((1, s, bkv.shape[2]), lambda b, i: (b, 0, 0)),
                  pl.BlockSpec(bias.shape, lambda b, i: (0, 0, 0, 0),
                               pipeline_mode=pl.Buffered(1))],
        out_specs=pl.BlockSpec((1, t, B_HEADS * HEAD_DIM), lambda b, i: (b, i, 0)),
        out_shape=jax.ShapeDtypeStruct((bsz, s, B_HEADS * HEAD_DIM), BF16),
        scratch_shapes=[pltpu.VMEM((nq, t, t), I32),
                        pltpu.VMEM((IDX_HEADS, t, LANES), BF16),
                        pltpu.VMEM((IDX_HEADS, t, LANES), F32),
                        pltpu.VMEM((t, t), I32),
                        pltpu.VMEM((B_KV_HEADS, B_GROUP * t, 2 * LANES), F32)],
        compiler_params=pltpu.CompilerParams(dimension_semantics=("arbitrary", "arbitrary"),
                                             vmem_limit_bytes=VMEM_LIMIT),
        name="dsa_attn",
    )(bound, iq, iw, ik, bq, bkv, bias)


def _ffn_kernel(x_ref, oa_ref, ob_ref, mod_ref, g_ref, wo_ref, wg_ref, wv_ref, cwg_ref, cwv_ref,
                cbg_ref, cbv_ref, wd_ref, o_ref, carry_ref, h_ref, y_ref, ua_ref, ub_ref, a_ref, *, nf):
    i = pl.program_id(1)
    x = x_ref[0]
    ts = x.shape[0]
    mixed = _dot(oa_ref[0], wo_ref[0]) + _dot(ob_ref[0], wo_ref[1])
    x1 = x + mod_ref[0, 2:3, :] * mixed
    h_ref[...] = _modulated_norm(x1, g_ref[...], mod_ref[0, 4:5, :], mod_ref[0, 3:4, :]).astype(BF16)
    y_ref[...] = x1

    @pl.when(i == 0)
    def _():
        carry_ref[...] = jnp.zeros(carry_ref.shape, F32)

    def up(f, buf_ref):
        h = h_ref[...]
        buf_ref[0, 8:8 + ts] = _dot(h, wg_ref[f])
        buf_ref[1, 8:8 + ts] = _dot(h, wv_ref[f])

    def conv(buf_ref, part, prev8, cw, cb):
        buf_ref[part, 0:8] = prev8
        return (cb + cw[2:3] * buf_ref[part, 8:8 + ts] + cw[1:2] * buf_ref[part, 7:7 + ts]
                + cw[0:1] * buf_ref[part, 6:6 + ts])

    def gate(f, buf_ref):
        yg = conv(buf_ref, 0, carry_ref[f, 0], cwg_ref[f], cbg_ref[f])
        yv = conv(buf_ref, 1, carry_ref[f, 1], cwv_ref[f], cbv_ref[f])
        carry_ref[f, 0] = buf_ref[0, ts:ts + 8]
        carry_ref[f, 1] = buf_ref[1, ts:ts + 8]
        a_ref[f] = ((yg / (1.0 + jnp.exp(-yg))) * yv).astype(BF16)

    def stage(f, cur_ref, nxt_ref):
        gate(f, cur_ref)
        up(f + 1, nxt_ref)

    up(0, ua_ref)

    def body(k, carry):
        stage(2 * k, ua_ref, ub_ref)
        stage(2 * k + 1, ub_ref, ua_ref)
        return carry

    assert nf % 2 == 1
    lax.fori_loop(0, (nf - 1) // 2, body, 0)
    gate(nf - 1, ua_ref)
    a = jnp.concatenate([a_ref[f] for f in range(nf)], axis=1)
    o_ref[0] = y_ref[...] + mod_ref[0, 5:6, :] * _dot(a, wd_ref[...])


def _ffn_call(x, oa, ob, mod, g_ffn, w_out, w_up, conv_w, conv_b, w_down, ts, fc):
    bsz, s, d = x.shape
    dff = w_down.shape[0]
    nf = dff // fc
    na = oa.shape[2]
    wo = w_out.astype(BF16).reshape(2, na, d)
    wup = w_up.astype(BF16)
    wg = wup[:, :dff].reshape(d, nf, fc).transpose(1, 0, 2)
    wv = wup[:, dff:].reshape(d, nf, fc).transpose(1, 0, 2)
    cwg = conv_w[:, :dff].reshape(CONV_W, nf, fc).transpose(1, 0, 2)
    cwv = conv_w[:, dff:].reshape(CONV_W, nf, fc).transpose(1, 0, 2)
    cbg = conv_b[:dff].reshape(nf, 1, fc)
    cbv = conv_b[dff:].reshape(nf, 1, fc)
    wd = w_down.astype(BF16)

    def full(a):
        return pl.BlockSpec(a.shape, lambda b, i: (0,) * a.ndim, pipeline_mode=pl.Buffered(1))

    def tok(n):
        return pl.BlockSpec((1, ts, n), lambda b, i: (b, i, 0))

    consts = [wo, wg, wv, cwg, cwv, cbg, cbv, wd]
    return pl.pallas_call(
        functools.partial(_ffn_kernel, nf=nf),
        grid=(bsz, s // ts),
        in_specs=[tok(d), tok(na), tok(ob.shape[2]),
                  pl.BlockSpec((1, 6, d), lambda b, i: (b, 0, 0)), full(g_ffn)]
                 + [full(a) for a in consts],
        out_specs=tok(d),
        out_shape=jax.ShapeDtypeStruct((bsz, s, d), F32),
        scratch_shapes=[pltpu.VMEM((nf, 2, 8, fc), F32), pltpu.VMEM((ts, d), BF16),
                        pltpu.VMEM((ts, d), F32), pltpu.VMEM((2, ts + 8, fc), F32),
                        pltpu.VMEM((2, ts + 8, fc), F32), pltpu.VMEM((nf, ts, fc), BF16)],
        compiler_params=pltpu.CompilerParams(dimension_semantics=("arbitrary", "arbitrary"),
                                             vmem_limit_bytes=VMEM_LIMIT),
        name="outproj_ffn",
    )(x, oa, ob, mod, g_ffn, *consts)


def kernel(x, c, w_ada, b_ada, g_attn, w_in, q_norm_a, k_norm_a, q_norm_b, k_norm_b, lam_vecs,
           subln_a, w_out, g_ffn, w_up, conv_w, conv_b, w_down, rel_bias):
    bsz, s, d = x.shape
    t = min(ATT_T, s)
    ta = min(DIFF_T, s)
    ts = min(512, s)
    topk = min(TOPK_MAX, s // 4)
    scale = HEAD_DIM ** -0.5
    for l in range(w_ada.shape[0]):
        mod = _ada_call(c, w_ada[l], b_ada[l]).reshape(bsz, 6, d)
        gains = jnp.stack([q_norm_a[l], k_norm_a[l], q_norm_b[l], k_norm_b[l]])
        bias_a, bound = _bias_call(rel_bias, gains, ta, 0, A_HEADS)
        bias_b, _ = _bias_call(rel_bias, gains, t, A_HEADS, B_HEADS)
        gqa = (jnp.tile(q_norm_a[l], 2 * A_HEADS) * (scale * LOG2E))[None]
        gka = jnp.tile(k_norm_a[l], 2 * A_HEADS)[None]
        gqb = (jnp.tile(q_norm_b[l], B_HEADS) * (scale * LOG2E))[None]
        gkb = jnp.tile(k_norm_b[l], B_KV_HEADS)[None]
        aq, ak, av, bq, bkv, iq, ik, iw = _inproj_call(
            x, mod, g_attn[l][None], w_in[l], gqa, gka, gqb, gkb, ts)
        lam_init = 0.8 - 0.6 * math.exp(-0.3 * l)
        o_a = _diff_call(lam_vecs[l], subln_a[l][None], bound, aq, ak, av, bias_a, ta, lam_init)
        o_b = _dsa_call(bound, iq, iw, ik, bq, bkv, bias_b, t, topk)
        x = _ffn_call(x, o_a, o_b, mod, g_ffn[l][None], w_out[l], w_up[l], conv_w[l], conv_b[l],
                      w_down[l], ts, 256)
    return x
```

```python
import functools
import math

import jax
import jax.numpy as jnp
from jax import lax
from jax.experimental import pallas as pl
from jax.experimental.pallas import tpu as pltpu

F32 = jnp.float32
BF16 = jnp.bfloat16
I32 = jnp.int32

HEAD_DIM = 64
A_HEADS = 4
A_VDIM = 2 * HEAD_DIM
B_HEADS = 8
B_KV_HEADS = 2
B_GROUP = B_HEADS // B_KV_HEADS
IDX_HEADS = 8
IDX_DIM = 64
TOPK_MAX = 256
N_BUCKETS = 32
MAX_DISTANCE = 128
CONV_W = 3
EPS = 1e-6

LANES = 128
ATT_T = 512
SEL_ROWS = 128
DIFF_T = 512
FAR_BLOCKS = 2
MASKED = -1e30
LOG2E = math.log2(math.e)
L_TINY = 2.0 ** -60
NEG_KEY = -2139095041
INT_MIN = -2147483648
VMEM_LIMIT = 56 * 1024 * 1024


def _dot(a, b):
    return jnp.dot(a, b, preferred_element_type=F32)


def _dot_nt(a, b):
    return lax.dot_general(a, b, (((1,), (1,)), ((), ())), preferred_element_type=F32)


def _split_bf16(a):
    hi = a.astype(BF16)
    lo = (a - hi.astype(F32)).astype(BF16)
    return hi, lo


def _ada_kernel(c_ref, w_ref, b_ref, o_ref):
    c = c_ref[...]
    a = c / (1.0 + jnp.exp(-c))
    a_hi, a_lo = _split_bf16(a)
    w_hi, w_lo = _split_bf16(w_ref[...])
    o_ref[...] = _dot(a_hi, w_hi) + _dot(a_hi, w_lo) + _dot(a_lo, w_hi) + b_ref[...]


def _ada_call(c, w, b):
    bsz, d = c.shape
    n = w.shape[1]
    tn = 1024
    return pl.pallas_call(
        _ada_kernel,
        grid=(n // tn,),
        in_specs=[pl.BlockSpec((bsz, d), lambda j: (0, 0)),
                  pl.BlockSpec((d, tn), lambda j: (0, j)),
                  pl.BlockSpec((1, tn), lambda j: (0, j))],
        out_specs=pl.BlockSpec((bsz, tn), lambda j: (0, j)),
        out_shape=jax.ShapeDtypeStruct((bsz, n), F32),
        compiler_params=pltpu.CompilerParams(dimension_semantics=("arbitrary",),
                                             vmem_limit_bytes=VMEM_LIMIT),
        name="ada_mod",
    )(c, w, b.reshape(1, n))


def _bias_kernel(tab_ref, tabv_ref, gains_ref, o_ref, bound_ref, *, t, h0):
    h = pl.program_id(0) + h0
    max_exact = N_BUCKETS // 2
    b = MAX_DISTANCE
    row = lax.broadcasted_iota(I32, (b, b), 0)
    col = lax.broadcasted_iota(I32, (b, b), 1)
    last = tab_ref[N_BUCKETS - 1, h]
    base = []
    for d in range(2):
        n = jnp.maximum(row - col + d * b, 0)
        nf = jnp.maximum(n, 1).astype(F32)
        large = max_exact + (jnp.log(nf / max_exact) / math.log(MAX_DISTANCE / max_exact)
                             * (N_BUCKETS - max_exact)).astype(I32)
        large = jnp.minimum(large, N_BUCKETS - 1)
        bucket = jnp.where(n < max_exact, n, large)
        acc = jnp.zeros((b, b), F32)
        for k in range(N_BUCKETS - 1):
            acc = jnp.where(bucket == k, (tab_ref[k, h] - last) * LOG2E, acc)
        base.append(acc)
    by_distance = {0: jnp.where(row >= col, base[0], MASKED), 1: base[1]}
    masked = jnp.full((b, b), MASKED, F32)
    zero = jnp.zeros((b, b), F32)
    nb = t // b
    for d in range(2):
        o_ref[0, d] = jnp.concatenate(
            [jnp.concatenate([masked if d * nb + a - c < 0 else by_distance.get(d * nb + a - c, zero)
                              for c in range(nb)], axis=1) for a in range(nb)], axis=0)
    tab = tabv_ref[...]
    rel = (tab - tab[N_BUCKETS - 1:N_BUCKETS, :]) * LOG2E
    hcol = lax.broadcasted_iota(I32, rel.shape, 1)
    g = jnp.max(jnp.abs(gains_ref[...]), axis=1, keepdims=True)
    for grp in range(2):
        in_grp = (hcol < A_HEADS) if grp == 0 else (hcol >= A_HEADS)
        bmax = jnp.max(jnp.where(in_grp, rel, 0.0), keepdims=True)
        qk = g[2 * grp:2 * grp + 1] * g[2 * grp + 1:2 * grp + 2] * (HEAD_DIM ** 0.5 * LOG2E)
        bound_ref[grp] = jnp.broadcast_to(qk + bmax, bound_ref.shape[1:])


def _bias_call(rel_bias, gains, t, h0, nh):
    return pl.pallas_call(
        functools.partial(_bias_kernel, t=t, h0=h0),
        grid=(nh,),
        in_specs=[pl.BlockSpec(memory_space=pltpu.SMEM),
                  pl.BlockSpec(rel_bias.shape, lambda h: (0, 0)),
                  pl.BlockSpec(gains.shape, lambda h: (0, 0))],
        out_specs=[pl.BlockSpec((1, 2, t, t), lambda h: (h, 0, 0, 0)),
                   pl.BlockSpec((2, 8, LANES), lambda h: (0, 0, 0))],
        out_shape=[jax.ShapeDtypeStruct((nh, 2, t, t), F32),
                   jax.ShapeDtypeStruct((2, 8, LANES), F32)],
        compiler_params=pltpu.CompilerParams(dimension_semantics=("arbitrary",)),
        name="bias_tiles",
    )(rel_bias, rel_bias, gains)


def _modulated_norm(x, g, scale, shift):
    y = x * lax.rsqrt(jnp.mean(x * x, axis=-1, keepdims=True) + EPS)
    return (y * g) * (1.0 + scale) + shift


def _head_norm(y, gain, bd):
    y2 = (y * y).astype(BF16)
    w = bd.shape[0]
    ss = jnp.concatenate([_dot(y2[:, c:c + w], bd) for c in range(0, y.shape[1], w)], axis=1)
    return y * lax.rsqrt(ss * (1.0 / HEAD_DIM) + EPS) * gain


def _inproj_kernel(x_ref, mod_ref, g_ref, waq_ref, wak_ref, wav_ref, wbq_ref, wbkv_ref, wiq_ref,
                   wik_ref, gqa_ref, gka_ref, gqb_ref, gkb_ref, bd_ref,
                   aq_ref, ak_ref, av_ref, bq_ref, bkv_ref, iq_ref, ik_ref, iw_ref):
    x = x_ref[0]
    h = _modulated_norm(x, g_ref[...], mod_ref[0, 1:2, :], mod_ref[0, 0:1, :]).astype(BF16)
    bd = bd_ref[...]
    aq_ref[0] = _head_norm(_dot(h, waq_ref[...]), gqa_ref[...], bd).astype(BF16)
    ak_ref[0] = _head_norm(_dot(h, wak_ref[...]), gka_ref[...], bd).astype(BF16)
    av_ref[0] = _dot(h, wav_ref[...]).astype(BF16)
    bq_ref[0] = _head_norm(_dot(h, wbq_ref[...]), gqb_ref[...], bd).astype(BF16)
    kv = _dot(h, wbkv_ref[...])
    nk = B_KV_HEADS * HEAD_DIM
    bk = _head_norm(kv[:, :nk], gkb_ref[...], bd[:nk, :nk])
    bkv_ref[0] = jnp.concatenate([bk, kv[:, nk:]], axis=1).astype(BF16)
    iq_ref[0] = _dot(h, wiq_ref[...]).astype(BF16)
    ikw = _dot(h, wik_ref[...])
    ik_ref[0] = ikw[:, :2 * IDX_DIM].astype(BF16)
    iw_ref[0] = ikw[:, 2 * IDX_DIM:2 * IDX_DIM + IDX_HEADS] * ((IDX_HEADS * IDX_DIM) ** -0.5)


def _inproj_call(x, mod, g_attn, w_in, gqa, gka, gqb, gkb, ts):
    bsz, s, d = x.shape
    na = A_HEADS * 2 * HEAD_DIM
    nb = B_HEADS * HEAD_DIM
    nkv = B_KV_HEADS * HEAD_DIM
    ni = IDX_HEADS * IDX_DIM
    o = 0
    parts = []
    for width in (na, na, A_HEADS * A_VDIM, nb, 2 * nkv, ni, IDX_DIM + IDX_HEADS):
        parts.append(w_in[:, o:o + width].astype(BF16))
        o += width
    wik = jnp.concatenate([parts[-1][:, :IDX_DIM], parts[-1]], axis=1)
    parts[-1] = jnp.pad(wik, ((0, 0), (0, 2 * LANES - wik.shape[1])))
    grp = jnp.arange(2 * LANES) // HEAD_DIM
    bd = (grp[:, None] == grp[None, :]).astype(BF16)

    def full(a):
        return pl.BlockSpec(a.shape, lambda b, i: (0,) * a.ndim)

    def tok(n):
        return pl.BlockSpec((1, ts, n), lambda b, i: (b, i, 0))

    consts = parts + [gqa, gka, gqb, gkb, bd]
    out_w = [(na, BF16), (na, BF16), (A_HEADS * A_VDIM, BF16), (nb, BF16), (2 * nkv, BF16),
             (ni, BF16), (2 * IDX_DIM, BF16), (IDX_HEADS, F32)]
    return pl.pallas_call(
        _inproj_kernel,
        grid=(bsz, s // ts),
        in_specs=[tok(d), pl.BlockSpec((1, 6, d), lambda b, i: (b, 0, 0)), full(g_attn)]
                 + [full(a) for a in consts],
        out_specs=[tok(n) for n, _ in out_w],
        out_shape=[jax.ShapeDtypeStruct((bsz, s, n), dt) for n, dt in out_w],
        compiler_params=pltpu.CompilerParams(dimension_semantics=("arbitrary", "arbitrary"),
                                             vmem_limit_bytes=VMEM_LIMIT),
        name="in_proj",
    )(x, mod, g_attn, *consts)


def _diff_kernel(lam_ref, g_ref, bound_ref, q_ref, k_ref, v_ref, bias_ref, o_ref, acc_ref, *,
                 t, lam_init):
    i = pl.program_id(2)
    q = q_ref[0]
    lane = lax.broadcasted_iota(I32, q.shape, 1)
    zero = jnp.zeros_like(q)
    qq = jnp.concatenate([jnp.where(lane < HEAD_DIM, q, zero),
                          jnp.where(lane >= HEAD_DIM, q, zero)], axis=0)

    def logits(start, width):
        k = k_ref[0, pl.ds(pl.multiple_of(start, t), width), :]
        return _dot_nt(qq, k)

    def v_ones(start, width):
        v = v_ref[0, pl.ds(pl.multiple_of(start, t), width), :]
        return jnp.concatenate([v, jnp.ones((width, LANES), BF16)], axis=1)

    def near_logits():
        b = jnp.concatenate([bias_ref[0, 1], bias_ref[0, 0]], axis=1)
        return logits((i - 1) * t, 2 * t) + jnp.concatenate([b, b], axis=0)

    def first_logits():
        b = bias_ref[0, 0]
        return logits(0, t) + jnp.concatenate([b, b], axis=0)

    def for_far_blocks(fn):
        nfar = jnp.maximum(i - 1, 0)
        nsup = nfar // FAR_BLOCKS

        def sup(j, c):
            fn(j * (FAR_BLOCKS * t), FAR_BLOCKS * t)
            return c

        def single(j, c):
            fn(j * t, t)
            return c

        lax.fori_loop(0, nsup, sup, 0)
        lax.fori_loop(nsup * FAR_BLOCKS, nfar, single, 0)

    def accumulate(m):
        acc_ref[...] = jnp.zeros(acc_ref.shape, F32)

        def far(start, width):
            p = jnp.exp2(logits(start, width) - m).astype(BF16)
            acc_ref[...] += _dot(p, v_ones(start, width))

        for_far_blocks(far)

        @pl.when(i > 0)
        def _():
            p = jnp.exp2(near_logits() - m).astype(BF16)
            acc_ref[...] += _dot(p, v_ones((i - 1) * t, 2 * t))

        @pl.when(i == 0)
        def _():
            p = jnp.exp2(first_logits() - m).astype(BF16)
            acc_ref[...] += _dot(p, v_ones(0, t))

    accumulate(bound_ref[0, 0:1, 0:1])

    @pl.when(jnp.min(acc_ref[:, LANES:LANES + 1]) < L_TINY)
    def _():
        acc_ref[...] = jnp.full(acc_ref.shape, MASKED, F32)

        def fold_max(s):
            w = acc_ref.shape[1]
            for c in range(s.shape[1] // w):
                acc_ref[...] = jnp.maximum(acc_ref[...], s[:, c * w:(c + 1) * w])

        for_far_blocks(lambda start, width: fold_max(logits(start, width)))

        @pl.when(i > 0)
        def _():
            fold_max(near_logits())

        @pl.when(i == 0)
        def _():
            fold_max(first_logits())

        accumulate(jnp.max(acc_ref[...], axis=1, keepdims=True))

    lv = lam_ref[...]
    lam = (jnp.exp(jnp.sum(lv[0:1] * lv[1:2], axis=1, keepdims=True))
           - jnp.exp(jnp.sum(lv[2:3] * lv[3:4], axis=1, keepdims=True)) + lam_init)
    acc = acc_ref[...]
    o = acc[:, :LANES] / acc[:, LANES:]
    o = o[:t] - lam * o[t:]
    o = o * lax.rsqrt(jnp.mean(o * o, axis=-1, keepdims=True) + EPS)
    o_ref[0] = (o * g_ref[...] * (1.0 - lam_init)).astype(BF16)


def _diff_call(lam_vecs, subln, bound, aq, ak, av, bias, t, lam_init):
    bsz, s, _ = aq.shape
    assert t % (2 * LANES) == 0
    return pl.pallas_call(
        functools.partial(_diff_kernel, t=t, lam_init=lam_init),
        grid=(bsz, A_HEADS, s // t),
        in_specs=[pl.BlockSpec(lam_vecs.shape, lambda b, h, i: (0, 0)),
                  pl.BlockSpec(subln.shape, lambda b, h, i: (0, 0)),
                  pl.BlockSpec((1, 8, LANES), lambda b, h, i: (0, 0, 0)),
                  pl.BlockSpec((1, t, A_VDIM), lambda b, h, i: (b, i, h)),
                  pl.BlockSpec((1, s, A_VDIM), lambda b, h, i: (b, 0, h)),
                  pl.BlockSpec((1, s, A_VDIM), lambda b, h, i: (b, 0, h)),
                  pl.BlockSpec((1, 2, t, t), lambda b, h, i: (h, 0, 0, 0))],
        out_specs=pl.BlockSpec((1, t, A_VDIM), lambda b, h, i: (b, i, h)),
        out_shape=jax.ShapeDtypeStruct((bsz, s, A_HEADS * A_VDIM), BF16),
        scratch_shapes=[pltpu.VMEM((2 * t, 2 * LANES), F32)],
        compiler_params=pltpu.CompilerParams(
            dimension_semantics=("arbitrary", "arbitrary", "arbitrary"),
            vmem_limit_bytes=VMEM_LIMIT),
        name="diff_attn",
    )(lam_vecs, subln, bound, aq, ak, av, bias)


def _sortable_key(x):
    b = pltpu.bitcast(x, I32)
    return b ^ ((b >> 31) & 0x7FFFFFFF)


def _dsa_kernel(bound_ref, iq_ref, iw_ref, ik_ref, q_ref, kv_ref, bias_ref, o_ref,
                keys_ref, iqz_ref, wb_ref, thr_ref, acc_ref, *, t, topk):
    i = pl.program_id(1)
    row = lax.broadcasted_iota(I32, (t, t), 0)
    col = lax.broadcasted_iota(I32, (t, t), 1)

    iq = iq_ref[0]
    iw = iw_ref[0]
    lane = lax.broadcasted_iota(I32, (t, LANES), 1)
    for h in range(IDX_HEADS):
        pair = iq[:, (h // 2) * LANES:(h // 2 + 1) * LANES]
        mine = (lane < IDX_DIM) if h % 2 == 0 else (lane >= IDX_DIM)
        iqz_ref[h] = jnp.where(mine, pair, jnp.zeros_like(pair))
        wb_ref[h] = jnp.broadcast_to(iw[:, h:h + 1], (t, LANES))

    def score_block(j, causal):
        kc = ik_ref[0, pl.ds(pl.multiple_of(j * t, t), t), :]
        acc = jnp.zeros((t, t), F32)
        for h in range(IDX_HEADS):
            wb = wb_ref[h]
            wb = jnp.concatenate([wb] * (t // LANES), axis=1)
            acc = acc + wb * jnp.maximum(_dot_nt(iqz_ref[h], kc), 0.0)
        if causal:
            acc = jnp.where(row >= col, acc, -jnp.inf)
        keys_ref[j] = _sortable_key(acc)

    def score_body(j, carry):
        score_block(j, False)
        return carry

    lax.fori_loop(0, i, score_body, 0)
    score_block(i, True)

    def count(pred, ref_val):
        cnts = []
        for r in range(t // SEL_ROWS):
            c = ref_val[r * SEL_ROWS:(r + 1) * SEL_ROWS]

            def add_block(j, cnt, lane_groups, r=r, c=c):
                k = keys_ref[j, pl.ds(r * SEL_ROWS, SEL_ROWS), :]
                for w in range(lane_groups):
                    cnt = cnt + jnp.where(pred(k[:, w * LANES:(w + 1) * LANES], c), 1.0, 0.0)
                return cnt

            cnt = lax.fori_loop(0, i, lambda j, cnt, f=add_block: f(j, cnt, t // LANES),
                                jnp.zeros((SEL_ROWS, LANES), F32))
            cnts.append(add_block(i, cnt, -(-(r + 1) * SEL_ROWS // LANES)))
        cnt = jnp.concatenate(cnts, axis=0)
        return jnp.broadcast_to(jnp.sum(cnt, axis=1, keepdims=True), cnt.shape)

    def bit_body(it, thr):
        cand = thr + jnp.left_shift(jnp.int32(1), 31 - it)
        return jnp.where(count(lambda k, c: k >= c, cand) >= float(topk), cand, thr)

    thr = lax.fori_loop(0, 32, bit_body, jnp.full((t, LANES), INT_MIN, I32))

    excess = (count(lambda k, c: k >= c, thr) > float(topk)) & (thr > NEG_KEY)
    thr_ref[...] = jnp.concatenate([thr] * (t // LANES), axis=1)

    @pl.when(jnp.max(jnp.where(excess, 1, 0)) > 0)
    def _():
        tri = (row <= col).astype(BF16)
        thr2 = thr_ref[...]
        need = float(topk) - count(lambda k, c: k > c, thr)
        need2 = jnp.concatenate([need] * (t // LANES), axis=1)
        excess2 = jnp.concatenate([excess] * (t // LANES), axis=1)

        def body(j, seen):
            k = keys_ref[j]
            eq = k == thr2
            eqf = jnp.where(eq, 1.0, 0.0)
            rank = seen + _dot(eqf.astype(BF16), tri)
            keys_ref[j] = jnp.where(eq & (rank > need2) & excess2, INT_MIN, k)
            return seen + jnp.sum(eqf, axis=1, keepdims=True)

        lax.fori_loop(0, i + 1, body, jnp.zeros((t, 1), F32))

    thr_ref[...] = jnp.maximum(thr_ref[...], NEG_KEY + 1)

    q = q_ref[0]
    qs = []
    for g in range(B_KV_HEADS):
        parts = []
        for r in range(B_GROUP):
            hq = q[:, (g * B_GROUP + r) * HEAD_DIM:(g * B_GROUP + r + 1) * HEAD_DIM]
            z = jnp.zeros_like(hq)
            parts.append(jnp.concatenate([hq, z] if g == 0 else [z, hq], axis=1))
        qs.append(jnp.concatenate(parts, axis=0))

    def logits(j, g, near):
        k = kv_ref[0, pl.ds(pl.multiple_of(j * t, t), t), :LANES]
        madd = jnp.where(keys_ref[j] >= thr_ref[...], 0.0, MASKED)
        s = _dot_nt(qs[g], k).reshape(B_GROUP, t, t)
        if near is None:
            s = s + madd[None]
        else:
            s = s + (bias_ref[g * B_GROUP:(g + 1) * B_GROUP, near] + madd[None])
        return s.reshape(B_GROUP * t, t)

    def for_blocks(fn):
        def far(j, c):
            fn(j, None)
            return c

        lax.fori_loop(0, i - 1, far, 0)

        @pl.when(i > 0)
        def _():
            fn(i - 1, 1)

        fn(i, 0)

    def accumulate(ms):
        acc_ref[...] = jnp.zeros(acc_ref.shape, F32)

        def step(j, near):
            v = kv_ref[0, pl.ds(pl.multiple_of(j * t, t), t), LANES:]
            vx = jnp.concatenate([v, jnp.ones((t, LANES), BF16)], axis=1)
            for g in range(B_KV_HEADS):
                p = jnp.exp2(logits(j, g, near) - ms[g]).astype(BF16)
                acc_ref[g] += _dot(p, vx)

        for_blocks(step)

    m0 = bound_ref[0, 0:1, 0:1]
    accumulate([m0, m0])

    @pl.when(jnp.min(acc_ref[:, :, LANES:LANES + 1]) < L_TINY)
    def _():
        acc_ref[...] = jnp.full(acc_ref.shape, MASKED, F32)

        def step(j, near):
            w = acc_ref.shape[2]
            for g in range(B_KV_HEADS):
                s = logits(j, g, near)
                for c in range(t // w):
                    acc_ref[g] = jnp.maximum(acc_ref[g], s[:, c * w:(c + 1) * w])

        for_blocks(step)
        accumulate([jnp.max(acc_ref[g], axis=1, keepdims=True) for g in range(B_KV_HEADS)])

    outs = []
    for g in range(B_KV_HEADS):
        a = acc_ref[g]
        o = a[:, g * HEAD_DIM:(g + 1) * HEAD_DIM] / a[:, LANES:LANES + HEAD_DIM]
        for r in range(B_GROUP):
            outs.append(o[r * t:(r + 1) * t])
    o_ref[0] = jnp.concatenate(outs, axis=1).astype(BF16)


def _dsa_call(bound, iq, iw, ik, bq, bkv, bias, t, topk):
    bsz, s, _ = bq.shape
    nq = s // t
    assert t % SEL_ROWS == 0 and t % (2 * LANES) == 0
    return pl.pallas_call(
        functools.partial(_dsa_kernel, t=t, topk=topk),
        grid=(bsz, nq),
        in_specs=[pl.BlockSpec((1, 8, LANES), lambda b, i: (1, 0, 0)),
                  pl.BlockSpec((1, t, iq.shape[2]), lambda b, i: (b, i, 0)),
                  pl.BlockSpec((1, t, iw.shape[2]), lambda b, i: (b, i, 0)),
                  pl.BlockSpec((1, s, ik.shape[2]), lambda b, i: (b, 0, 0)),
                  pl.BlockSpec((1, t, bq.shape[2]), lambda b, i: (b, i, 0)),
                  pl.BlockSpec((1, s, bkv.shape[2]), lambda b, i: (b, 0, 0)),
                  pl.BlockSpec(bias.shape, lambda b, i: (0, 0, 0, 0),
                               pipeline_mode=pl.Buffered(1))],
        out_specs=pl.BlockSpec((1, t, B_HEADS * HEAD_DIM), lambda b, i: (b, i, 0)),
        out_shape=jax.ShapeDtypeStruct((bsz, s, B_HEADS * HEAD_DIM), BF16),
        scratch_shapes=[pltpu.VMEM((nq, t, t), I32),
                        pltpu.VMEM((IDX_HEADS, t, LANES), BF16),
                        pltpu.VMEM((IDX_HEADS, t, LANES), F32),
                        pltpu.VMEM((t, t), I32),
                        pltpu.VMEM((B_KV_HEADS, B_GROUP * t, 2 * LANES), F32)],
        compiler_params=pltpu.CompilerParams(dimension_semantics=("arbitrary", "arbitrary"),
                                             vmem_limit_bytes=VMEM_LIMIT),
        name="dsa_attn",
    )(bound, iq, iw, ik, bq, bkv, bias)


def _ffn_kernel(x_ref, oa_ref, ob_ref, mod_ref, g_ref, wo_ref, wgv_ref, cwg_ref, cwv_ref,
                cbg_ref, cbv_ref, wd_ref, o_ref, carry_ref, h_ref, y_ref, ua_ref, ub_ref, a_ref, *, nf):
    i = pl.program_id(1)
    x = x_ref[0]
    ts = x.shape[0]
    mixed = _dot(oa_ref[0], wo_ref[0]) + _dot(ob_ref[0], wo_ref[1])
    x1 = x + mod_ref[0, 2:3, :] * mixed
    h_ref[...] = _modulated_norm(x1, g_ref[...], mod_ref[0, 4:5, :], mod_ref[0, 3:4, :]).astype(BF16)
    y_ref[...] = x1

    @pl.when(i == 0)
    def _():
        carry_ref[...] = jnp.zeros(carry_ref.shape, F32)

    def up(f, buf_ref):
        u = _dot(h_ref[...], wgv_ref[f])
        fc = u.shape[1] // 2
        buf_ref[0, 8:8 + ts] = u[:, :fc]
        buf_ref[1, 8:8 + ts] = u[:, fc:]

    def conv(buf_ref, part, prev8, cw, cb):
        buf_ref[part, 0:8] = prev8
        return (cb + cw[2:3] * buf_ref[part, 8:8 + ts] + cw[1:2] * buf_ref[part, 7:7 + ts]
                + cw[0:1] * buf_ref[part, 6:6 + ts])

    def gate(f, buf_ref):
        yg = conv(buf_ref, 0, carry_ref[f, 0], cwg_ref[f], cbg_ref[f])
        yv = conv(buf_ref, 1, carry_ref[f, 1], cwv_ref[f], cbv_ref[f])
        carry_ref[f, 0] = buf_ref[0, ts:ts + 8]
        carry_ref[f, 1] = buf_ref[1, ts:ts + 8]
        a_ref[f] = ((yg / (1.0 + jnp.exp(-yg))) * yv).astype(BF16)

    def stage(f, cur_ref, nxt_ref):
        gate(f, cur_ref)
        up(f + 1, nxt_ref)

    up(0, ua_ref)

    def body(k, carry):
        stage(2 * k, ua_ref, ub_ref)
        stage(2 * k + 1, ub_ref, ua_ref)
        return carry

    assert nf % 2 == 1
    lax.fori_loop(0, (nf - 1) // 2, body, 0)
    gate(nf - 1, ua_ref)
    a = jnp.concatenate([a_ref[f] for f in range(nf)], axis=1)
    o_ref[0] = y_ref[...] + mod_ref[0, 5:6, :] * _dot(a, wd_ref[...])


def _ffn_call(x, oa, ob, mod, g_ffn, w_out, w_up, conv_w, conv_b, w_down, ts, fc):
    bsz, s, d = x.shape
    dff = w_down.shape[0]
    nf = dff // fc
    na = oa.shape[2]
    wo = w_out.astype(BF16).reshape(2, na, d)
    wup = w_up.astype(BF16)
    wgv = jnp.concatenate([wup[:, :dff].reshape(d, nf, fc), wup[:, dff:].reshape(d, nf, fc)],
                          axis=2).transpose(1, 0, 2)
    cwg = conv_w[:, :dff].reshape(CONV_W, nf, fc).transpose(1, 0, 2)
    cwv = conv_w[:, dff:].reshape(CONV_W, nf, fc).transpose(1, 0, 2)
    cbg = conv_b[:dff].reshape(nf, 1, fc)
    cbv = conv_b[dff:].reshape(nf, 1, fc)
    wd = w_down.astype(BF16)

    def full(a):
        return pl.BlockSpec(a.shape, lambda b, i: (0,) * a.ndim, pipeline_mode=pl.Buffered(1))

    def tok(n):
        return pl.BlockSpec((1, ts, n), lambda b, i: (b, i, 0))

    consts = [wo, wgv, cwg, cwv, cbg, cbv, wd]
    return pl.pallas_call(
        functools.partial(_ffn_kernel, nf=nf),
        grid=(bsz, s // ts),
        in_specs=[tok(d), tok(na), tok(ob.shape[2]),
                  pl.BlockSpec((1, 6, d), lambda b, i: (b, 0, 0)), full(g_ffn)]
                 + [full(a) for a in consts],
        out_specs=tok(d),
        out_shape=jax.ShapeDtypeStruct((bsz, s, d), F32),
        scratch_shapes=[pltpu.VMEM((nf, 2, 8, fc), F32), pltpu.VMEM((ts, d), BF16),
                        pltpu.VMEM((ts, d), F32), pltpu.VMEM((2, ts + 8, fc), F32),
                        pltpu.VMEM((2, ts + 8, fc), F32), pltpu.VMEM((nf, ts, fc), BF16)],
        compiler_params=pltpu.CompilerParams(dimension_semantics=("arbitrary", "arbitrary"),
                                             vmem_limit_bytes=VMEM_LIMIT),
        name="outproj_ffn",
    )(x, oa, ob, mod, g_ffn, *consts)


def kernel(x, c, w_ada, b_ada, g_attn, w_in, q_norm_a, k_norm_a, q_norm_b, k_norm_b, lam_vecs,
           subln_a, w_out, g_ffn, w_up, conv_w, conv_b, w_down, rel_bias):
    bsz, s, d = x.shape
    t = min(ATT_T, s)
    ta = min(DIFF_T, s)
    ts = min(512, s)
    topk = min(TOPK_MAX, s // 4)
    scale = HEAD_DIM ** -0.5
    for l in range(w_ada.shape[0]):
        mod = _ada_call(c, w_ada[l], b_ada[l]).reshape(bsz, 6, d)
        gains = jnp.stack([q_norm_a[l], k_norm_a[l], q_norm_b[l], k_norm_b[l]])
        bias_a, bound = _bias_call(rel_bias, gains, ta, 0, A_HEADS)
        bias_b, _ = _bias_call(rel_bias, gains, t, A_HEADS, B_HEADS)
        gqa = (jnp.tile(q_norm_a[l], 2 * A_HEADS) * (scale * LOG2E))[None]
        gka = jnp.tile(k_norm_a[l], 2 * A_HEADS)[None]
        gqb = (jnp.tile(q_norm_b[l], B_HEADS) * (scale * LOG2E))[None]
        gkb = jnp.tile(k_norm_b[l], B_KV_HEADS)[None]
        aq, ak, av, bq, bkv, iq, ik, iw = _inproj_call(
            x, mod, g_attn[l][None], w_in[l], gqa, gka, gqb, gkb, ts)
        lam_init = 0.8 - 0.6 * math.exp(-0.3 * l)
        o_a = _diff_call(lam_vecs[l], subln_a[l][None], bound, aq, ak, av, bias_a, ta, lam_init)
        o_b = _dsa_call(bound, iq, iw, ik, bq, bkv, bias_b, t, topk)
        x = _ffn_call(x, o_a, o_b, mod, g_ffn[l][None], w_out[l], w_up[l], conv_w[l], conv_b[l],
                      w_down[l], ts, 256)
    return x
```

```python
import functools
import math

import jax
import jax.numpy as jnp
from jax import lax
from jax.experimental import pallas as pl
from jax.experimental.pallas import tpu as pltpu

F32 = jnp.float32
BF16 = jnp.bfloat16
I32 = jnp.int32

HEAD_DIM = 64
A_HEADS = 4
A_VDIM = 2 * HEAD_DIM
B_HEADS = 8
B_KV_HEADS = 2
B_GROUP = B_HEADS // B_KV_HEADS
IDX_HEADS = 8
IDX_DIM = 64
TOPK_MAX = 256
N_BUCKETS = 32
MAX_DISTANCE = 128
CONV_W = 3
EPS = 1e-6

LANES = 128
ATT_T = 512
SEL_ROWS = 128
DIFF_T = 512
FAR_BLOCKS = 2
MASKED = -1e30
LOG2E = math.log2(math.e)
L_TINY = 2.0 ** -60
NEG_KEY = -2139095041
INT_MIN = -2147483648
VMEM_LIMIT = 56 * 1024 * 1024


def _dot(a, b):
    return jnp.dot(a, b, preferred_element_type=F32)


def _dot_nt(a, b):
    return lax.dot_general(a, b, (((1,), (1,)), ((), ())), preferred_element_type=F32)


def _split_bf16(a):
    hi = a.astype(BF16)
    lo = (a - hi.astype(F32)).astype(BF16)
    return hi, lo


def _ada_kernel(c_ref, w_ref, b_ref, o_ref):
    c = c_ref[...]
    a = c / (1.0 + jnp.exp(-c))
    a_hi, a_lo = _split_bf16(a)
    w_hi, w_lo = _split_bf16(w_ref[...])
    o_ref[...] = _dot(a_hi, w_hi) + _dot(a_hi, w_lo) + _dot(a_lo, w_hi) + b_ref[...]


def _ada_call(c, w, b):
    bsz, d = c.shape
    n = w.shape[1]
    tn = 1024
    return pl.pallas_call(
        _ada_kernel,
        grid=(n // tn,),
        in_specs=[pl.BlockSpec((bsz, d), lambda j: (0, 0)),
                  pl.BlockSpec((d, tn), lambda j: (0, j)),
                  pl.BlockSpec((1, tn), lambda j: (0, j))],
        out_specs=pl.BlockSpec((bsz, tn), lambda j: (0, j)),
        out_shape=jax.ShapeDtypeStruct((bsz, n), F32),
        compiler_params=pltpu.CompilerParams(dimension_semantics=("arbitrary",),
                                             vmem_limit_bytes=VMEM_LIMIT),
        name="ada_mod",
    )(c, w, b.reshape(1, n))


def _bias_kernel(tab_ref, tabv_ref, gains_ref, o_ref, bound_ref, *, t, h0):
    h = pl.program_id(0) + h0
    max_exact = N_BUCKETS // 2
    b = MAX_DISTANCE
    row = lax.broadcasted_iota(I32, (b, b), 0)
    col = lax.broadcasted_iota(I32, (b, b), 1)
    last = tab_ref[N_BUCKETS - 1, h]
    base = []
    for d in range(2):
        n = jnp.maximum(row - col + d * b, 0)
        nf = jnp.maximum(n, 1).astype(F32)
        large = max_exact + (jnp.log(nf / max_exact) / math.log(MAX_DISTANCE / max_exact)
                             * (N_BUCKETS - max_exact)).astype(I32)
        large = jnp.minimum(large, N_BUCKETS - 1)
        bucket = jnp.where(n < max_exact, n, large)
        acc = jnp.zeros((b, b), F32)
        for k in range(N_BUCKETS - 1):
            acc = jnp.where(bucket == k, (tab_ref[k, h] - last) * LOG2E, acc)
        base.append(acc)
    by_distance = {0: jnp.where(row >= col, base[0], MASKED), 1: base[1]}
    masked = jnp.full((b, b), MASKED, F32)
    zero = jnp.zeros((b, b), F32)
    nb = t // b
    for d in range(2):
        o_ref[0, d] = jnp.concatenate(
            [jnp.concatenate([masked if d * nb + a - c < 0 else by_distance.get(d * nb + a - c, zero)
                              for c in range(nb)], axis=1) for a in range(nb)], axis=0)
    tab = tabv_ref[...]
    rel = (tab - tab[N_BUCKETS - 1:N_BUCKETS, :]) * LOG2E
    hcol = lax.broadcasted_iota(I32, rel.shape, 1)
    g = jnp.max(jnp.abs(gains_ref[...]), axis=1, keepdims=True)
    for grp in range(2):
        in_grp = (hcol < A_HEADS) if grp == 0 else (hcol >= A_HEADS)
        bmax = jnp.max(jnp.where(in_grp, rel, 0.0), keepdims=True)
        qk = g[2 * grp:2 * grp + 1] * g[2 * grp + 1:2 * grp + 2] * (HEAD_DIM ** 0.5 * LOG2E)
        bound_ref[grp] = jnp.broadcast_to(qk + bmax, bound_ref.shape[1:])


def _bias_call(rel_bias, gains, t, h0, nh):
    return pl.pallas_call(
        functools.partial(_bias_kernel, t=t, h0=h0),
        grid=(nh,),
        in_specs=[pl.BlockSpec(memory_space=pltpu.SMEM),
                  pl.BlockSpec(rel_bias.shape, lambda h: (0, 0)),
                  pl.BlockSpec(gains.shape, lambda h: (0, 0))],
        out_specs=[pl.BlockSpec((1, 2, t, t), lambda h: (h, 0, 0, 0)),
                   pl.BlockSpec((2, 8, LANES), lambda h: (0, 0, 0))],
        out_shape=[jax.ShapeDtypeStruct((nh, 2, t, t), F32),
                   jax.ShapeDtypeStruct((2, 8, LANES), F32)],
        compiler_params=pltpu.CompilerParams(dimension_semantics=("arbitrary",)),
        name="bias_tiles",
    )(rel_bias, rel_bias, gains)


def _modulated_norm(x, g, scale, shift):
    y = x * lax.rsqrt(jnp.mean(x * x, axis=-1, keepdims=True) + EPS)
    return (y * g) * (1.0 + scale) + shift


def _head_norm(y, gain, bd):
    y2 = (y * y).astype(BF16)
    w = bd.shape[0]
    ss = jnp.concatenate([_dot(y2[:, c:c + w], bd) for c in range(0, y.shape[1], w)], axis=1)
    return y * lax.rsqrt(ss * (1.0 / HEAD_DIM) + EPS) * gain


def _inproj_kernel(x_ref, mod_ref, g_ref, waq_ref, wak_ref, wav_ref, wbq_ref, wbkv_ref, wiq_ref,
                   wik_ref, gqa_ref, gka_ref, gqb_ref, gkb_ref, bd_ref,
                   aq_ref, ak_ref, av_ref, bq_ref, bkv_ref, iq_ref, ik_ref, iw_ref):
    x = x_ref[0]
    h = _modulated_norm(x, g_ref[...], mod_ref[0, 1:2, :], mod_ref[0, 0:1, :]).astype(BF16)
    bd = bd_ref[...]
    aq_ref[0] = _head_norm(_dot(h, waq_ref[...]), gqa_ref[...], bd).astype(BF16)
    ak_ref[0] = _head_norm(_dot(h, wak_ref[...]), gka_ref[...], bd).astype(BF16)
    av_ref[0] = _dot(h, wav_ref[...]).astype(BF16)
    bq_ref[0] = _head_norm(_dot(h, wbq_ref[...]), gqb_ref[...], bd).astype(BF16)
    kv = _dot(h, wbkv_ref[...])
    nk = B_KV_HEADS * HEAD_DIM
    bk = _head_norm(kv[:, :nk], gkb_ref[...], bd[:nk, :nk])
    bkv_ref[0] = jnp.concatenate([bk, kv[:, nk:]], axis=1).astype(BF16)
    iq_ref[0] = _dot(h, wiq_ref[...]).astype(BF16)
    ikw = _dot(h, wik_ref[...])
    ik_ref[0] = ikw[:, :2 * IDX_DIM].astype(BF16)
    iw_ref[0] = ikw[:, 2 * IDX_DIM:2 * IDX_DIM + IDX_HEADS] * ((IDX_HEADS * IDX_DIM) ** -0.5)


def _inproj_call(x, mod, g_attn, w_in, gqa, gka, gqb, gkb, ts):
    bsz, s, d = x.shape
    na = A_HEADS * 2 * HEAD_DIM
    nb = B_HEADS * HEAD_DIM
    nkv = B_KV_HEADS * HEAD_DIM
    ni = IDX_HEADS * IDX_DIM
    o = 0
    parts = []
    for width in (na, na, A_HEADS * A_VDIM, nb, 2 * nkv, ni, IDX_DIM + IDX_HEADS):
        parts.append(w_in[:, o:o + width].astype(BF16))
        o += width
    wik = jnp.concatenate([parts[-1][:, :IDX_DIM], parts[-1]], axis=1)
    parts[-1] = jnp.pad(wik, ((0, 0), (0, 2 * LANES - wik.shape[1])))
    grp = jnp.arange(2 * LANES) // HEAD_DIM
    bd = (grp[:, None] == grp[None, :]).astype(BF16)

    def full(a):
        return pl.BlockSpec(a.shape, lambda b, i: (0,) * a.ndim)

    def tok(n):
        return pl.BlockSpec((1, ts, n), lambda b, i: (b, i, 0))

    consts = parts + [gqa, gka, gqb, gkb, bd]
    out_w = [(na, BF16), (na, BF16), (A_HEADS * A_VDIM, BF16), (nb, BF16), (2 * nkv, BF16),
             (ni, BF16), (2 * IDX_DIM, BF16), (IDX_HEADS, F32)]
    return pl.pallas_call(
        _inproj_kernel,
        grid=(bsz, s // ts),
        in_specs=[tok(d), pl.BlockSpec((1, 6, d), lambda b, i: (b, 0, 0)), full(g_attn)]
                 + [full(a) for a in consts],
        out_specs=[tok(n) for n, _ in out_w],
        out_shape=[jax.ShapeDtypeStruct((bsz, s, n), dt) for n, dt in out_w],
        compiler_params=pltpu.CompilerParams(dimension_semantics=("arbitrary", "arbitrary"),
                                             vmem_limit_bytes=VMEM_LIMIT),
        name="in_proj",
    )(x, mod, g_attn, *consts)


def _diff_kernel(lam_ref, g_ref, bound_ref, q_ref, k_ref, v_ref, bias_ref, o_ref, acc_ref, *,
                 t, lam_init):
    i = pl.program_id(2)
    q = q_ref[0]
    lane = lax.broadcasted_iota(I32, q.shape, 1)
    zero = jnp.zeros_like(q)
    qq = jnp.concatenate([jnp.where(lane < HEAD_DIM, q, zero),
                          jnp.where(lane >= HEAD_DIM, q, zero)], axis=0)

    def logits(start, width):
        k = k_ref[0, pl.ds(pl.multiple_of(start, t), width), :]
        return _dot_nt(qq, k)

    def v_ones(start, width):
        v = v_ref[0, pl.ds(pl.multiple_of(start, t), width), :]
        return jnp.concatenate([v, jnp.ones((width, LANES), BF16)], axis=1)

    def near_logits():
        b = jnp.concatenate([bias_ref[0, 1], bias_ref[0, 0]], axis=1)
        return logits((i - 1) * t, 2 * t) + jnp.concatenate([b, b], axis=0)

    def first_logits():
        b = bias_ref[0, 0]
        return logits(0, t) + jnp.concatenate([b, b], axis=0)

    def for_far_blocks(fn):
        nfar = jnp.maximum(i - 1, 0)
        nsup = nfar // FAR_BLOCKS

        def sup(j, c):
            fn(j * (FAR_BLOCKS * t), FAR_BLOCKS * t)
            return c

        def single(j, c):
            fn(j * t, t)
            return c

        lax.fori_loop(0, nsup, sup, 0)
        lax.fori_loop(nsup * FAR_BLOCKS, nfar, single, 0)

    def accumulate(m):
        acc_ref[...] = jnp.zeros(acc_ref.shape, F32)

        def far(start, width):
            p = jnp.exp2(logits(start, width) - m).astype(BF16)
            acc_ref[...] += _dot(p, v_ones(start, width))

        for_far_blocks(far)

        @pl.when(i > 0)
        def _():
            p = jnp.exp2(near_logits() - m).astype(BF16)
            acc_ref[...] += _dot(p, v_ones((i - 1) * t, 2 * t))

        @pl.when(i == 0)
        def _():
            p = jnp.exp2(first_logits() - m).astype(BF16)
            acc_ref[...] += _dot(p, v_ones(0, t))

    accumulate(bound_ref[0, 0:1, 0:1])

    @pl.when(jnp.min(acc_ref[:, LANES:LANES + 1]) < L_TINY)
    def _():
        acc_ref[...] = jnp.full(acc_ref.shape, MASKED, F32)

        def fold_max(s):
            w = acc_ref.shape[1]
            for c in range(s.shape[1] // w):
                acc_ref[...] = jnp.maximum(acc_ref[...], s[:, c * w:(c + 1) * w])

        for_far_blocks(lambda start, width: fold_max(logits(start, width)))

        @pl.when(i > 0)
        def _():
            fold_max(near_logits())

        @pl.when(i == 0)
        def _():
            fold_max(first_logits())

        accumulate(jnp.max(acc_ref[...], axis=1, keepdims=True))

    lv = lam_ref[...]
    lam = (jnp.exp(jnp.sum(lv[0:1] * lv[1:2], axis=1, keepdims=True))
           - jnp.exp(jnp.sum(lv[2:3] * lv[3:4], axis=1, keepdims=True)) + lam_init)
    acc = acc_ref[...]
    o = acc[:, :LANES] / acc[:, LANES:]
    o = o[:t] - lam * o[t:]
    o = o * lax.rsqrt(jnp.mean(o * o, axis=-1, keepdims=True) + EPS)
    o_ref[0] = (o * g_ref[...] * (1.0 - lam_init)).astype(BF16)


def _diff_call(lam_vecs, subln, bound, aq, ak, av, bias, t, lam_init):
    bsz, s, _ = aq.shape
    assert t % (2 * LANES) == 0
    return pl.pallas_call(
        functools.partial(_diff_kernel, t=t, lam_init=lam_init),
        grid=(bsz, A_HEADS, s // t),
        in_specs=[pl.BlockSpec(lam_vecs.shape, lambda b, h, i: (0, 0)),
                  pl.BlockSpec(subln.shape, lambda b, h, i: (0, 0)),
                  pl.BlockSpec((1, 8, LANES), lambda b, h, i: (0, 0, 0)),
                  pl.BlockSpec((1, t, A_VDIM), lambda b, h, i: (b, i, h)),
                  pl.BlockSpec((1, s, A_VDIM), lambda b, h, i: (b, 0, h)),
                  pl.BlockSpec((1, s, A_VDIM), lambda b, h, i: (b, 0, h)),
                  pl.BlockSpec((1, 2, t, t), lambda b, h, i: (h, 0, 0, 0))],
        out_specs=pl.BlockSpec((1, t, A_VDIM), lambda b, h, i: (b, i, h)),
        out_shape=jax.ShapeDtypeStruct((bsz, s, A_HEADS * A_VDIM), BF16),
        scratch_shapes=[pltpu.VMEM((2 * t, 2 * LANES), F32)],
        compiler_params=pltpu.CompilerParams(
            dimension_semantics=("arbitrary", "arbitrary", "arbitrary"),
            vmem_limit_bytes=VMEM_LIMIT),
        name="diff_attn",
    )(lam_vecs, subln, bound, aq, ak, av, bias)


def _sortable_key(x):
    b = pltpu.bitcast(x, I32)
    return b ^ ((b >> 31) & 0x7FFFFFFF)


def _dsa_kernel(bound_ref, iq_ref, iw_ref, ik_ref, q_ref, kv_ref, bias_ref, o_ref,
                keys_ref, iqz_ref, wb_ref, thr_ref, acc_ref, *, t, topk):
    i = pl.program_id(1)
    row = lax.broadcasted_iota(I32, (t, t), 0)
    col = lax.broadcasted_iota(I32, (t, t), 1)

    iq = iq_ref[0]
    iw = iw_ref[0]
    lane = lax.broadcasted_iota(I32, (t, LANES), 1)
    for h in range(IDX_HEADS):
        pair = iq[:, (h // 2) * LANES:(h // 2 + 1) * LANES]
        mine = (lane < IDX_DIM) if h % 2 == 0 else (lane >= IDX_DIM)
        iqz_ref[h] = jnp.where(mine, pair, jnp.zeros_like(pair))
        wb_ref[h] = jnp.broadcast_to(iw[:, h:h + 1], (t, LANES))

    def score_block(j, causal):
        kc = ik_ref[0, pl.ds(pl.multiple_of(j * t, t), t), :]
        acc = jnp.zeros((t, t), F32)
        for h in range(IDX_HEADS):
            wb = wb_ref[h]
            wb = jnp.concatenate([wb] * (t // LANES), axis=1)
            acc = acc + wb * jnp.maximum(_dot_nt(iqz_ref[h], kc), 0.0)
        if causal:
            acc = jnp.where(row >= col, acc, -jnp.inf)
        keys_ref[j] = _sortable_key(acc)

    def score_body(j, carry):
        score_block(j, False)
        return carry

    lax.fori_loop(0, i, score_body, 0)
    score_block(i, True)

    def count(pred, ref_val):
        cnts = []
        for r in range(t // SEL_ROWS):
            c = ref_val[r * SEL_ROWS:(r + 1) * SEL_ROWS]

            def add_block(j, cnt, lane_groups, r=r, c=c):
                k = keys_ref[j, pl.ds(r * SEL_ROWS, SEL_ROWS), :]
                for w in range(lane_groups):
                    cnt = cnt + jnp.where(pred(k[:, w * LANES:(w + 1) * LANES], c), 1.0, 0.0)
                return cnt

            cnt = lax.fori_loop(0, i, lambda j, cnt, f=add_block: f(j, cnt, t // LANES),
                                jnp.zeros((SEL_ROWS, LANES), F32))
            cnts.append(add_block(i, cnt, -(-(r + 1) * SEL_ROWS // LANES)))
        cnt = jnp.concatenate(cnts, axis=0)
        return jnp.broadcast_to(jnp.sum(cnt, axis=1, keepdims=True), cnt.shape)

    def bit_body(it, thr):
        cand = thr + jnp.left_shift(jnp.int32(1), 31 - it)
        return jnp.where(count(lambda k, c: k >= c, cand) >= float(topk), cand, thr)

    thr = lax.fori_loop(0, 32, bit_body, jnp.full((t, LANES), INT_MIN, I32))

    excess = (count(lambda k, c: k >= c, thr) > float(topk)) & (thr > NEG_KEY)
    thr_ref[...] = jnp.concatenate([thr] * (t // LANES), axis=1)

    @pl.when(jnp.max(jnp.where(excess, 1, 0)) > 0)
    def _():
        tri = (row <= col).astype(BF16)
        thr2 = thr_ref[...]
        need = float(topk) - count(lambda k, c: k > c, thr)
        need2 = jnp.concatenate([need] * (t // LANES), axis=1)
        excess2 = jnp.concatenate([excess] * (t // LANES), axis=1)

        def body(j, seen):
            k = keys_ref[j]
            eq = k == thr2
            eqf = jnp.where(eq, 1.0, 0.0)
            rank = seen + _dot(eqf.astype(BF16), tri)
            keys_ref[j] = jnp.where(eq & (rank > need2) & excess2, INT_MIN, k)
            return seen + jnp.sum(eqf, axis=1, keepdims=True)

        lax.fori_loop(0, i + 1, body, jnp.zeros((t, 1), F32))

    thr_ref[...] = jnp.maximum(thr_ref[...], NEG_KEY + 1)

    q = q_ref[0]
    qs = []
    for g in range(B_KV_HEADS):
        parts = []
        for r in range(B_GROUP):
            hq = q[:, (g * B_GROUP + r) * HEAD_DIM:(g * B_GROUP + r + 1) * HEAD_DIM]
            z = jnp.zeros_like(hq)
            parts.append(jnp.concatenate([hq, z] if g == 0 else [z, hq], axis=1))
        qs.append(jnp.concatenate(parts, axis=0))

    def logits(j, g, near):
        k = kv_ref[0, pl.ds(pl.multiple_of(j * t, t), t), :LANES]
        madd = jnp.where(keys_ref[j] >= thr_ref[...], 0.0, MASKED)
        s = _dot_nt(qs[g], k).reshape(B_GROUP, t, t)
        if near is None:
            s = s + madd[None]
        else:
            s = s + (bias_ref[g * B_GROUP:(g + 1) * B_GROUP, near] + madd[None])
        return s.reshape(B_GROUP * t, t)

    def for_blocks(fn):
        def far(j, c):
            fn(j, None)
            return c

        lax.fori_loop(0, i - 1, far, 0)

        @pl.when(i > 0)
        def _():
            fn(i - 1, 1)

        fn(i, 0)

    def accumulate(ms):
        acc_ref[...] = jnp.zeros(acc_ref.shape, F32)

        def step(j, near):
            v = kv_ref[0, pl.ds(pl.multiple_of(j * t, t), t), LANES:]
            vx = jnp.concatenate([v, jnp.ones((t, LANES), BF16)], axis=1)
            for g in range(B_KV_HEADS):
                p = jnp.exp2(logits(j, g, near) - ms[g]).astype(BF16)
                acc_ref[g] += _dot(p, vx)

        for_blocks(step)

    m0 = bound_ref[0, 0:1, 0:1]
    accumulate([m0, m0])

    @pl.when(jnp.min(acc_ref[:, :, LANES:LANES + 1]) < L_TINY)
    def _():
        acc_ref[...] = jnp.full(acc_ref.shape, MASKED, F32)

        def step(j, near):
            w = acc_ref.shape[2]
            for g in range(B_KV_HEADS):
                s = logits(j, g, near)
                for c in range(t // w):
                    acc_ref[g] = jnp.maximum(acc_ref[g], s[:, c * w:(c + 1) * w])

        for_blocks(step)
        accumulate([jnp.max(acc_ref[g], axis=1, keepdims=True) for g in range(B_KV_HEADS)])

    outs = []
    for g in range(B_KV_HEADS):
        a = acc_ref[g]
        o = a[:, g * HEAD_DIM:(g + 1) * HEAD_DIM] / a[:, LANES:LANES + HEAD_DIM]
        for r in range(B_GROUP):
            outs.append(o[r * t:(r + 1) * t])
    o_ref[0] = jnp.concatenate(outs, axis=1).astype(BF16)


def _dsa_call(bound, iq, iw, ik, bq, bkv, bias, t, topk):
    bsz, s, _ = bq.shape
    nq = s // t
    assert t % SEL_ROWS == 0 and t % (2 * LANES) == 0
    return pl.pallas_call(
        functools.partial(_dsa_kernel, t=t, topk=topk),
        grid=(bsz, nq),
        in_specs=[pl.BlockSpec((1, 8, LANES), lambda b, i: (1, 0, 0)),
                  pl.BlockSpec((1, t, iq.shape[2]), lambda b, i: (b, i, 0)),
                  pl.BlockSpec((1, t, iw.shape[2]), lambda b, i: (b, i, 0)),
                  pl.BlockSpec((1, s, ik.shape[2]), lambda b, i: (b, 0, 0)),
                  pl.BlockSpec((1, t, bq.shape[2]), lambda b, i: (b, i, 0)),
                  pl.BlockSpec((1, s, bkv.shape[2]), lambda b, i: (b, 0, 0)),
                  pl.BlockSpec(bias.shape, lambda b, i: (0, 0, 0, 0),
                               pipeline_mode=pl.Buffered(1))],
        out_specs=pl.BlockSpec((1, t, B_HEADS * HEAD_DIM), lambda b, i: (b, i, 0)),
        out_shape=jax.ShapeDtypeStruct((bsz, s, B_HEADS * HEAD_DIM), BF16),
        scratch_shapes=[pltpu.VMEM((nq, t, t), I32),
                        pltpu.VMEM((IDX_HEADS, t, LANES), BF16),
                        pltpu.VMEM((IDX_HEADS, t, LANES), F32),
                        pltpu.VMEM((t, t), I32),
                        pltpu.VMEM((B_KV_HEADS, B_GROUP * t, 2 * LANES), F32)],
        compiler_params=pltpu.CompilerParams(dimension_semantics=("arbitrary", "arbitrary"),
                                             vmem_limit_bytes=VMEM_LIMIT),
        name="dsa_attn",
    )(bound, iq, iw, ik, bq, bkv, bias)


def _ffn_kernel(x_ref, oa_ref, ob_ref, mod_ref, g_ref, wo_ref, wgv_ref, cwg_ref, cwv_ref,
                cbg_ref, cbv_ref, wd_ref, o_ref, carry_ref, h_ref, y_ref, ua_ref, ub_ref, a_ref, *, nf):
    i = pl.program_id(1)
    x = x_ref[0]
    ts = x.shape[0]
    mixed = _dot(oa_ref[0], wo_ref[0]) + _dot(ob_ref[0], wo_ref[1])
    x1 = x + mod_ref[0, 2:3, :] * mixed
    h_ref[...] = _modulated_norm(x1, g_ref[...], mod_ref[0, 4:5, :], mod_ref[0, 3:4, :]).astype(BF16)
    y_ref[...] = x1

    @pl.when(i == 0)
    def _():
        carry_ref[...] = jnp.zeros(carry_ref.shape, F32)

    def up(f, buf_ref):
        u = _dot(h_ref[...], wgv_ref[f])
        fc = u.shape[1] // 2
        buf_ref[0, 8:8 + ts] = u[:, :fc]
        buf_ref[1, 8:8 + ts] = u[:, fc:]

    def conv(buf_ref, part, prev8, cw, cb):
        buf_ref[part, 0:8] = prev8
        return (cb + cw[2:3] * buf_ref[part, 8:8 + ts] + cw[1:2] * buf_ref[part, 7:7 + ts]
                + cw[0:1] * buf_ref[part, 6:6 + ts])

    def gate(f, buf_ref):
        yg = conv(buf_ref, 0, carry_ref[f, 0], cwg_ref[f], cbg_ref[f])
        yv = conv(buf_ref, 1, carry_ref[f, 1], cwv_ref[f], cbv_ref[f])
        carry_ref[f, 0] = buf_ref[0, ts:ts + 8]
        carry_ref[f, 1] = buf_ref[1, ts:ts + 8]
        a_ref[f] = ((yg / (1.0 + jnp.exp(-yg))) * yv).astype(BF16)

    def stage(f, cur_ref, nxt_ref):
        gate(f, cur_ref)
        up(f + 1, nxt_ref)

    up(0, ua_ref)

    assert nf % 2 == 1
    for k in range((nf - 1) // 2):
        stage(2 * k, ua_ref, ub_ref)
        stage(2 * k + 1, ub_ref, ua_ref)
    gate(nf - 1, ua_ref)
    a = jnp.concatenate([a_ref[f] for f in range(nf)], axis=1)
    o_ref[0] = y_ref[...] + mod_ref[0, 5:6, :] * _dot(a, wd_ref[...])


def _ffn_call(x, oa, ob, mod, g_ffn, w_out, w_up, conv_w, conv_b, w_down, ts, fc):
    bsz, s, d = x.shape
    dff = w_down.shape[0]
    nf = dff // fc
    na = oa.shape[2]
    wo = w_out.astype(BF16).reshape(2, na, d)
    wup = w_up.astype(BF16)
    wgv = jnp.concatenate([wup[:, :dff].reshape(d, nf, fc), wup[:, dff:].reshape(d, nf, fc)],
                          axis=2).transpose(1, 0, 2)
    cwg = conv_w[:, :dff].reshape(CONV_W, nf, fc).transpose(1, 0, 2)
    cwv = conv_w[:, dff:].reshape(CONV_W, nf, fc).transpose(1, 0, 2)
    cbg = conv_b[:dff].reshape(nf, 1, fc)
    cbv = conv_b[dff:].reshape(nf, 1, fc)
    wd = w_down.astype(BF16)

    def full(a):
        return pl.BlockSpec(a.shape, lambda b, i: (0,) * a.ndim, pipeline_mode=pl.Buffered(1))

    def tok(n):
        return pl.BlockSpec((1, ts, n), lambda b, i: (b, i, 0))

    consts = [wo, wgv, cwg, cwv, cbg, cbv, wd]
    return pl.pallas_call(
        functools.partial(_ffn_kernel, nf=nf),
        grid=(bsz, s // ts),
        in_specs=[tok(d), tok(na), tok(ob.shape[2]),
                  pl.BlockSpec((1, 6, d), lambda b, i: (b, 0, 0)), full(g_ffn)]
                 + [full(a) for a in consts],
        out_specs=tok(d),
        out_shape=jax.ShapeDtypeStruct((bsz, s, d), F32),
        scratch_shapes=[pltpu.VMEM((nf, 2, 8, fc), F32), pltpu.VMEM((ts, d), BF16),
                        pltpu.VMEM((ts, d), F32), pltpu.VMEM((2, ts + 8, fc), F32),
                        pltpu.VMEM((2, ts + 8, fc), F32), pltpu.VMEM((nf, ts, fc), BF16)],
        compiler_params=pltpu.CompilerParams(dimension_semantics=("arbitrary", "arbitrary"),
                                             vmem_limit_bytes=VMEM_LIMIT),
        name="outproj_ffn",
    )(x, oa, ob, mod, g_ffn, *consts)


def kernel(x, c, w_ada, b_ada, g_attn, w_in, q_norm_a, k_norm_a, q_norm_b, k_norm_b, lam_vecs,
           subln_a, w_out, g_ffn, w_up, conv_w, conv_b, w_down, rel_bias):
    bsz, s, d = x.shape
    t = min(ATT_T, s)
    ta = min(DIFF_T, s)
    ts = min(512, s)
    topk = min(TOPK_MAX, s // 4)
    scale = HEAD_DIM ** -0.5
    for l in range(w_ada.shape[0]):
        mod = _ada_call(c, w_ada[l], b_ada[l]).reshape(bsz, 6, d)
        gains = jnp.stack([q_norm_a[l], k_norm_a[l], q_norm_b[l], k_norm_b[l]])
        bias_a, bound = _bias_call(rel_bias, gains, ta, 0, A_HEADS)
        bias_b, _ = _bias_call(rel_bias, gains, t, A_HEADS, B_HEADS)
        gqa = (jnp.tile(q_norm_a[l], 2 * A_HEADS) * (scale * LOG2E))[None]
        gka = jnp.tile(k_norm_a[l], 2 * A_HEADS)[None]
        gqb = (jnp.tile(q_norm_b[l], B_HEADS) * (scale * LOG2E))[None]
        gkb = jnp.tile(k_norm_b[l], B_KV_HEADS)[None]
        aq, ak, av, bq, bkv, iq, ik, iw = _inproj_call(
            x, mod, g_attn[l][None], w_in[l], gqa, gka, gqb, gkb, ts)
        lam_init = 0.8 - 0.6 * math.exp(-0.3 * l)
        o_a = _diff_call(lam_vecs[l], subln_a[l][None], bound, aq, ak, av, bias_a, ta, lam_init)
        o_b = _dsa_call(bound, iq, iw, ik, bq, bkv, bias_b, t, topk)
        x = _ffn_call(x, o_a, o_b, mod, g_ffn[l][None], w_out[l], w_up[l], conv_w[l], conv_b[l],
                      w_down[l], ts, 256)
    return x
```

```python
import functools
import math

import jax
import jax.numpy as jnp
from jax import lax
from jax.experimental import pallas as pl
from jax.experimental.pallas import tpu as pltpu

F32 = jnp.float32
BF16 = jnp.bfloat16
I32 = jnp.int32

HEAD_DIM = 64
A_HEADS = 4
A_VDIM = 2 * HEAD_DIM
B_HEADS = 8
B_KV_HEADS = 2
B_GROUP = B_HEADS // B_KV_HEADS
IDX_HEADS = 8
IDX_DIM = 64
TOPK_MAX = 256
N_BUCKETS = 32
MAX_DISTANCE = 128
CONV_W = 3
EPS = 1e-6

LANES = 128
ATT_T = 512
SEL_ROWS = 128
DIFF_T = 512
FAR_BLOCKS = 2
MASKED = -1e30
LOG2E = math.log2(math.e)
L_TINY = 2.0 ** -60
NEG_KEY = -2139095041
INT_MIN = -2147483648
VMEM_LIMIT = 56 * 1024 * 1024


def _dot(a, b):
    return jnp.dot(a, b, preferred_element_type=F32)


def _dot_nt(a, b):
    return lax.dot_general(a, b, (((1,), (1,)), ((), ())), preferred_element_type=F32)


def _split_bf16(a):
    hi = a.astype(BF16)
    lo = (a - hi.astype(F32)).astype(BF16)
    return hi, lo


def _ada_kernel(c_ref, w_ref, b_ref, o_ref):
    c = c_ref[...]
    a = c / (1.0 + jnp.exp(-c))
    a_hi, a_lo = _split_bf16(a)
    w_hi, w_lo = _split_bf16(w_ref[...])
    o_ref[...] = _dot(a_hi, w_hi) + _dot(a_hi, w_lo) + _dot(a_lo, w_hi) + b_ref[...]


def _ada_call(c, w, b):
    bsz, d = c.shape
    n = w.shape[1]
    tn = 1024
    return pl.pallas_call(
        _ada_kernel,
        grid=(n // tn,),
        in_specs=[pl.BlockSpec((bsz, d), lambda j: (0, 0)),
                  pl.BlockSpec((d, tn), lambda j: (0, j)),
                  pl.BlockSpec((1, tn), lambda j: (0, j))],
        out_specs=pl.BlockSpec((bsz, tn), lambda j: (0, j)),
        out_shape=jax.ShapeDtypeStruct((bsz, n), F32),
        compiler_params=pltpu.CompilerParams(dimension_semantics=("arbitrary",),
                                             vmem_limit_bytes=VMEM_LIMIT),
        name="ada_mod",
    )(c, w, b.reshape(1, n))


def _bias_kernel(tab_ref, tabv_ref, gains_ref, o_ref, bound_ref, *, t, h0):
    h = pl.program_id(0) + h0
    max_exact = N_BUCKETS // 2
    b = MAX_DISTANCE
    row = lax.broadcasted_iota(I32, (b, b), 0)
    col = lax.broadcasted_iota(I32, (b, b), 1)
    last = tab_ref[N_BUCKETS - 1, h]
    base = []
    for d in range(2):
        n = jnp.maximum(row - col + d * b, 0)
        nf = jnp.maximum(n, 1).astype(F32)
        large = max_exact + (jnp.log(nf / max_exact) / math.log(MAX_DISTANCE / max_exact)
                             * (N_BUCKETS - max_exact)).astype(I32)
        large = jnp.minimum(large, N_BUCKETS - 1)
        bucket = jnp.where(n < max_exact, n, large)
        acc = jnp.zeros((b, b), F32)
        for k in range(N_BUCKETS - 1):
            acc = jnp.where(bucket == k, (tab_ref[k, h] - last) * LOG2E, acc)
        base.append(acc)
    by_distance = {0: jnp.where(row >= col, base[0], MASKED), 1: base[1]}
    masked = jnp.full((b, b), MASKED, F32)
    zero = jnp.zeros((b, b), F32)
    nb = t // b
    for d in range(2):
        o_ref[0, d] = jnp.concatenate(
            [jnp.concatenate([masked if d * nb + a - c < 0 else by_distance.get(d * nb + a - c, zero)
                              for c in range(nb)], axis=1) for a in range(nb)], axis=0)
    tab = tabv_ref[...]
    rel = (tab - tab[N_BUCKETS - 1:N_BUCKETS, :]) * LOG2E
    hcol = lax.broadcasted_iota(I32, rel.shape, 1)
    g = jnp.max(jnp.abs(gains_ref[...]), axis=1, keepdims=True)
    for grp in range(2):
        in_grp = (hcol < A_HEADS) if grp == 0 else (hcol >= A_HEADS)
        bmax = jnp.max(jnp.where(in_grp, rel, 0.0), keepdims=True)
        qk = g[2 * grp:2 * grp + 1] * g[2 * grp + 1:2 * grp + 2] * (HEAD_DIM ** 0.5 * LOG2E)
        bound_ref[grp] = jnp.broadcast_to(qk + bmax, bound_ref.shape[1:])


def _bias_call(rel_bias, gains, t, h0, nh):
    return pl.pallas_call(
        functools.partial(_bias_kernel, t=t, h0=h0),
        grid=(nh,),
        in_specs=[pl.BlockSpec(memory_space=pltpu.SMEM),
                  pl.BlockSpec(rel_bias.shape, lambda h: (0, 0)),
                  pl.BlockSpec(gains.shape, lambda h: (0, 0))],
        out_specs=[pl.BlockSpec((1, 2, t, t), lambda h: (h, 0, 0, 0)),
                   pl.BlockSpec((2, 8, LANES), lambda h: (0, 0, 0))],
        out_shape=[jax.ShapeDtypeStruct((nh, 2, t, t), F32),
                   jax.ShapeDtypeStruct((2, 8, LANES), F32)],
        compiler_params=pltpu.CompilerParams(dimension_semantics=("arbitrary",)),
        name="bias_tiles",
    )(rel_bias, rel_bias, gains)


def _modulated_norm(x, g, scale, shift):
    y = x * lax.rsqrt(jnp.mean(x * x, axis=-1, keepdims=True) + EPS)
    return (y * g) * (1.0 + scale) + shift


def _head_norm(y, gain, bd):
    y2 = (y * y).astype(BF16)
    w = bd.shape[0]
    ss = jnp.concatenate([_dot(y2[:, c:c + w], bd) for c in range(0, y.shape[1], w)], axis=1)
    return y * lax.rsqrt(ss * (1.0 / HEAD_DIM) + EPS) * gain


def _inproj_kernel(x_ref, mod_ref, g_ref, waq_ref, wak_ref, wav_ref, wbq_ref, wbkv_ref, wiq_ref,
                   wik_ref, gqa_ref, gka_ref, gqb_ref, gkb_ref, bd_ref,
                   aq_ref, ak_ref, av_ref, bq_ref, bkv_ref, iq_ref, ik_ref, iw_ref):
    x = x_ref[0]
    h = _modulated_norm(x, g_ref[...], mod_ref[0, 1:2, :], mod_ref[0, 0:1, :]).astype(BF16)
    bd = bd_ref[...]
    aq_ref[0] = _head_norm(_dot(h, waq_ref[...]), gqa_ref[...], bd).astype(BF16)
    ak_ref[0] = _head_norm(_dot(h, wak_ref[...]), gka_ref[...], bd).astype(BF16)
    av_ref[0] = _dot(h, wav_ref[...]).astype(BF16)
    bq_ref[0] = _head_norm(_dot(h, wbq_ref[...]), gqb_ref[...], bd).astype(BF16)
    kv = _dot(h, wbkv_ref[...])
    nk = B_KV_HEADS * HEAD_DIM
    bk = _head_norm(kv[:, :nk], gkb_ref[...], bd[:nk, :nk])
    bkv_ref[0] = jnp.concatenate([bk, kv[:, nk:]], axis=1).astype(BF16)
    iq_ref[0] = _dot(h, wiq_ref[...]).astype(BF16)
    ikw = _dot(h, wik_ref[...])
    ik_ref[0] = ikw[:, :2 * IDX_DIM].astype(BF16)
    iw_ref[0] = ikw[:, 2 * IDX_DIM:2 * IDX_DIM + IDX_HEADS] * ((IDX_HEADS * IDX_DIM) ** -0.5)


def _inproj_call(x, mod, g_attn, w_in, gqa, gka, gqb, gkb, ts):
    bsz, s, d = x.shape
    na = A_HEADS * 2 * HEAD_DIM
    nb = B_HEADS * HEAD_DIM
    nkv = B_KV_HEADS * HEAD_DIM
    ni = IDX_HEADS * IDX_DIM
    o = 0
    parts = []
    for width in (na, na, A_HEADS * A_VDIM, nb, 2 * nkv, ni, IDX_DIM + IDX_HEADS):
        parts.append(w_in[:, o:o + width].astype(BF16))
        o += width
    wik = jnp.concatenate([parts[-1][:, :IDX_DIM], parts[-1]], axis=1)
    parts[-1] = jnp.pad(wik, ((0, 0), (0, 2 * LANES - wik.shape[1])))
    grp = jnp.arange(2 * LANES) // HEAD_DIM
    bd = (grp[:, None] == grp[None, :]).astype(BF16)

    def full(a):
        return pl.BlockSpec(a.shape, lambda b, i: (0,) * a.ndim)

    def tok(n):
        return pl.BlockSpec((1, ts, n), lambda b, i: (b, i, 0))

    consts = parts + [gqa, gka, gqb, gkb, bd]
    out_w = [(na, BF16), (na, BF16), (A_HEADS * A_VDIM, BF16), (nb, BF16), (2 * nkv, BF16),
             (ni, BF16), (2 * IDX_DIM, BF16), (IDX_HEADS, F32)]
    return pl.pallas_call(
        _inproj_kernel,
        grid=(bsz, s // ts),
        in_specs=[tok(d), pl.BlockSpec((1, 6, d), lambda b, i: (b, 0, 0)), full(g_attn)]
                 + [full(a) for a in consts],
        out_specs=[tok(n) for n, _ in out_w],
        out_shape=[jax.ShapeDtypeStruct((bsz, s, n), dt) for n, dt in out_w],
        compiler_params=pltpu.CompilerParams(dimension_semantics=("arbitrary", "arbitrary"),
                                             vmem_limit_bytes=VMEM_LIMIT),
        name="in_proj",
    )(x, mod, g_attn, *consts)


def _diff_kernel(lam_ref, g_ref, bound_ref, q_ref, k_ref, v_ref, bias_ref, o_ref, acc_ref, *,
                 t, lam_init):
    i = pl.program_id(2)
    q = q_ref[0]
    lane = lax.broadcasted_iota(I32, q.shape, 1)
    zero = jnp.zeros_like(q)
    qq = jnp.concatenate([jnp.where(lane < HEAD_DIM, q, zero),
                          jnp.where(lane >= HEAD_DIM, q, zero)], axis=0)

    def logits(start, width):
        k = k_ref[0, pl.ds(pl.multiple_of(start, t), width), :]
        return _dot_nt(qq, k)

    def v_ones(start, width):
        v = v_ref[0, pl.ds(pl.multiple_of(start, t), width), :]
        return jnp.concatenate([v, jnp.ones((width, LANES), BF16)], axis=1)

    def near_logits():
        b = jnp.concatenate([bias_ref[0, 1], bias_ref[0, 0]], axis=1)
        return logits((i - 1) * t, 2 * t) + jnp.concatenate([b, b], axis=0)

    def first_logits():
        b = bias_ref[0, 0]
        return logits(0, t) + jnp.concatenate([b, b], axis=0)

    def for_far_blocks(fn):
        nfar = jnp.maximum(i - 1, 0)
        nsup = nfar // FAR_BLOCKS

        def sup(j, c):
            fn(j * (FAR_BLOCKS * t), FAR_BLOCKS * t)
            return c

        def single(j, c):
            fn(j * t, t)
            return c

        lax.fori_loop(0, nsup, sup, 0)
        lax.fori_loop(nsup * FAR_BLOCKS, nfar, single, 0)

    def accumulate(m):
        acc_ref[...] = jnp.zeros(acc_ref.shape, F32)

        def far(start, width):
            p = jnp.exp2(logits(start, width) - m).astype(BF16)
            acc_ref[...] += _dot(p, v_ones(start, width))

        for_far_blocks(far)

        @pl.when(i > 0)
        def _():
            p = jnp.exp2(near_logits() - m).astype(BF16)
            acc_ref[...] += _dot(p, v_ones((i - 1) * t, 2 * t))

        @pl.when(i == 0)
        def _():
            p = jnp.exp2(first_logits() - m).astype(BF16)
            acc_ref[...] += _dot(p, v_ones(0, t))

    accumulate(bound_ref[0, 0:1, 0:1])

    @pl.when(jnp.min(acc_ref[:, LANES:LANES + 1]) < L_TINY)
    def _():
        acc_ref[...] = jnp.full(acc_ref.shape, MASKED, F32)

        def fold_max(s):
            w = acc_ref.shape[1]
            for c in range(s.shape[1] // w):
                acc_ref[...] = jnp.maximum(acc_ref[...], s[:, c * w:(c + 1) * w])

        for_far_blocks(lambda start, width: fold_max(logits(start, width)))

        @pl.when(i > 0)
        def _():
            fold_max(near_logits())

        @pl.when(i == 0)
        def _():
            fold_max(first_logits())

        accumulate(jnp.max(acc_ref[...], axis=1, keepdims=True))

    lv = lam_ref[...]
    lam = (jnp.exp(jnp.sum(lv[0:1] * lv[1:2], axis=1, keepdims=True))
           - jnp.exp(jnp.sum(lv[2:3] * lv[3:4], axis=1, keepdims=True)) + lam_init)
    acc = acc_ref[...]
    o = acc[:, :LANES] / acc[:, LANES:]
    o = o[:t] - lam * o[t:]
    o = o * lax.rsqrt(jnp.mean(o * o, axis=-1, keepdims=True) + EPS)
    o_ref[0] = (o * g_ref[...] * (1.0 - lam_init)).astype(BF16)


def _diff_call(lam_vecs, subln, bound, aq, ak, av, bias, t, lam_init):
    bsz, s, _ = aq.shape
    assert t % (2 * LANES) == 0
    return pl.pallas_call(
        functools.partial(_diff_kernel, t=t, lam_init=lam_init),
        grid=(bsz, A_HEADS, s // t),
        in_specs=[pl.BlockSpec(lam_vecs.shape, lambda b, h, i: (0, 0)),
                  pl.BlockSpec(subln.shape, lambda b, h, i: (0, 0)),
                  pl.BlockSpec((1, 8, LANES), lambda b, h, i: (0, 0, 0)),
                  pl.BlockSpec((1, t, A_VDIM), lambda b, h, i: (b, i, h)),
                  pl.BlockSpec((1, s, A_VDIM), lambda b, h, i: (b, 0, h)),
                  pl.BlockSpec((1, s, A_VDIM), lambda b, h, i: (b, 0, h)),
                  pl.BlockSpec((1, 2, t, t), lambda b, h, i: (h, 0, 0, 0))],
        out_specs=pl.BlockSpec((1, t, A_VDIM), lambda b, h, i: (b, i, h)),
        out_shape=jax.ShapeDtypeStruct((bsz, s, A_HEADS * A_VDIM), BF16),
        scratch_shapes=[pltpu.VMEM((2 * t, 2 * LANES), F32)],
        compiler_params=pltpu.CompilerParams(
            dimension_semantics=("arbitrary", "arbitrary", "arbitrary"),
            vmem_limit_bytes=VMEM_LIMIT),
        name="diff_attn",
    )(lam_vecs, subln, bound, aq, ak, av, bias)


def _sortable_key(x):
    b = pltpu.bitcast(x, I32)
    return b ^ ((b >> 31) & 0x7FFFFFFF)


def _dsa_kernel(bound_ref, iq_ref, iw_ref, ik_ref, q_ref, kv_ref, bias_ref, o_ref,
                keys_ref, iqz_ref, wb_ref, thr_ref, tie_ref, acc_ref, *, t, topk):
    i = pl.program_id(1)
    row = lax.broadcasted_iota(I32, (t, t), 0)
    col = lax.broadcasted_iota(I32, (t, t), 1)

    iq = iq_ref[0]
    iw = iw_ref[0]
    lane = lax.broadcasted_iota(I32, (t, LANES), 1)
    for h in range(IDX_HEADS):
        pair = iq[:, (h // 2) * LANES:(h // 2 + 1) * LANES]
        mine = (lane < IDX_DIM) if h % 2 == 0 else (lane >= IDX_DIM)
        iqz_ref[h] = jnp.where(mine, pair, jnp.zeros_like(pair))
        wb_ref[h] = jnp.broadcast_to(iw[:, h:h + 1], (t, LANES))

    def score_block(j, causal):
        kc = ik_ref[0, pl.ds(pl.multiple_of(j * t, t), t), :]
        acc = jnp.zeros((t, t), F32)
        for h in range(IDX_HEADS):
            wb = wb_ref[h]
            wb = jnp.concatenate([wb] * (t // LANES), axis=1)
            acc = acc + wb * jnp.maximum(_dot_nt(iqz_ref[h], kc), 0.0)
        if causal:
            acc = jnp.where(row >= col, acc, -jnp.inf)
        keys_ref[j] = _sortable_key(acc)

    def score_body(j, carry):
        score_block(j, False)
        return carry

    lax.fori_loop(0, i, score_body, 0)
    score_block(i, True)

    def count(pred, ref_val, nfull):
        cnts = []
        for r in range(t // SEL_ROWS):
            c = ref_val[r * SEL_ROWS:(r + 1) * SEL_ROWS]

            def add_block(j, cnt, lane_groups, r=r, c=c):
                k = keys_ref[j, pl.ds(r * SEL_ROWS, SEL_ROWS), :]
                for w in range(lane_groups):
                    cnt = cnt + jnp.where(pred(k[:, w * LANES:(w + 1) * LANES], c), 1.0, 0.0)
                return cnt

            cnt = jnp.zeros((SEL_ROWS, LANES), F32)
            if isinstance(nfull, int):
                for j in range(nfull):
                    cnt = add_block(j, cnt, t // LANES)
            else:
                cnt = lax.fori_loop(0, nfull, lambda j, cnt, f=add_block: f(j, cnt, t // LANES), cnt)
            cnts.append(add_block(nfull, cnt, -(-(r + 1) * SEL_ROWS // LANES)))
        cnt = jnp.concatenate(cnts, axis=0)
        return jnp.broadcast_to(jnp.sum(cnt, axis=1, keepdims=True), cnt.shape)

    def select(nfull):
        def bit_body(it, thr):
            cand = thr + jnp.left_shift(jnp.int32(1), 31 - it)
            return jnp.where(count(lambda k, c: k >= c, cand, nfull) >= float(topk), cand, thr)

        thr = lax.fori_loop(0, 32, bit_body, jnp.full((t, LANES), INT_MIN, I32))
        excess = (count(lambda k, c: k >= c, thr, nfull) > float(topk)) & (thr > NEG_KEY)
        thr_ref[...] = jnp.concatenate([thr] * (t // LANES), axis=1)
        tie_ref[...] = jnp.where(excess, 1.0, 0.0)

    for n in range(keys_ref.shape[0]):
        pl.when(i == n)(functools.partial(select, n))

    @pl.when(jnp.max(tie_ref[...]) > 0.0)
    def _():
        tri = (row <= col).astype(BF16)
        thr2 = thr_ref[...]
        excess = tie_ref[...] > 0.0
        need = float(topk) - count(lambda k, c: k > c, thr2[:, :LANES], i)
        need2 = jnp.concatenate([need] * (t // LANES), axis=1)
        excess2 = jnp.concatenate([excess] * (t // LANES), axis=1)

        def body(j, seen):
            k = keys_ref[j]
            eq = k == thr2
            eqf = jnp.where(eq, 1.0, 0.0)
            rank = seen + _dot(eqf.astype(BF16), tri)
            keys_ref[j] = jnp.where(eq & (rank > need2) & excess2, INT_MIN, k)
            return seen + jnp.sum(eqf, axis=1, keepdims=True)

        lax.fori_loop(0, i + 1, body, jnp.zeros((t, 1), F32))

    thr_ref[...] = jnp.maximum(thr_ref[...], NEG_KEY + 1)

    q = q_ref[0]
    qs = []
    for g in range(B_KV_HEADS):
        parts = []
        for r in range(B_GROUP):
            hq = q[:, (g * B_GROUP + r) * HEAD_DIM:(g * B_GROUP + r + 1) * HEAD_DIM]
            z = jnp.zeros_like(hq)
            parts.append(jnp.concatenate([hq, z] if g == 0 else [z, hq], axis=1))
        qs.append(jnp.concatenate(parts, axis=0))

    def logits(j, g, near):
        k = kv_ref[0, pl.ds(pl.multiple_of(j * t, t), t), :LANES]
        madd = jnp.where(keys_ref[j] >= thr_ref[...], 0.0, MASKED)
        s = _dot_nt(qs[g], k).reshape(B_GROUP, t, t)
        if near is None:
            s = s + madd[None]
        else:
            s = s + (bias_ref[g * B_GROUP:(g + 1) * B_GROUP, near] + madd[None])
        return s.reshape(B_GROUP * t, t)

    def for_blocks(fn):
        def far(j, c):
            fn(j, None)
            return c

        lax.fori_loop(0, i - 1, far, 0)

        @pl.when(i > 0)
        def _():
            fn(i - 1, 1)

        fn(i, 0)

    def accumulate(ms):
        acc_ref[...] = jnp.zeros(acc_ref.shape, F32)

        def step(j, near):
            v = kv_ref[0, pl.ds(pl.multiple_of(j * t, t), t), LANES:]
            vx = jnp.concatenate([v, jnp.ones((t, LANES), BF16)], axis=1)
            for g in range(B_KV_HEADS):
                p = jnp.exp2(logits(j, g, near) - ms[g]).astype(BF16)
                acc_ref[g] += _dot(p, vx)

        for_blocks(step)

    m0 = bound_ref[0, 0:1, 0:1]
    accumulate([m0, m0])

    @pl.when(jnp.min(acc_ref[:, :, LANES:LANES + 1]) < L_TINY)
    def _():
        acc_ref[...] = jnp.full(acc_ref.shape, MASKED, F32)

        def step(j, near):
            w = acc_ref.shape[2]
            for g in range(B_KV_HEADS):
                s = logits(j, g, near)
                for c in range(t // w):
                    acc_ref[g] = jnp.maximum(acc_ref[g], s[:, c * w:(c + 1) * w])

        for_blocks(step)
        accumulate([jnp.max(acc_ref[g], axis=1, keepdims=True) for g in range(B_KV_HEADS)])

    outs = []
    for g in range(B_KV_HEADS):
        a = acc_ref[g]
        o = a[:, g * HEAD_DIM:(g + 1) * HEAD_DIM] / a[:, LANES:LANES + HEAD_DIM]
        for r in range(B_GROUP):
            outs.append(o[r * t:(r + 1) * t])
    o_ref[0] = jnp.concatenate(outs, axis=1).astype(BF16)


def _dsa_call(bound, iq, iw, ik, bq, bkv, bias, t, topk):
    bsz, s, _ = bq.shape
    nq = s // t
    assert t % SEL_ROWS == 0 and t % (2 * LANES) == 0
    return pl.pallas_call(
        functools.partial(_dsa_kernel, t=t, topk=topk),
        grid=(bsz, nq),
        in_specs=[pl.BlockSpec((1, 8, LANES), lambda b, i: (1, 0, 0)),
                  pl.BlockSpec((1, t, iq.shape[2]), lambda b, i: (b, i, 0)),
                  pl.BlockSpec((1, t, iw.shape[2]), lambda b, i: (b, i, 0)),
                  pl.BlockSpec((1, s, ik.shape[2]), lambda b, i: (b, 0, 0)),
                  pl.BlockSpec((1, t, bq.shape[2]), lambda b, i: (b, i, 0)),
                  pl.BlockSpec((1, s, bkv.shape[2]), lambda b, i: (b, 0, 0)),
                  pl.BlockSpec(bias.shape, lambda b, i: (0, 0, 0, 0),
                               pipeline_mode=pl.Buffered(1))],
        out_specs=pl.BlockSpec((1, t, B_HEADS * HEAD_DIM), lambda b, i: (b, i, 0)),
        out_shape=jax.ShapeDtypeStruct((bsz, s, B_HEADS * HEAD_DIM), BF16),
        scratch_shapes=[pltpu.VMEM((nq, t, t), I32),
                        pltpu.VMEM((IDX_HEADS, t, LANES), BF16),
                        pltpu.VMEM((IDX_HEADS, t, LANES), F32),
                        pltpu.VMEM((t, t), I32),
                        pltpu.VMEM((t, LANES), F32),
                        pltpu.VMEM((B_KV_HEADS, B_GROUP * t, 2 * LANES), F32)],
        compiler_params=pltpu.CompilerParams(dimension_semantics=("arbitrary", "arbitrary"),
                                             vmem_limit_bytes=VMEM_LIMIT),
        name="dsa_attn",
    )(bound, iq, iw, ik, bq, bkv, bias)


def _ffn_kernel(x_ref, oa_ref, ob_ref, mod_ref, g_ref, wo_ref, wgv_ref, cwg_ref, cwv_ref,
                cbg_ref, cbv_ref, wd_ref, o_ref, carry_ref, h_ref, y_ref, ua_ref, ub_ref, a_ref, *, nf):
    i = pl.program_id(1)
    x = x_ref[0]
    ts = x.shape[0]
    mixed = _dot(oa_ref[0], wo_ref[0]) + _dot(ob_ref[0], wo_ref[1])
    x1 = x + mod_ref[0, 2:3, :] * mixed
    h_ref[...] = _modulated_norm(x1, g_ref[...], mod_ref[0, 4:5, :], mod_ref[0, 3:4, :]).astype(BF16)
    y_ref[...] = x1

    @pl.when(i == 0)
    def _():
        carry_ref[...] = jnp.zeros(carry_ref.shape, F32)

    def up(f, buf_ref):
        u = _dot(h_ref[...], wgv_ref[f])
        fc = u.shape[1] // 2
        buf_ref[0, 8:8 + ts] = u[:, :fc]
        buf_ref[1, 8:8 + ts] = u[:, fc:]

    def conv(buf_ref, part, prev8, cw, cb):
        buf_ref[part, 0:8] = prev8
        return (cb + cw[2:3] * buf_ref[part, 8:8 + ts] + cw[1:2] * buf_ref[part, 7:7 + ts]
                + cw[0:1] * buf_ref[part, 6:6 + ts])

    def gate(f, buf_ref):
        yg = conv(buf_ref, 0, carry_ref[f, 0], cwg_ref[f], cbg_ref[f])
        yv = conv(buf_ref, 1, carry_ref[f, 1], cwv_ref[f], cbv_ref[f])
        carry_ref[f, 0] = buf_ref[0, ts:ts + 8]
        carry_ref[f, 1] = buf_ref[1, ts:ts + 8]
        a_ref[f] = ((yg / (1.0 + jnp.exp(-yg))) * yv).astype(BF16)

    def stage(f, cur_ref, nxt_ref):
        gate(f, cur_ref)
        up(f + 1, nxt_ref)

    up(0, ua_ref)

    assert nf % 2 == 1
    for k in range((nf - 1) // 2):
        stage(2 * k, ua_ref, ub_ref)
        stage(2 * k + 1, ub_ref, ua_ref)
    gate(nf - 1, ua_ref)
    a = jnp.concatenate([a_ref[f] for f in range(nf)], axis=1)
    o_ref[0] = y_ref[...] + mod_ref[0, 5:6, :] * _dot(a, wd_ref[...])


def _ffn_call(x, oa, ob, mod, g_ffn, w_out, w_up, conv_w, conv_b, w_down, ts, fc):
    bsz, s, d = x.shape
    dff = w_down.shape[0]
    nf = dff // fc
    na = oa.shape[2]
    wo = w_out.astype(BF16).reshape(2, na, d)
    wup = w_up.astype(BF16)
    wgv = jnp.concatenate([wup[:, :dff].reshape(d, nf, fc), wup[:, dff:].reshape(d, nf, fc)],
                          axis=2).transpose(1, 0, 2)
    cwg = conv_w[:, :dff].reshape(CONV_W, nf, fc).transpose(1, 0, 2)
    cwv = conv_w[:, dff:].reshape(CONV_W, nf, fc).transpose(1, 0, 2)
    cbg = conv_b[:dff].reshape(nf, 1, fc)
    cbv = conv_b[dff:].reshape(nf, 1, fc)
    wd = w_down.astype(BF16)

    def full(a):
        return pl.BlockSpec(a.shape, lambda b, i: (0,) * a.ndim, pipeline_mode=pl.Buffered(1))

    def tok(n):
        return pl.BlockSpec((1, ts, n), lambda b, i: (b, i, 0))

    consts = [wo, wgv, cwg, cwv, cbg, cbv, wd]
    return pl.pallas_call(
        functools.partial(_ffn_kernel, nf=nf),
        grid=(bsz, s // ts),
        in_specs=[tok(d), tok(na), tok(ob.shape[2]),
                  pl.BlockSpec((1, 6, d), lambda b, i: (b, 0, 0)), full(g_ffn)]
                 + [full(a) for a in consts],
        out_specs=tok(d),
        out_shape=jax.ShapeDtypeStruct((bsz, s, d), F32),
        scratch_shapes=[pltpu.VMEM((nf, 2, 8, fc), F32), pltpu.VMEM((ts, d), BF16),
                        pltpu.VMEM((ts, d), F32), pltpu.VMEM((2, ts + 8, fc), F32),
                        pltpu.VMEM((2, ts + 8, fc), F32), pltpu.VMEM((nf, ts, fc), BF16)],
        compiler_params=pltpu.CompilerParams(dimension_semantics=("arbitrary", "arbitrary"),
                                             vmem_limit_bytes=VMEM_LIMIT),
        name="outproj_ffn",
    )(x, oa, ob, mod, g_ffn, *consts)


def kernel(x, c, w_ada, b_ada, g_attn, w_in, q_norm_a, k_norm_a, q_norm_b, k_norm_b, lam_vecs,
           subln_a, w_out, g_ffn, w_up, conv_w, conv_b, w_down, rel_bias):
    bsz, s, d = x.shape
    t = min(ATT_T, s)
    ta = min(DIFF_T, s)
    ts = min(512, s)
    topk = min(TOPK_MAX, s // 4)
    scale = HEAD_DIM ** -0.5
    for l in range(w_ada.shape[0]):
        mod = _ada_call(c, w_ada[l], b_ada[l]).reshape(bsz, 6, d)
        gains = jnp.stack([q_norm_a[l], k_norm_a[l], q_norm_b[l], k_norm_b[l]])
        bias_a, bound = _bias_call(rel_bias, gains, ta, 0, A_HEADS)
        bias_b, _ = _bias_call(rel_bias, gains, t, A_HEADS, B_HEADS)
        gqa = (jnp.tile(q_norm_a[l], 2 * A_HEADS) * (scale * LOG2E))[None]
        gka = jnp.tile(k_norm_a[l], 2 * A_HEADS)[None]
        gqb = (jnp.tile(q_norm_b[l], B_HEADS) * (scale * LOG2E))[None]
        gkb = jnp.tile(k_norm_b[l], B_KV_HEADS)[None]
        aq, ak, av, bq, bkv, iq, ik, iw = _inproj_call(
            x, mod, g_attn[l][None], w_in[l], gqa, gka, gqb, gkb, ts)
        lam_init = 0.8 - 0.6 * math.exp(-0.3 * l)
        o_a = _diff_call(lam_vecs[l], subln_a[l][None], bound, aq, ak, av, bias_a, ta, lam_init)
        o_b = _dsa_call(bound, iq, iw, ik, bq, bkv, bias_b, t, topk)
        x = _ffn_call(x, o_a, o_b, mod, g_ffn[l][None], w_out[l], w_up[l], conv_w[l], conv_b[l],
                      w_down[l], ts, 256)
    return x
```

```python
import functools
import math

import jax
import jax.numpy as jnp
from jax import lax
from jax.experimental import pallas as pl
from jax.experimental.pallas import tpu as pltpu

F32 = jnp.float32
BF16 = jnp.bfloat16
I32 = jnp.int32

HEAD_DIM = 64
A_HEADS = 4
A_VDIM = 2 * HEAD_DIM
B_HEADS = 8
B_KV_HEADS = 2
B_GROUP = B_HEADS // B_KV_HEADS
IDX_HEADS = 8
IDX_DIM = 64
TOPK_MAX = 256
N_BUCKETS = 32
MAX_DISTANCE = 128
CONV_W = 3
EPS = 1e-6

LANES = 128
ATT_T = 512
SEL_ROWS = 128
DIFF_T = 512
FAR_BLOCKS = 2
DOWN_GROUP = 4
MASKED = -1e30
LOG2E = math.log2(math.e)
L_TINY = 2.0 ** -60
NEG_KEY = -2139095041
INT_MIN = -2147483648
VMEM_LIMIT = 56 * 1024 * 1024


def _dot(a, b):
    return jnp.dot(a, b, preferred_element_type=F32)


def _dot_nt(a, b):
    return lax.dot_general(a, b, (((1,), (1,)), ((), ())), preferred_element_type=F32)


def _split_bf16(a):
    hi = a.astype(BF16)
    lo = (a - hi.astype(F32)).astype(BF16)
    return hi, lo


def _ada_kernel(c_ref, w_ref, b_ref, o_ref):
    c = c_ref[...]
    a = c / (1.0 + jnp.exp(-c))
    a_hi, a_lo = _split_bf16(a)
    w_hi, w_lo = _split_bf16(w_ref[...])
    o_ref[...] = _dot(a_hi, w_hi) + _dot(a_hi, w_lo) + _dot(a_lo, w_hi) + b_ref[...]


def _ada_call(c, w, b):
    bsz, d = c.shape
    n = w.shape[1]
    tn = 1024
    return pl.pallas_call(
        _ada_kernel,
        grid=(n // tn,),
        in_specs=[pl.BlockSpec((bsz, d), lambda j: (0, 0)),
                  pl.BlockSpec((d, tn), lambda j: (0, j)),
                  pl.BlockSpec((1, tn), lambda j: (0, j))],
        out_specs=pl.BlockSpec((bsz, tn), lambda j: (0, j)),
        out_shape=jax.ShapeDtypeStruct((bsz, n), F32),
        compiler_params=pltpu.CompilerParams(dimension_semantics=("arbitrary",),
                                             vmem_limit_bytes=VMEM_LIMIT),
        name="ada_mod",
    )(c, w, b.reshape(1, n))


def _bias_kernel(tab_ref, tabv_ref, gains_ref, o_ref, bound_ref, *, t, h0):
    h = pl.program_id(0) + h0
    max_exact = N_BUCKETS // 2
    b = MAX_DISTANCE
    row = lax.broadcasted_iota(I32, (b, b), 0)
    col = lax.broadcasted_iota(I32, (b, b), 1)
    last = tab_ref[N_BUCKETS - 1, h]
    base = []
    for d in range(2):
        n = jnp.maximum(row - col + d * b, 0)
        nf = jnp.maximum(n, 1).astype(F32)
        large = max_exact + (jnp.log(nf / max_exact) / math.log(MAX_DISTANCE / max_exact)
                             * (N_BUCKETS - max_exact)).astype(I32)
        large = jnp.minimum(large, N_BUCKETS - 1)
        bucket = jnp.where(n < max_exact, n, large)
        acc = jnp.zeros((b, b), F32)
        for k in range(N_BUCKETS - 1):
            acc = jnp.where(bucket == k, (tab_ref[k, h] - last) * LOG2E, acc)
        base.append(acc)
    by_distance = {0: jnp.where(row >= col, base[0], MASKED), 1: base[1]}
    masked = jnp.full((b, b), MASKED, F32)
    zero = jnp.zeros((b, b), F32)
    nb = t // b
    for d in range(2):
        o_ref[0, d] = jnp.concatenate(
            [jnp.concatenate([masked if d * nb + a - c < 0 else by_distance.get(d * nb + a - c, zero)
                              for c in range(nb)], axis=1) for a in range(nb)], axis=0)
    tab = tabv_ref[...]
    rel = (tab - tab[N_BUCKETS - 1:N_BUCKETS, :]) * LOG2E
    hcol = lax.broadcasted_iota(I32, rel.shape, 1)
    g = jnp.max(jnp.abs(gains_ref[...]), axis=1, keepdims=True)
    for grp in range(2):
        in_grp = (hcol < A_HEADS) if grp == 0 else (hcol >= A_HEADS)
        bmax = jnp.max(jnp.where(in_grp, rel, 0.0), keepdims=True)
        qk = g[2 * grp:2 * grp + 1] * g[2 * grp + 1:2 * grp + 2] * (HEAD_DIM ** 0.5 * LOG2E)
        bound_ref[grp] = jnp.broadcast_to(qk + bmax, bound_ref.shape[1:])


def _bias_call(rel_bias, gains, t, h0, nh):
    return pl.pallas_call(
        functools.partial(_bias_kernel, t=t, h0=h0),
        grid=(nh,),
        in_specs=[pl.BlockSpec(memory_space=pltpu.SMEM),
                  pl.BlockSpec(rel_bias.shape, lambda h: (0, 0)),
                  pl.BlockSpec(gains.shape, lambda h: (0, 0))],
        out_specs=[pl.BlockSpec((1, 2, t, t), lambda h: (h, 0, 0, 0)),
                   pl.BlockSpec((2, 8, LANES), lambda h: (0, 0, 0))],
        out_shape=[jax.ShapeDtypeStruct((nh, 2, t, t), F32),
                   jax.ShapeDtypeStruct((2, 8, LANES), F32)],
        compiler_params=pltpu.CompilerParams(dimension_semantics=("arbitrary",)),
        name="bias_tiles",
    )(rel_bias, rel_bias, gains)


def _modulated_norm(x, g, scale, shift):
    y = x * lax.rsqrt(jnp.mean(x * x, axis=-1, keepdims=True) + EPS)
    return (y * g) * (1.0 + scale) + shift


def _head_norm(y, gain, bd):
    y2 = (y * y).astype(BF16)
    w = bd.shape[0]
    ss = jnp.concatenate([_dot(y2[:, c:c + w], bd) for c in range(0, y.shape[1], w)], axis=1)
    return y * lax.rsqrt(ss * (1.0 / HEAD_DIM) + EPS) * gain


def _inproj_kernel(x_ref, mod_ref, g_ref, waq_ref, wak_ref, wav_ref, wbq_ref, wbkv_ref, wiq_ref,
                   wik_ref, gqa_ref, gka_ref, gqb_ref, gkb_ref, bd_ref,
                   aq_ref, ak_ref, av_ref, bq_ref, bkv_ref, iq_ref, ik_ref, iw_ref):
    x = x_ref[0]
    h = _modulated_norm(x, g_ref[...], mod_ref[0, 1:2, :], mod_ref[0, 0:1, :]).astype(BF16)
    bd = bd_ref[...]
    aq_ref[0] = _head_norm(_dot(h, waq_ref[...]), gqa_ref[...], bd).astype(BF16)
    ak_ref[0] = _head_norm(_dot(h, wak_ref[...]), gka_ref[...], bd).astype(BF16)
    av_ref[0] = _dot(h, wav_ref[...]).astype(BF16)
    bq_ref[0] = _head_norm(_dot(h, wbq_ref[...]), gqb_ref[...], bd).astype(BF16)
    kv = _dot(h, wbkv_ref[...])
    nk = B_KV_HEADS * HEAD_DIM
    bk = _head_norm(kv[:, :nk], gkb_ref[...], bd[:nk, :nk])
    bkv_ref[0] = jnp.concatenate([bk, kv[:, nk:]], axis=1).astype(BF16)
    iq_ref[0] = _dot(h, wiq_ref[...]).astype(BF16)
    ikw = _dot(h, wik_ref[...])
    ik_ref[0] = ikw[:, :2 * IDX_DIM].astype(BF16)
    iw_ref[0] = ikw[:, 2 * IDX_DIM:2 * IDX_DIM + IDX_HEADS] * ((IDX_HEADS * IDX_DIM) ** -0.5)


def _inproj_call(x, mod, g_attn, w_in, gqa, gka, gqb, gkb, ts):
    bsz, s, d = x.shape
    na = A_HEADS * 2 * HEAD_DIM
    nb = B_HEADS * HEAD_DIM
    nkv = B_KV_HEADS * HEAD_DIM
    ni = IDX_HEADS * IDX_DIM
    o = 0
    parts = []
    for width in (na, na, A_HEADS * A_VDIM, nb, 2 * nkv, ni, IDX_DIM + IDX_HEADS):
        parts.append(w_in[:, o:o + width].astype(BF16))
        o += width
    wik = jnp.concatenate([parts[-1][:, :IDX_DIM], parts[-1]], axis=1)
    parts[-1] = jnp.pad(wik, ((0, 0), (0, 2 * LANES - wik.shape[1])))
    grp = jnp.arange(2 * LANES) // HEAD_DIM
    bd = (grp[:, None] == grp[None, :]).astype(BF16)

    def full(a):
        return pl.BlockSpec(a.shape, lambda b, i: (0,) * a.ndim)

    def tok(n):
        return pl.BlockSpec((1, ts, n), lambda b, i: (b, i, 0))

    consts = parts + [gqa, gka, gqb, gkb, bd]
    out_w = [(na, BF16), (na, BF16), (A_HEADS * A_VDIM, BF16), (nb, BF16), (2 * nkv, BF16),
             (ni, BF16), (2 * IDX_DIM, BF16), (IDX_HEADS, F32)]
    return pl.pallas_call(
        _inproj_kernel,
        grid=(bsz, s // ts),
        in_specs=[tok(d), pl.BlockSpec((1, 6, d), lambda b, i: (b, 0, 0)), full(g_attn)]
                 + [full(a) for a in consts],
        out_specs=[tok(n) for n, _ in out_w],
        out_shape=[jax.ShapeDtypeStruct((bsz, s, n), dt) for n, dt in out_w],
        compiler_params=pltpu.CompilerParams(dimension_semantics=("arbitrary", "arbitrary"),
                                             vmem_limit_bytes=VMEM_LIMIT),
        name="in_proj",
    )(x, mod, g_attn, *consts)


def _diff_kernel(lam_ref, g_ref, bound_ref, q_ref, k_ref, v_ref, bias_ref, o_ref, acc_ref, *,
                 t, nq, lam_init):
    i = pl.program_id(2)
    q = q_ref[0]
    lane = lax.broadcasted_iota(I32, q.shape, 1)
    zero = jnp.zeros_like(q)
    qq = jnp.concatenate([jnp.where(lane < HEAD_DIM, q, zero),
                          jnp.where(lane >= HEAD_DIM, q, zero)], axis=0)

    def rows(start, width):
        return pl.ds(start if isinstance(start, int) else pl.multiple_of(start, t), width)

    def logits(start, width):
        return _dot_nt(qq, k_ref[0, rows(start, width), :])

    def v_ones(start, width):
        v = v_ref[0, rows(start, width), :]
        return jnp.concatenate([v, jnp.ones((width, LANES), BF16)], axis=1)

    def near_logits(blk=None):
        blk = i if blk is None else blk
        b = jnp.concatenate([bias_ref[0, 1], bias_ref[0, 0]], axis=1)
        return logits((blk - 1) * t, 2 * t) + jnp.concatenate([b, b], axis=0)

    def first_logits():
        b = bias_ref[0, 0]
        return logits(0, t) + jnp.concatenate([b, b], axis=0)

    def for_far_blocks(fn):
        nfar = jnp.maximum(i - 1, 0)
        nsup = nfar // FAR_BLOCKS

        def sup(j, c):
            fn(j * (FAR_BLOCKS * t), FAR_BLOCKS * t)
            return c

        def single(j, c):
            fn(j * t, t)
            return c

        lax.fori_loop(0, nsup, sup, 0)
        lax.fori_loop(nsup * FAR_BLOCKS, nfar, single, 0)

    def accumulate(m):
        acc_ref[...] = jnp.zeros(acc_ref.shape, F32)

        def far(start, width):
            p = jnp.exp2(logits(start, width) - m).astype(BF16)
            acc_ref[...] += _dot(p, v_ones(start, width))

        for_far_blocks(far)

        @pl.when(i > 0)
        def _():
            p = jnp.exp2(near_logits() - m).astype(BF16)
            acc_ref[...] += _dot(p, v_ones((i - 1) * t, 2 * t))

        @pl.when(i == 0)
        def _():
            p = jnp.exp2(first_logits() - m).astype(BF16)
            acc_ref[...] += _dot(p, v_ones(0, t))

    def accumulate_at(blk):
        m = bound_ref[0, 0:1, 0:1]
        acc = jnp.zeros(acc_ref.shape, F32)
        for j in range(max(blk - 1, 0)):
            acc += _dot(jnp.exp2(logits(j * t, t) - m).astype(BF16), v_ones(j * t, t))
        if blk > 0:
            acc += _dot(jnp.exp2(near_logits(blk) - m).astype(BF16), v_ones((blk - 1) * t, 2 * t))
        else:
            acc += _dot(jnp.exp2(first_logits() - m).astype(BF16), v_ones(0, t))
        acc_ref[...] = acc

    for blk in range(nq):
        pl.when(i == blk)(functools.partial(accumulate_at, blk))

    @pl.when(jnp.min(acc_ref[:, LANES:LANES + 1]) < L_TINY)
    def _():
        acc_ref[...] = jnp.full(acc_ref.shape, MASKED, F32)

        def fold_max(s):
            w = acc_ref.shape[1]
            for c in range(s.shape[1] // w):
                acc_ref[...] = jnp.maximum(acc_ref[...], s[:, c * w:(c + 1) * w])

        for_far_blocks(lambda start, width: fold_max(logits(start, width)))

        @pl.when(i > 0)
        def _():
            fold_max(near_logits())

        @pl.when(i == 0)
        def _():
            fold_max(first_logits())

        accumulate(jnp.max(acc_ref[...], axis=1, keepdims=True))

    lv = lam_ref[...]
    lam = (jnp.exp(jnp.sum(lv[0:1] * lv[1:2], axis=1, keepdims=True))
           - jnp.exp(jnp.sum(lv[2:3] * lv[3:4], axis=1, keepdims=True)) + lam_init)
    acc = acc_ref[...]
    o = acc[:, :LANES] / acc[:, LANES:]
    o = o[:t] - lam * o[t:]
    o = o * lax.rsqrt(jnp.mean(o * o, axis=-1, keepdims=True) + EPS)
    o_ref[0] = (o * g_ref[...] * (1.0 - lam_init)).astype(BF16)


def _diff_call(lam_vecs, subln, bound, aq, ak, av, bias, t, lam_init):
    bsz, s, _ = aq.shape
    assert t % (2 * LANES) == 0
    return pl.pallas_call(
        functools.partial(_diff_kernel, t=t, nq=s // t, lam_init=lam_init),
        grid=(bsz, A_HEADS, s // t),
        in_specs=[pl.BlockSpec(lam_vecs.shape, lambda b, h, i: (0, 0)),
                  pl.BlockSpec(subln.shape, lambda b, h, i: (0, 0)),
                  pl.BlockSpec((1, 8, LANES), lambda b, h, i: (0, 0, 0)),
                  pl.BlockSpec((1, t, A_VDIM), lambda b, h, i: (b, i, h)),
                  pl.BlockSpec((1, s, A_VDIM), lambda b, h, i: (b, 0, h)),
                  pl.BlockSpec((1, s, A_VDIM), lambda b, h, i: (b, 0, h)),
                  pl.BlockSpec((1, 2, t, t), lambda b, h, i: (h, 0, 0, 0))],
        out_specs=pl.BlockSpec((1, t, A_VDIM), lambda b, h, i: (b, i, h)),
        out_shape=jax.ShapeDtypeStruct((bsz, s, A_HEADS * A_VDIM), BF16),
        scratch_shapes=[pltpu.VMEM((2 * t, 2 * LANES), F32)],
        compiler_params=pltpu.CompilerParams(
            dimension_semantics=("arbitrary", "arbitrary", "arbitrary"),
            vmem_limit_bytes=VMEM_LIMIT),
        name="diff_attn",
    )(lam_vecs, subln, bound, aq, ak, av, bias)


def _sortable_key(x):
    b = pltpu.bitcast(x, I32)
    return b ^ ((b >> 31) & 0x7FFFFFFF)


def _dsa_kernel(bound_ref, iq_ref, iw_ref, ik_ref, q_ref, kv_ref, bias_ref, o_ref,
                keys_ref, iqz_ref, wb_ref, thr_ref, tie_ref, acc_ref, *, t, topk):
    i = pl.program_id(1)
    row = lax.broadcasted_iota(I32, (t, t), 0)
    col = lax.broadcasted_iota(I32, (t, t), 1)

    iq = iq_ref[0]
    iw = iw_ref[0]
    lane = lax.broadcasted_iota(I32, (t, LANES), 1)
    for h in range(IDX_HEADS):
        pair = iq[:, (h // 2) * LANES:(h // 2 + 1) * LANES]
        mine = (lane < IDX_DIM) if h % 2 == 0 else (lane >= IDX_DIM)
        iqz_ref[h] = jnp.where(mine, pair, jnp.zeros_like(pair))
        wb_ref[h] = jnp.broadcast_to(iw[:, h:h + 1], (t, LANES))

    def score_block(j, causal):
        kc = ik_ref[0, pl.ds(pl.multiple_of(j * t, t), t), :]
        acc = jnp.zeros((t, t), F32)
        for h in range(IDX_HEADS):
            wb = wb_ref[h]
            wb = jnp.concatenate([wb] * (t // LANES), axis=1)
            acc = acc + wb * jnp.maximum(_dot_nt(iqz_ref[h], kc), 0.0)
        if causal:
            acc = jnp.where(row >= col, acc, -jnp.inf)
        keys_ref[j] = _sortable_key(acc)

    def score_body(j, carry):
        score_block(j, False)
        return carry

    lax.fori_loop(0, i, score_body, 0)
    score_block(i, True)

    def count(pred, ref_val, nfull):
        cnts = []
        for r in range(t // SEL_ROWS):
            c = ref_val[r * SEL_ROWS:(r + 1) * SEL_ROWS]

            def add_block(j, cnt, lane_groups, r=r, c=c):
                k = keys_ref[j, pl.ds(r * SEL_ROWS, SEL_ROWS), :]
                for w in range(lane_groups):
                    cnt = cnt + jnp.where(pred(k[:, w * LANES:(w + 1) * LANES], c), 1.0, 0.0)
                return cnt

            cnt = jnp.zeros((SEL_ROWS, LANES), F32)
            if isinstance(nfull, int):
                for j in range(nfull):
                    cnt = add_block(j, cnt, t // LANES)
            else:
                cnt = lax.fori_loop(0, nfull, lambda j, cnt, f=add_block: f(j, cnt, t // LANES), cnt)
            cnts.append(add_block(nfull, cnt, -(-(r + 1) * SEL_ROWS // LANES)))
        cnt = jnp.concatenate(cnts, axis=0)
        return jnp.broadcast_to(jnp.sum(cnt, axis=1, keepdims=True), cnt.shape)

    def select(nfull):
        def bit_body(it, thr):
            cand = thr + jnp.left_shift(jnp.int32(1), 31 - it)
            return jnp.where(count(lambda k, c: k >= c, cand, nfull) >= float(topk), cand, thr)

        thr = lax.fori_loop(0, 32, bit_body, jnp.full((t, LANES), INT_MIN, I32))
        excess = (count(lambda k, c: k >= c, thr, nfull) > float(topk)) & (thr > NEG_KEY)
        thr_ref[...] = jnp.concatenate([thr] * (t // LANES), axis=1)
        tie_ref[...] = jnp.where(excess, 1.0, 0.0)

    for n in range(keys_ref.shape[0]):
        pl.when(i == n)(functools.partial(select, n))

    @pl.when(jnp.max(tie_ref[...]) > 0.0)
    def _():
        tri = (row <= col).astype(BF16)
        thr2 = thr_ref[...]
        excess = tie_ref[...] > 0.0
        need = float(topk) - count(lambda k, c: k > c, thr2[:, :LANES], i)
        need2 = jnp.concatenate([need] * (t // LANES), axis=1)
        excess2 = jnp.concatenate([excess] * (t // LANES), axis=1)

        def body(j, seen):
            k = keys_ref[j]
            eq = k == thr2
            eqf = jnp.where(eq, 1.0, 0.0)
            rank = seen + _dot(eqf.astype(BF16), tri)
            keys_ref[j] = jnp.where(eq & (rank > need2) & excess2, INT_MIN, k)
            return seen + jnp.sum(eqf, axis=1, keepdims=True)

        lax.fori_loop(0, i + 1, body, jnp.zeros((t, 1), F32))

    thr_ref[...] = jnp.maximum(thr_ref[...], NEG_KEY + 1)

    q = q_ref[0]
    qs = []
    for g in range(B_KV_HEADS):
        parts = []
        for r in range(B_GROUP):
            hq = q[:, (g * B_GROUP + r) * HEAD_DIM:(g * B_GROUP + r + 1) * HEAD_DIM]
            z = jnp.zeros_like(hq)
            parts.append(jnp.concatenate([hq, z] if g == 0 else [z, hq], axis=1))
        qs.append(jnp.concatenate(parts, axis=0))

    def logits(j, g, near):
        k = kv_ref[0, pl.ds(pl.multiple_of(j * t, t), t), :LANES]
        madd = jnp.where(keys_ref[j] >= thr_ref[...], 0.0, MASKED)
        s = _dot_nt(qs[g], k).reshape(B_GROUP, t, t)
        if near is None:
            s = s + madd[None]
        else:
            s = s + (bias_ref[g * B_GROUP:(g + 1) * B_GROUP, near] + madd[None])
        return s.reshape(B_GROUP * t, t)

    def for_blocks(fn):
        def far(j, c):
            fn(j, None)
            return c

        lax.fori_loop(0, i - 1, far, 0)

        @pl.when(i > 0)
        def _():
            fn(i - 1, 1)

        fn(i, 0)

    def accumulate(ms):
        acc_ref[...] = jnp.zeros(acc_ref.shape, F32)

        def step(j, near):
            v = kv_ref[0, pl.ds(pl.multiple_of(j * t, t), t), LANES:]
            vx = jnp.concatenate([v, jnp.ones((t, LANES), BF16)], axis=1)
            for g in range(B_KV_HEADS):
                p = jnp.exp2(logits(j, g, near) - ms[g]).astype(BF16)
                acc_ref[g] += _dot(p, vx)

        for_blocks(step)

    m0 = bound_ref[0, 0:1, 0:1]
    accumulate([m0, m0])

    @pl.when(jnp.min(acc_ref[:, :, LANES:LANES + 1]) < L_TINY)
    def _():
        acc_ref[...] = jnp.full(acc_ref.shape, MASKED, F32)

        def step(j, near):
            w = acc_ref.shape[2]
            for g in range(B_KV_HEADS):
                s = logits(j, g, near)
                for c in range(t // w):
                    acc_ref[g] = jnp.maximum(acc_ref[g], s[:, c * w:(c + 1) * w])

        for_blocks(step)
        accumulate([jnp.max(acc_ref[g], axis=1, keepdims=True) for g in range(B_KV_HEADS)])

    outs = []
    for g in range(B_KV_HEADS):
        a = acc_ref[g]
        o = a[:, g * HEAD_DIM:(g + 1) * HEAD_DIM] / a[:, LANES:LANES + HEAD_DIM]
        for r in range(B_GROUP):
            outs.append(o[r * t:(r + 1) * t])
    o_ref[0] = jnp.concatenate(outs, axis=1).astype(BF16)


def _dsa_call(bound, iq, iw, ik, bq, bkv, bias, t, topk):
    bsz, s, _ = bq.shape
    nq = s // t
    assert t % SEL_ROWS == 0 and t % (2 * LANES) == 0
    return pl.pallas_call(
        functools.partial(_dsa_kernel, t=t, topk=topk),
        grid=(bsz, nq),
        in_specs=[pl.BlockSpec((1, 8, LANES), lambda b, i: (1, 0, 0)),
                  pl.BlockSpec((1, t, iq.shape[2]), lambda b, i: (b, i, 0)),
                  pl.BlockSpec((1, t, iw.shape[2]), lambda b, i: (b, i, 0)),
                  pl.BlockSpec((1, s, ik.shape[2]), lambda b, i: (b, 0, 0)),
                  pl.BlockSpec((1, t, bq.shape[2]), lambda b, i: (b, i, 0)),
                  pl.BlockSpec((1, s, bkv.shape[2]), lambda b, i: (b, 0, 0)),
                  pl.BlockSpec(bias.shape, lambda b, i: (0, 0, 0, 0),
                               pipeline_mode=pl.Buffered(1))],
        out_specs=pl.BlockSpec((1, t, B_HEADS * HEAD_DIM), lambda b, i: (b, i, 0)),
        out_shape=jax.ShapeDtypeStruct((bsz, s, B_HEADS * HEAD_DIM), BF16),
        scratch_shapes=[pltpu.VMEM((nq, t, t), I32),
                        pltpu.VMEM((IDX_HEADS, t, LANES), BF16),
                        pltpu.VMEM((IDX_HEADS, t, LANES), F32),
                        pltpu.VMEM((t, t), I32),
                        pltpu.VMEM((t, LANES), F32),
                        pltpu.VMEM((B_KV_HEADS, B_GROUP * t, 2 * LANES), F32)],
        compiler_params=pltpu.CompilerParams(dimension_semantics=("arbitrary", "arbitrary"),
                                             vmem_limit_bytes=VMEM_LIMIT),
        name="dsa_attn",
    )(bound, iq, iw, ik, bq, bkv, bias)


def _ffn_kernel(x_ref, oa_ref, ob_ref, mod_ref, g_ref, wo_ref, wgv_ref, cwg_ref, cwv_ref,
                cbg_ref, cbv_ref, wd_ref, o_ref, carry_ref, h_ref, y_ref, ua_ref, ub_ref, a_ref, *, nf):
    i = pl.program_id(1)
    x = x_ref[0]
    ts = x.shape[0]
    mixed = _dot(oa_ref[0], wo_ref[0]) + _dot(ob_ref[0], wo_ref[1])
    x1 = x + mod_ref[0, 2:3, :] * mixed
    h_ref[...] = _modulated_norm(x1, g_ref[...], mod_ref[0, 4:5, :], mod_ref[0, 3:4, :]).astype(BF16)
    y_ref[...] = x1

    @pl.when(i == 0)
    def _():
        carry_ref[...] = jnp.zeros(carry_ref.shape, F32)

    def up(f, buf_ref):
        u = _dot(h_ref[...], wgv_ref[f])
        fc = u.shape[1] // 2
        buf_ref[0, 8:8 + ts] = u[:, :fc]
        buf_ref[1, 8:8 + ts] = u[:, fc:]

    def conv(buf_ref, part, prev8, cw, cb):
        buf_ref[part, 0:8] = prev8
        return (cb + cw[2:3] * buf_ref[part, 8:8 + ts] + cw[1:2] * buf_ref[part, 7:7 + ts]
                + cw[0:1] * buf_ref[part, 6:6 + ts])

    def gate(f, buf_ref):
        yg = conv(buf_ref, 0, carry_ref[f, 0], cwg_ref[f], cbg_ref[f])
        yv = conv(buf_ref, 1, carry_ref[f, 1], cwv_ref[f], cbv_ref[f])
        carry_ref[f, 0] = buf_ref[0, ts:ts + 8]
        carry_ref[f, 1] = buf_ref[1, ts:ts + 8]
        a_ref[f] = ((yg / (1.0 + jnp.exp(-yg))) * yv).astype(BF16)

    def stage(f, cur_ref, nxt_ref):
        gate(f, cur_ref)
        up(f + 1, nxt_ref)

    up(0, ua_ref)

    assert nf % 2 == 1
    for k in range((nf - 1) // 2):
        stage(2 * k, ua_ref, ub_ref)
        stage(2 * k + 1, ub_ref, ua_ref)
    gate(nf - 1, ua_ref)
    fc = a_ref.shape[2]
    down = None
    for lo in range(0, nf, DOWN_GROUP):
        hi = min(lo + DOWN_GROUP, nf)
        a = jnp.concatenate([a_ref[f] for f in range(lo, hi)], axis=1)
        part = _dot(a, wd_ref[lo * fc:hi * fc, :])
        down = part if down is None else down + part
    o_ref[0] = y_ref[...] + mod_ref[0, 5:6, :] * down


def _ffn_call(x, oa, ob, mod, g_ffn, w_out, w_up, conv_w, conv_b, w_down, ts, fc):
    bsz, s, d = x.shape
    dff = w_down.shape[0]
    nf = dff // fc
    na = oa.shape[2]
    wo = w_out.astype(BF16).reshape(2, na, d)
    wup = w_up.astype(BF16)
    wgv = jnp.concatenate([wup[:, :dff].reshape(d, nf, fc), wup[:, dff:].reshape(d, nf, fc)],
                          axis=2).transpose(1, 0, 2)
    cwg = conv_w[:, :dff].reshape(CONV_W, nf, fc).transpose(1, 0, 2)
    cwv = conv_w[:, dff:].reshape(CONV_W, nf, fc).transpose(1, 0, 2)
    cbg = conv_b[:dff].reshape(nf, 1, fc)
    cbv = conv_b[dff:].reshape(nf, 1, fc)
    wd = w_down.astype(BF16)

    def full(a):
        return pl.BlockSpec(a.shape, lambda b, i: (0,) * a.ndim, pipeline_mode=pl.Buffered(1))

    def tok(n):
        return pl.BlockSpec((1, ts, n), lambda b, i: (b, i, 0))

    consts = [wo, wgv, cwg, cwv, cbg, cbv, wd]
    return pl.pallas_call(
        functools.partial(_ffn_kernel, nf=nf),
        grid=(bsz, s // ts),
        in_specs=[tok(d), tok(na), tok(ob.shape[2]),
                  pl.BlockSpec((1, 6, d), lambda b, i: (b, 0, 0)), full(g_ffn)]
                 + [full(a) for a in consts],
        out_specs=tok(d),
        out_shape=jax.ShapeDtypeStruct((bsz, s, d), F32),
        scratch_shapes=[pltpu.VMEM((nf, 2, 8, fc), F32), pltpu.VMEM((ts, d), BF16),
                        pltpu.VMEM((ts, d), F32), pltpu.VMEM((2, ts + 8, fc), F32),
                        pltpu.VMEM((2, ts + 8, fc), F32), pltpu.VMEM((nf, ts, fc), BF16)],
        compiler_params=pltpu.CompilerParams(dimension_semantics=("arbitrary", "arbitrary"),
                                             vmem_limit_bytes=VMEM_LIMIT),
        name="outproj_ffn",
    )(x, oa, ob, mod, g_ffn, *consts)


def kernel(x, c, w_ada, b_ada, g_attn, w_in, q_norm_a, k_norm_a, q_norm_b, k_norm_b, lam_vecs,
           subln_a, w_out, g_ffn, w_up, conv_w, conv_b, w_down, rel_bias):
    bsz, s, d = x.shape
    t = min(ATT_T, s)
    ta = min(DIFF_T, s)
    ts = min(512, s)
    topk = min(TOPK_MAX, s // 4)
    scale = HEAD_DIM ** -0.5
    for l in range(w_ada.shape[0]):
        mod = _ada_call(c, w_ada[l], b_ada[l]).reshape(bsz, 6, d)
        gains = jnp.stack([q_norm_a[l], k_norm_a[l], q_norm_b[l], k_norm_b[l]])
        bias_a, bound = _bias_call(rel_bias, gains, ta, 0, A_HEADS)
        bias_b, _ = _bias_call(rel_bias, gains, t, A_HEADS, B_HEADS)
        gqa = (jnp.tile(q_norm_a[l], 2 * A_HEADS) * (scale * LOG2E))[None]
        gka = jnp.tile(k_norm_a[l], 2 * A_HEADS)[None]
        gqb = (jnp.tile(q_norm_b[l], B_HEADS) * (scale * LOG2E))[None]
        gkb = jnp.tile(k_norm_b[l], B_KV_HEADS)[None]
        aq, ak, av, bq, bkv, iq, ik, iw = _inproj_call(
            x, mod, g_attn[l][None], w_in[l], gqa, gka, gqb, gkb, ts)
        lam_init = 0.8 - 0.6 * math.exp(-0.3 * l)
        o_a = _diff_call(lam_vecs[l], subln_a[l][None], bound, aq, ak, av, bias_a, ta, lam_init)
        o_b = _dsa_call(bound, iq, iw, ik, bq, bkv, bias_b, t, topk)
        x = _ffn_call(x, o_a, o_b, mod, g_ffn[l][None], w_out[l], w_up[l], conv_w[l], conv_b[l],
                      w_down[l], ts, 256)
    return x
```

```python
import functools
import math

import jax
import jax.numpy as jnp
from jax import lax
from jax.experimental import pallas as pl
from jax.experimental.pallas import tpu as pltpu

F32 = jnp.float32
BF16 = jnp.bfloat16
I32 = jnp.int32

HEAD_DIM = 64
A_HEADS = 4
A_VDIM = 2 * HEAD_DIM
B_HEADS = 8
B_KV_HEADS = 2
B_GROUP = B_HEADS // B_KV_HEADS
IDX_HEADS = 8
IDX_DIM = 64
TOPK_MAX = 256
N_BUCKETS = 32
MAX_DISTANCE = 128
CONV_W = 3
EPS = 1e-6

LANES = 128
ATT_T = 512
SEL_ROWS = 128
DIFF_T = 512
FAR_BLOCKS = 2
MASKED = -1e30
LOG2E = math.log2(math.e)
L_TINY = 2.0 ** -60
NEG_KEY = -2139095041
INT_MIN = -2147483648
VMEM_LIMIT = 56 * 1024 * 1024


def _dot(a, b):
    return jnp.dot(a, b, preferred_element_type=F32)


def _dot_nt(a, b):
    return lax.dot_general(a, b, (((1,), (1,)), ((), ())), preferred_element_type=F32)


def _split_bf16(a):
    hi = a.astype(BF16)
    lo = (a - hi.astype(F32)).astype(BF16)
    return hi, lo


def _ada_kernel(c_ref, w_ref, b_ref, o_ref):
    c = c_ref[...]
    a = c / (1.0 + jnp.exp(-c))
    a_hi, a_lo = _split_bf16(a)
    w_hi, w_lo = _split_bf16(w_ref[...])
    o_ref[...] = _dot(a_hi, w_hi) + _dot(a_hi, w_lo) + _dot(a_lo, w_hi) + b_ref[...]


def _ada_call(c, w, b):
    bsz, d = c.shape
    n = w.shape[1]
    tn = 1024
    return pl.pallas_call(
        _ada_kernel,
        grid=(n // tn,),
        in_specs=[pl.BlockSpec((bsz, d), lambda j: (0, 0)),
                  pl.BlockSpec((d, tn), lambda j: (0, j)),
                  pl.BlockSpec((1, tn), lambda j: (0, j))],
        out_specs=pl.BlockSpec((bsz, tn), lambda j: (0, j)),
        out_shape=jax.ShapeDtypeStruct((bsz, n), F32),
        compiler_params=pltpu.CompilerParams(dimension_semantics=("arbitrary",),
                                             vmem_limit_bytes=VMEM_LIMIT),
        name="ada_mod",
    )(c, w, b.reshape(1, n))


def _bias_kernel(tab_ref, tabv_ref, gains_ref, o_ref, bound_ref, *, t, h0):
    h = pl.program_id(0) + h0
    max_exact = N_BUCKETS // 2
    b = MAX_DISTANCE
    row = lax.broadcasted_iota(I32, (b, b), 0)
    col = lax.broadcasted_iota(I32, (b, b), 1)
    last = tab_ref[N_BUCKETS - 1, h]
    base = []
    for d in range(2):
        n = jnp.maximum(row - col + d * b, 0)
        nf = jnp.maximum(n, 1).astype(F32)
        large = max_exact + (jnp.log(nf / max_exact) / math.log(MAX_DISTANCE / max_exact)
                             * (N_BUCKETS - max_exact)).astype(I32)
        large = jnp.minimum(large, N_BUCKETS - 1)
        bucket = jnp.where(n < max_exact, n, large)
        acc = jnp.zeros((b, b), F32)
        for k in range(N_BUCKETS - 1):
            acc = jnp.where(bucket == k, (tab_ref[k, h] - last) * LOG2E, acc)
        base.append(acc)
    by_distance = {0: jnp.where(row >= col, base[0], MASKED), 1: base[1]}
    masked = jnp.full((b, b), MASKED, F32)
    zero = jnp.zeros((b, b), F32)
    nb = t // b
    for d in range(2):
        o_ref[0, d] = jnp.concatenate(
            [jnp.concatenate([masked if d * nb + a - c < 0 else by_distance.get(d * nb + a - c, zero)
                              for c in range(nb)], axis=1) for a in range(nb)], axis=0)
    tab = tabv_ref[...]
    rel = (tab - tab[N_BUCKETS - 1:N_BUCKETS, :]) * LOG2E
    hcol = lax.broadcasted_iota(I32, rel.shape, 1)
    g = jnp.max(jnp.abs(gains_ref[...]), axis=1, keepdims=True)
    for grp in range(2):
        in_grp = (hcol < A_HEADS) if grp == 0 else (hcol >= A_HEADS)
        bmax = jnp.max(jnp.where(in_grp, rel, 0.0), keepdims=True)
        qk = g[2 * grp:2 * grp + 1] * g[2 * grp + 1:2 * grp + 2] * (HEAD_DIM ** 0.5 * LOG2E)
        bound_ref[grp] = jnp.broadcast_to(qk + bmax, bound_ref.shape[1:])


def _bias_call(rel_bias, gains, t, h0, nh):
    return pl.pallas_call(
        functools.partial(_bias_kernel, t=t, h0=h0),
        grid=(nh,),
        in_specs=[pl.BlockSpec(memory_space=pltpu.SMEM),
                  pl.BlockSpec(rel_bias.shape, lambda h: (0, 0)),
                  pl.BlockSpec(gains.shape, lambda h: (0, 0))],
        out_specs=[pl.BlockSpec((1, 2, t, t), lambda h: (h, 0, 0, 0)),
                   pl.BlockSpec((2, 8, LANES), lambda h: (0, 0, 0))],
        out_shape=[jax.ShapeDtypeStruct((nh, 2, t, t), F32),
                   jax.ShapeDtypeStruct((2, 8, LANES), F32)],
        compiler_params=pltpu.CompilerParams(dimension_semantics=("arbitrary",)),
        name="bias_tiles",
    )(rel_bias, rel_bias, gains)


def _modulated_norm(x, g, scale, shift):
    y = x * lax.rsqrt(jnp.mean(x * x, axis=-1, keepdims=True) + EPS)
    return (y * g) * (1.0 + scale) + shift


def _head_norm(y, gain, bd):
    y2 = (y * y).astype(BF16)
    w = bd.shape[0]
    ss = jnp.concatenate([_dot(y2[:, c:c + w], bd) for c in range(0, y.shape[1], w)], axis=1)
    return y * lax.rsqrt(ss * (1.0 / HEAD_DIM) + EPS) * gain


def _inproj_kernel(x_ref, mod_ref, g_ref, waq_ref, wak_ref, wav_ref, wbq_ref, wbkv_ref, wiq_ref,
                   wik_ref, gqa_ref, gka_ref, gqb_ref, gkb_ref, bd_ref,
                   aq_ref, ak_ref, av_ref, bq_ref, bkv_ref, iq_ref, ik_ref, iw_ref):
    x = x_ref[0]
    h = _modulated_norm(x, g_ref[...], mod_ref[0, 1:2, :], mod_ref[0, 0:1, :]).astype(BF16)
    bd = bd_ref[...]
    aq_ref[0] = _head_norm(_dot(h, waq_ref[...]), gqa_ref[...], bd).astype(BF16)
    ak_ref[0] = _head_norm(_dot(h, wak_ref[...]), gka_ref[...], bd).astype(BF16)
    av_ref[0] = _dot(h, wav_ref[...]).astype(BF16)
    bq_ref[0] = _head_norm(_dot(h, wbq_ref[...]), gqb_ref[...], bd).astype(BF16)
    kv = _dot(h, wbkv_ref[...])
    nk = B_KV_HEADS * HEAD_DIM
    bk = _head_norm(kv[:, :nk], gkb_ref[...], bd[:nk, :nk])
    bkv_ref[0] = jnp.concatenate([bk, kv[:, nk:]], axis=1).astype(BF16)
    iq_ref[0] = _dot(h, wiq_ref[...]).astype(BF16)
    ikw = _dot(h, wik_ref[...])
    ik_ref[0] = ikw[:, :2 * IDX_DIM].astype(BF16)
    iw_ref[0] = ikw[:, 2 * IDX_DIM:2 * IDX_DIM + IDX_HEADS] * ((IDX_HEADS * IDX_DIM) ** -0.5)


def _inproj_call(x, mod, g_attn, w_in, gqa, gka, gqb, gkb, ts):
    bsz, s, d = x.shape
    na = A_HEADS * 2 * HEAD_DIM
    nb = B_HEADS * HEAD_DIM
    nkv = B_KV_HEADS * HEAD_DIM
    ni = IDX_HEADS * IDX_DIM
    o = 0
    parts = []
    for width in (na, na, A_HEADS * A_VDIM, nb, 2 * nkv, ni, IDX_DIM + IDX_HEADS):
        parts.append(w_in[:, o:o + width].astype(BF16))
        o += width
    wik = jnp.concatenate([parts[-1][:, :IDX_DIM], parts[-1]], axis=1)
    parts[-1] = jnp.pad(wik, ((0, 0), (0, 2 * LANES - wik.shape[1])))
    grp = jnp.arange(2 * LANES) // HEAD_DIM
    bd = (grp[:, None] == grp[None, :]).astype(BF16)

    def full(a):
        return pl.BlockSpec(a.shape, lambda b, i: (0,) * a.ndim)

    def tok(n):
        return pl.BlockSpec((1, ts, n), lambda b, i: (b, i, 0))

    consts = parts + [gqa, gka, gqb, gkb, bd]
    out_w = [(na, BF16), (na, BF16), (A_HEADS * A_VDIM, BF16), (nb, BF16), (2 * nkv, BF16),
             (ni, BF16), (2 * IDX_DIM, BF16), (IDX_HEADS, F32)]
    return pl.pallas_call(
        _inproj_kernel,
        grid=(bsz, s // ts),
        in_specs=[tok(d), pl.BlockSpec((1, 6, d), lambda b, i: (b, 0, 0)), full(g_attn)]
                 + [full(a) for a in consts],
        out_specs=[tok(n) for n, _ in out_w],
        out_shape=[jax.ShapeDtypeStruct((bsz, s, n), dt) for n, dt in out_w],
        compiler_params=pltpu.CompilerParams(dimension_semantics=("arbitrary", "arbitrary"),
                                             vmem_limit_bytes=VMEM_LIMIT),
        name="in_proj",
    )(x, mod, g_attn, *consts)


def _diff_kernel(lam_ref, g_ref, bound_ref, q_ref, k_ref, v_ref, bias_ref, o_ref, acc_ref, *,
                 t, lam_init):
    i = pl.program_id(2)
    q = q_ref[0]
    lane = lax.broadcasted_iota(I32, q.shape, 1)
    zero = jnp.zeros_like(q)
    qq = jnp.concatenate([jnp.where(lane < HEAD_DIM, q, zero),
                          jnp.where(lane >= HEAD_DIM, q, zero)], axis=0)

    def logits(start, width):
        k = k_ref[0, pl.ds(pl.multiple_of(start, t), width), :]
        return _dot_nt(qq, k)

    def v_ones(start, width):
        v = v_ref[0, pl.ds(pl.multiple_of(start, t), width), :]
        return jnp.concatenate([v, jnp.ones((width, LANES), BF16)], axis=1)

    def near_logits():
        b = jnp.concatenate([bias_ref[0, 1], bias_ref[0, 0]], axis=1)
        return logits((i - 1) * t, 2 * t) + jnp.concatenate([b, b], axis=0)

    def first_logits():
        b = bias_ref[0, 0]
        return logits(0, t) + jnp.concatenate([b, b], axis=0)

    def for_far_blocks(fn):
        nfar = jnp.maximum(i - 1, 0)
        nsup = nfar // FAR_BLOCKS

        def sup(j, c):
            fn(j * (FAR_BLOCKS * t), FAR_BLOCKS * t)
            return c

        def single(j, c):
            fn(j * t, t)
            return c

        lax.fori_loop(0, nsup, sup, 0)
        lax.fori_loop(nsup * FAR_BLOCKS, nfar, single, 0)

    def accumulate(m):
        acc_ref[...] = jnp.zeros(acc_ref.shape, F32)

        def far(start, width):
            p = jnp.exp2(logits(start, width) - m).astype(BF16)
            acc_ref[...] += _dot(p, v_ones(start, width))

        for_far_blocks(far)

        @pl.when(i > 0)
        def _():
            p = jnp.exp2(near_logits() - m).astype(BF16)
            acc_ref[...] += _dot(p, v_ones((i - 1) * t, 2 * t))

        @pl.when(i == 0)
        def _():
            p = jnp.exp2(first_logits() - m).astype(BF16)
            acc_ref[...] += _dot(p, v_ones(0, t))

    accumulate(bound_ref[0, 0:1, 0:1])

    @pl.when(jnp.min(acc_ref[:, LANES:LANES + 1]) < L_TINY)
    def _():
        acc_ref[...] = jnp.full(acc_ref.shape, MASKED, F32)

        def fold_max(s):
            w = acc_ref.shape[1]
            for c in range(s.shape[1] // w):
                acc_ref[...] = jnp.maximum(acc_ref[...], s[:, c * w:(c + 1) * w])

        for_far_blocks(lambda start, width: fold_max(logits(start, width)))

        @pl.when(i > 0)
        def _():
            fold_max(near_logits())

        @pl.when(i == 0)
        def _():
            fold_max(first_logits())

        accumulate(jnp.max(acc_ref[...], axis=1, keepdims=True))

    lv = lam_ref[...]
    lam = (jnp.exp(jnp.sum(lv[0:1] * lv[1:2], axis=1, keepdims=True))
           - jnp.exp(jnp.sum(lv[2:3] * lv[3:4], axis=1, keepdims=True)) + lam_init)
    acc = acc_ref[...]
    o = acc[:, :LANES] / acc[:, LANES:]
    o = o[:t] - lam * o[t:]
    o = o * lax.rsqrt(jnp.mean(o * o, axis=-1, keepdims=True) + EPS)
    o_ref[0] = (o * g_ref[...] * (1.0 - lam_init)).astype(BF16)


def _diff_call(lam_vecs, subln, bound, aq, ak, av, bias, t, lam_init):
    bsz, s, _ = aq.shape
    assert t % (2 * LANES) == 0
    return pl.pallas_call(
        functools.partial(_diff_kernel, t=t, lam_init=lam_init),
        grid=(bsz, A_HEADS, s // t),
        in_specs=[pl.BlockSpec(lam_vecs.shape, lambda b, h, i: (0, 0)),
                  pl.BlockSpec(subln.shape, lambda b, h, i: (0, 0)),
                  pl.BlockSpec((1, 8, LANES), lambda b, h, i: (0, 0, 0)),
                  pl.BlockSpec((1, t, A_VDIM), lambda b, h, i: (b, i, h)),
                  pl.BlockSpec((1, s, A_VDIM), lambda b, h, i: (b, 0, h)),
                  pl.BlockSpec((1, s, A_VDIM), lambda b, h, i: (b, 0, h)),
                  pl.BlockSpec((1, 2, t, t), lambda b, h, i: (h, 0, 0, 0))],
        out_specs=pl.BlockSpec((1, t, A_VDIM), lambda b, h, i: (b, i, h)),
        out_shape=jax.ShapeDtypeStruct((bsz, s, A_HEADS * A_VDIM), BF16),
        scratch_shapes=[pltpu.VMEM((2 * t, 2 * LANES), F32)],
        compiler_params=pltpu.CompilerParams(
            dimension_semantics=("arbitrary", "arbitrary", "arbitrary"),
            vmem_limit_bytes=VMEM_LIMIT),
        name="diff_attn",
    )(lam_vecs, subln, bound, aq, ak, av, bias)


def _sortable_key(x):
    b = pltpu.bitcast(x, I32)
    return b ^ ((b >> 31) & 0x7FFFFFFF)


def _dsa_kernel(bound_ref, iq_ref, iw_ref, ik_ref, q_ref, kv_ref, bias_ref, o_ref,
                keys_ref, iqz_ref, wb_ref, thr_ref, tie_ref, acc_ref, *, t, topk):
    i = pl.program_id(1)
    row = lax.broadcasted_iota(I32, (t, t), 0)
    col = lax.broadcasted_iota(I32, (t, t), 1)

    iq = iq_ref[0]
    iw = iw_ref[0]
    lane = lax.broadcasted_iota(I32, (t, LANES), 1)
    for h in range(IDX_HEADS):
        pair = iq[:, (h // 2) * LANES:(h // 2 + 1) * LANES]
        mine = (lane < IDX_DIM) if h % 2 == 0 else (lane >= IDX_DIM)
        iqz_ref[h] = jnp.where(mine, pair, jnp.zeros_like(pair))
        wb_ref[h] = jnp.broadcast_to(iw[:, h:h + 1], (t, LANES))

    def score_block(j, causal):
        kc = ik_ref[0, pl.ds(pl.multiple_of(j * t, t), t), :]
        acc = jnp.zeros((t, t), F32)
        for h in range(IDX_HEADS):
            wb = wb_ref[h]
            wb = jnp.concatenate([wb] * (t // LANES), axis=1)
            acc = acc + wb * jnp.maximum(_dot_nt(iqz_ref[h], kc), 0.0)
        if causal:
            acc = jnp.where(row >= col, acc, -jnp.inf)
        keys_ref[j] = _sortable_key(acc)

    def score_body(j, carry):
        score_block(j, False)
        return carry

    lax.fori_loop(0, i, score_body, 0)
    score_block(i, True)

    ngroups = t // SEL_ROWS

    def group_count(pred, c, r, nfull):
        def add_block(j, cnt, lane_groups):
            k = keys_ref[j, pl.ds(r * SEL_ROWS, SEL_ROWS), :]
            for w in range(lane_groups):
                cnt = cnt + jnp.where(pred(k[:, w * LANES:(w + 1) * LANES], c), 1.0, 0.0)
            return cnt

        cnt = jnp.zeros((SEL_ROWS, LANES), F32)
        if isinstance(nfull, int):
            for j in range(nfull):
                cnt = add_block(j, cnt, t // LANES)
        else:
            cnt = lax.fori_loop(0, nfull, lambda j, cnt: add_block(j, cnt, t // LANES), cnt)
        return add_block(nfull, cnt, -(-(r + 1) * SEL_ROWS // LANES))

    def row_total(cnt):
        return jnp.broadcast_to(jnp.sum(cnt, axis=1, keepdims=True), cnt.shape)

    def count(pred, ref_val, nfull):
        return jnp.concatenate(
            [row_total(group_count(pred, ref_val[r * SEL_ROWS:(r + 1) * SEL_ROWS], r, nfull))
             for r in range(ngroups)], axis=0)

    def ge(k, c):
        return k >= c

    def select(nfull):
        head = (ngroups - 1) * SEL_ROWS

        def resolve(thr_last, cand_last, pending):
            return jnp.where(row_total(pending) >= float(topk), cand_last, thr_last)

        def bit_body(it, carry):
            thr_head, thr_last, cand_last, pending = carry
            bit = jnp.left_shift(jnp.int32(1), 31 - it)
            thr_last = resolve(thr_last, cand_last, pending)
            cand_head = thr_head + bit
            cand_last = thr_last + bit
            keep = count_head(cand_head) >= float(topk)
            return (jnp.where(keep, cand_head, thr_head), thr_last, cand_last,
                    group_count(ge, cand_last, ngroups - 1, nfull))

        def count_head(cand_head):
            return jnp.concatenate(
                [row_total(group_count(ge, cand_head[r * SEL_ROWS:(r + 1) * SEL_ROWS], r, nfull))
                 for r in range(ngroups - 1)], axis=0)

        thr_ref[...] = jnp.full(thr_ref.shape, INT_MIN, I32)
        tie_ref[...] = jnp.zeros(tie_ref.shape, F32)
        start_last = thr_ref[head:, :LANES]
        thr_head, thr_last, cand_last, pending = lax.fori_loop(
            0, 32, bit_body, (thr_ref[:head, :LANES], start_last, start_last, tie_ref[head:]))
        thr = jnp.concatenate([thr_head, resolve(thr_last, cand_last, pending)], axis=0)
        excess = (count(lambda k, c: k >= c, thr, nfull) > float(topk)) & (thr > NEG_KEY)
        thr_ref[...] = jnp.concatenate([thr] * (t // LANES), axis=1)
        tie_ref[...] = jnp.where(excess, 1.0, 0.0)

    for n in range(keys_ref.shape[0]):
        pl.when(i == n)(functools.partial(select, n))

    @pl.when(jnp.max(tie_ref[...]) > 0.0)
    def _():
        tri = (row <= col).astype(BF16)
        thr2 = thr_ref[...]
        excess = tie_ref[...] > 0.0
        need = float(topk) - count(lambda k, c: k > c, thr2[:, :LANES], i)
        need2 = jnp.concatenate([need] * (t // LANES), axis=1)
        excess2 = jnp.concatenate([excess] * (t // LANES), axis=1)

        def body(j, seen):
            k = keys_ref[j]
            eq = k == thr2
            eqf = jnp.where(eq, 1.0, 0.0)
            rank = seen + _dot(eqf.astype(BF16), tri)
            keys_ref[j] = jnp.where(eq & (rank > need2) & excess2, INT_MIN, k)
            return seen + jnp.sum(eqf, axis=1, keepdims=True)

        lax.fori_loop(0, i + 1, body, jnp.zeros((t, 1), F32))

    thr_ref[...] = jnp.maximum(thr_ref[...], NEG_KEY + 1)

    q = q_ref[0]
    qs = []
    for g in range(B_KV_HEADS):
        parts = []
        for r in range(B_GROUP):
            hq = q[:, (g * B_GROUP + r) * HEAD_DIM:(g * B_GROUP + r + 1) * HEAD_DIM]
            z = jnp.zeros_like(hq)
            parts.append(jnp.concatenate([hq, z] if g == 0 else [z, hq], axis=1))
        qs.append(jnp.concatenate(parts, axis=0))

    def logits(j, g, near):
        k = kv_ref[0, pl.ds(pl.multiple_of(j * t, t), t), :LANES]
        madd = jnp.where(keys_ref[j] >= thr_ref[...], 0.0, MASKED)
        s = _dot_nt(qs[g], k).reshape(B_GROUP, t, t)
        if near is None:
            s = s + madd[None]
        else:
            s = s + (bias_ref[g * B_GROUP:(g + 1) * B_GROUP, near] + madd[None])
        return s.reshape(B_GROUP * t, t)

    def for_blocks(fn):
        def far(j, c):
            fn(j, None)
            return c

        lax.fori_loop(0, i - 1, far, 0)

        @pl.when(i > 0)
        def _():
            fn(i - 1, 1)

        fn(i, 0)

    def accumulate(ms):
        acc_ref[...] = jnp.zeros(acc_ref.shape, F32)

        def step(j, near):
            v = kv_ref[0, pl.ds(pl.multiple_of(j * t, t), t), LANES:]
            vx = jnp.concatenate([v, jnp.ones((t, LANES), BF16)], axis=1)
            for g in range(B_KV_HEADS):
                p = jnp.exp2(logits(j, g, near) - ms[g]).astype(BF16)
                acc_ref[g] += _dot(p, vx)

        for_blocks(step)

    m0 = bound_ref[0, 0:1, 0:1]
    accumulate([m0, m0])

    @pl.when(jnp.min(acc_ref[:, :, LANES:LANES + 1]) < L_TINY)
    def _():
        acc_ref[...] = jnp.full(acc_ref.shape, MASKED, F32)

        def step(j, near):
            w = acc_ref.shape[2]
            for g in range(B_KV_HEADS):
                s = logits(j, g, near)
                for c in range(t // w):
                    acc_ref[g] = jnp.maximum(acc_ref[g], s[:, c * w:(c + 1) * w])

        for_blocks(step)
        accumulate([jnp.max(acc_ref[g], axis=1, keepdims=True) for g in range(B_KV_HEADS)])

    outs = []
    for g in range(B_KV_HEADS):
        a = acc_ref[g]
        o = a[:, g * HEAD_DIM:(g + 1) * HEAD_DIM] / a[:, LANES:LANES + HEAD_DIM]
        for r in range(B_GROUP):
            outs.append(o[r * t:(r + 1) * t])
    o_ref[0] = jnp.concatenate(outs, axis=1).astype(BF16)


def _dsa_call(bound, iq, iw, ik, bq, bkv, bias, t, topk):
    bsz, s, _ = bq.shape
    nq = s // t
    assert t % SEL_ROWS == 0 and t % (2 * LANES) == 0
    return pl.pallas_call(
        functools.partial(_dsa_kernel, t=t, topk=topk),
        grid=(bsz, nq),
        in_specs=[pl.BlockSpec((1, 8, LANES), lambda b, i: (1, 0, 0)),
                  pl.BlockSpec((1, t, iq.shape[2]), lambda b, i: (b, i, 0)),
                  pl.BlockSpec((1, t, iw.shape[2]), lambda b, i: (b, i, 0)),
                  pl.BlockSpec((1, s, ik.shape[2]), lambda b, i: (b, 0, 0)),
                  pl.BlockSpec((1, t, bq.shape[2]), lambda b, i: (b, i, 0)),
                  pl.BlockSpec((1, s, bkv.shape[2]), lambda b, i: (b, 0, 0)),
                  pl.BlockSpec(bias.shape, lambda b, i: (0, 0, 0, 0),
                               pipeline_mode=pl.Buffered(1))],
        out_specs=pl.BlockSpec((1, t, B_HEADS * HEAD_DIM), lambda b, i: (b, i, 0)),
        out_shape=jax.ShapeDtypeStruct((bsz, s, B_HEADS * HEAD_DIM), BF16),
        scratch_shapes=[pltpu.VMEM((nq, t, t), I32),
                        pltpu.VMEM((IDX_HEADS, t, LANES), BF16),
                        pltpu.VMEM((IDX_HEADS, t, LANES), F32),
                        pltpu.VMEM((t, t), I32),
                        pltpu.VMEM((t, LANES), F32),
                        pltpu.VMEM((B_KV_HEADS, B_GROUP * t, 2 * LANES), F32)],
        compiler_params=pltpu.CompilerParams(dimension_semantics=("arbitrary", "arbitrary"),
                                             vmem_limit_bytes=VMEM_LIMIT),
        name="dsa_attn",
    )(bound, iq, iw, ik, bq, bkv, bias)


def _ffn_kernel(x_ref, oa_ref, ob_ref, mod_ref, g_ref, wo_ref, wgv_ref, cwg_ref, cwv_ref,
                cbg_ref, cbv_ref, wd_ref, o_ref, carry_ref, h_ref, y_ref, ua_ref, ub_ref, a_ref, *, nf):
    i = pl.program_id(1)
    x = x_ref[0]
    ts = x.shape[0]
    mixed = _dot(oa_ref[0], wo_ref[0]) + _dot(ob_ref[0], wo_ref[1])
    x1 = x + mod_ref[0, 2:3, :] * mixed
    h_ref[...] = _modulated_norm(x1, g_ref[...], mod_ref[0, 4:5, :], mod_ref[0, 3:4, :]).astype(BF16)
    y_ref[...] = x1

    @pl.when(i == 0)
    def _():
        carry_ref[...] = jnp.zeros(carry_ref.shape, F32)

    def up(f, buf_ref):
        u = _dot(h_ref[...], wgv_ref[f])
        fc = u.shape[1] // 2
        buf_ref[0, 8:8 + ts] = u[:, :fc]
        buf_ref[1, 8:8 + ts] = u[:, fc:]

    def conv(buf_ref, part, prev8, cw, cb):
        buf_ref[part, 0:8] = prev8
        return (cb + cw[2:3] * buf_ref[part, 8:8 + ts] + cw[1:2] * buf_ref[part, 7:7 + ts]
                + cw[0:1] * buf_ref[part, 6:6 + ts])

    def gate(f, buf_ref):
        yg = conv(buf_ref, 0, carry_ref[f, 0], cwg_ref[f], cbg_ref[f])
        yv = conv(buf_ref, 1, carry_ref[f, 1], cwv_ref[f], cbv_ref[f])
        carry_ref[f, 0] = buf_ref[0, ts:ts + 8]
        carry_ref[f, 1] = buf_ref[1, ts:ts + 8]
        a_ref[f] = ((yg / (1.0 + jnp.exp(-yg))) * yv).astype(BF16)

    def stage(f, cur_ref, nxt_ref):
        gate(f, cur_ref)
        up(f + 1, nxt_ref)

    up(0, ua_ref)

    assert nf % 2 == 1
    for k in range((nf - 1) // 2):
        stage(2 * k, ua_ref, ub_ref)
        stage(2 * k + 1, ub_ref, ua_ref)
    gate(nf - 1, ua_ref)
    a = jnp.concatenate([a_ref[f] for f in range(nf)], axis=1)
    o_ref[0] = y_ref[...] + mod_ref[0, 5:6, :] * _dot(a, wd_ref[...])


def _ffn_call(x, oa, ob, mod, g_ffn, w_out, w_up, conv_w, conv_b, w_down, ts, fc):
    bsz, s, d = x.shape
    dff = w_down.shape[0]
    nf = dff // fc
    na = oa.shape[2]
    wo = w_out.astype(BF16).reshape(2, na, d)
    wup = w_up.astype(BF16)
    wgv = jnp.concatenate([wup[:, :dff].reshape(d, nf, fc), wup[:, dff:].reshape(d, nf, fc)],
                          axis=2).transpose(1, 0, 2)
    cwg = conv_w[:, :dff].reshape(CONV_W, nf, fc).transpose(1, 0, 2)
    cwv = conv_w[:, dff:].reshape(CONV_W, nf, fc).transpose(1, 0, 2)
    cbg = conv_b[:dff].reshape(nf, 1, fc)
    cbv = conv_b[dff:].reshape(nf, 1, fc)
    wd = w_down.astype(BF16)

    def full(a):
        return pl.BlockSpec(a.shape, lambda b, i: (0,) * a.ndim, pipeline_mode=pl.Buffered(1))

    def tok(n):
        return pl.BlockSpec((1, ts, n), lambda b, i: (b, i, 0))

    consts = [wo, wgv, cwg, cwv, cbg, cbv, wd]
    return pl.pallas_call(
        functools.partial(_ffn_kernel, nf=nf),
        grid=(bsz, s // ts),
        in_specs=[tok(d), tok(na), tok(ob.shape[2]),
                  pl.BlockSpec((1, 6, d), lambda b, i: (b, 0, 0)), full(g_ffn)]
                 + [full(a) for a in consts],
        out_specs=tok(d),
        out_shape=jax.ShapeDtypeStruct((bsz, s, d), F32),
        scratch_shapes=[pltpu.VMEM((nf, 2, 8, fc), F32), pltpu.VMEM((ts, d), BF16),
                        pltpu.VMEM((ts, d), F32), pltpu.VMEM((2, ts + 8, fc), F32),
                        pltpu.VMEM((2, ts + 8, fc), F32), pltpu.VMEM((nf, ts, fc), BF16)],
        compiler_params=pltpu.CompilerParams(dimension_semantics=("arbitrary", "arbitrary"),
                                             vmem_limit_bytes=VMEM_LIMIT),
        name="outproj_ffn",
    )(x, oa, ob, mod, g_ffn, *consts)


def kernel(x, c, w_ada, b_ada, g_attn, w_in, q_norm_a, k_norm_a, q_norm_b, k_norm_b, lam_vecs,
           subln_a, w_out, g_ffn, w_up, conv_w, conv_b, w_down, rel_bias):
    bsz, s, d = x.shape
    t = min(ATT_T, s)
    ta = min(DIFF_T, s)
    ts = min(512, s)
    topk = min(TOPK_MAX, s // 4)
    scale = HEAD_DIM ** -0.5
    for l in range(w_ada.shape[0]):
        mod = _ada_call(c, w_ada[l], b_ada[l]).reshape(bsz, 6, d)
        gains = jnp.stack([q_norm_a[l], k_norm_a[l], q_norm_b[l], k_norm_b[l]])
        bias_a, bound = _bias_call(rel_bias, gains, ta, 0, A_HEADS)
        bias_b, _ = _bias_call(rel_bias, gains, t, A_HEADS, B_HEADS)
        gqa = (jnp.tile(q_norm_a[l], 2 * A_HEADS) * (scale * LOG2E))[None]
        gka = jnp.tile(k_norm_a[l], 2 * A_HEADS)[None]
        gqb = (jnp.tile(q_norm_b[l], B_HEADS) * (scale * LOG2E))[None]
        gkb = jnp.tile(k_norm_b[l], B_KV_HEADS)[None]
        aq, ak, av, bq, bkv, iq, ik, iw = _inproj_call(
            x, mod, g_attn[l][None], w_in[l], gqa, gka, gqb, gkb, ts)
        lam_init = 0.8 - 0.6 * math.exp(-0.3 * l)
        o_a = _diff_call(lam_vecs[l], subln_a[l][None], bound, aq, ak, av, bias_a, ta, lam_init)
        o_b = _dsa_call(bound, iq, iw, ik, bq, bkv, bias_b, t, topk)
        x = _ffn_call(x, o_a, o_b, mod, g_ffn[l][None], w_out[l], w_up[l], conv_w[l], conv_b[l],
                      w_down[l], ts, 256)
    return x
```

```python
import functools
import math

import jax
import jax.numpy as jnp
from jax import lax
from jax.experimental import pallas as pl
from jax.experimental.pallas import tpu as pltpu

F32 = jnp.float32
BF16 = jnp.bfloat16
I32 = jnp.int32
I16 = jnp.int16

HEAD_DIM = 64
A_HEADS = 4
A_VDIM = 2 * HEAD_DIM
B_HEADS = 8
B_KV_HEADS = 2
B_GROUP = B_HEADS // B_KV_HEADS
IDX_HEADS = 8
IDX_DIM = 64
TOPK_MAX = 256
N_BUCKETS = 32
MAX_DISTANCE = 128
CONV_W = 3
EPS = 1e-6

LANES = 128
ATT_T = 512
SEL_ROWS = 128
DIFF_T = 512
FAR_BLOCKS = 2
MASKED = -1e30
LOG2E = math.log2(math.e)
L_TINY = 2.0 ** -60
NEG_KEY = -2139095041
I16_MIN = -32768
VMEM_LIMIT = 56 * 1024 * 1024


def _dot(a, b):
    return jnp.dot(a, b, preferred_element_type=F32)


def _dot_nt(a, b):
    return lax.dot_general(a, b, (((1,), (1,)), ((), ())), preferred_element_type=F32)


def _split_bf16(a):
    hi = a.astype(BF16)
    lo = (a - hi.astype(F32)).astype(BF16)
    return hi, lo


def _ada_kernel(c_ref, w_ref, b_ref, o_ref):
    c = c_ref[...]
    a = c / (1.0 + jnp.exp(-c))
    a_hi, a_lo = _split_bf16(a)
    w_hi, w_lo = _split_bf16(w_ref[...])
    o_ref[...] = _dot(a_hi, w_hi) + _dot(a_hi, w_lo) + _dot(a_lo, w_hi) + b_ref[...]


def _ada_call(c, w, b):
    bsz, d = c.shape
    n = w.shape[1]
    tn = 1024
    return pl.pallas_call(
        _ada_kernel,
        grid=(n // tn,),
        in_specs=[pl.BlockSpec((bsz, d), lambda j: (0, 0)),
                  pl.BlockSpec((d, tn), lambda j: (0, j)),
                  pl.BlockSpec((1, tn), lambda j: (0, j))],
        out_specs=pl.BlockSpec((bsz, tn), lambda j: (0, j)),
        out_shape=jax.ShapeDtypeStruct((bsz, n), F32),
        compiler_params=pltpu.CompilerParams(dimension_semantics=("arbitrary",),
                                             vmem_limit_bytes=VMEM_LIMIT),
        name="ada_mod",
    )(c, w, b.reshape(1, n))


def _bias_kernel(tab_ref, tabv_ref, gains_ref, o_ref, bound_ref, *, t, h0):
    h = pl.program_id(0) + h0
    max_exact = N_BUCKETS // 2
    b = MAX_DISTANCE
    row = lax.broadcasted_iota(I32, (b, b), 0)
    col = lax.broadcasted_iota(I32, (b, b), 1)
    last = tab_ref[N_BUCKETS - 1, h]
    base = []
    for d in range(2):
        n = jnp.maximum(row - col + d * b, 0)
        nf = jnp.maximum(n, 1).astype(F32)
        large = max_exact + (jnp.log(nf / max_exact) / math.log(MAX_DISTANCE / max_exact)
                             * (N_BUCKETS - max_exact)).astype(I32)
        large = jnp.minimum(large, N_BUCKETS - 1)
        bucket = jnp.where(n < max_exact, n, large)
        acc = jnp.zeros((b, b), F32)
        for k in range(N_BUCKETS - 1):
            acc = jnp.where(bucket == k, (tab_ref[k, h] - last) * LOG2E, acc)
        base.append(acc)
    by_distance = {0: jnp.where(row >= col, base[0], MASKED), 1: base[1]}
    masked = jnp.full((b, b), MASKED, F32)
    zero = jnp.zeros((b, b), F32)
    nb = t // b
    for d in range(2):
        o_ref[0, d] = jnp.concatenate(
            [jnp.concatenate([masked if d * nb + a - c < 0 else by_distance.get(d * nb + a - c, zero)
                              for c in range(nb)], axis=1) for a in range(nb)], axis=0)
    tab = tabv_ref[...]
    rel = (tab - tab[N_BUCKETS - 1:N_BUCKETS, :]) * LOG2E
    hcol = lax.broadcasted_iota(I32, rel.shape, 1)
    g = jnp.max(jnp.abs(gains_ref[...]), axis=1, keepdims=True)
    for grp in range(2):
        in_grp = (hcol < A_HEADS) if grp == 0 else (hcol >= A_HEADS)
        bmax = jnp.max(jnp.where(in_grp, rel, 0.0), keepdims=True)
        qk = g[2 * grp:2 * grp + 1] * g[2 * grp + 1:2 * grp + 2] * (HEAD_DIM ** 0.5 * LOG2E)
        bound_ref[grp] = jnp.broadcast_to(qk + bmax, bound_ref.shape[1:])


def _bias_call(rel_bias, gains, t, h0, nh):
    return pl.pallas_call(
        functools.partial(_bias_kernel, t=t, h0=h0),
        grid=(nh,),
        in_specs=[pl.BlockSpec(memory_space=pltpu.SMEM),
                  pl.BlockSpec(rel_bias.shape, lambda h: (0, 0)),
                  pl.BlockSpec(gains.shape, lambda h: (0, 0))],
        out_specs=[pl.BlockSpec((1, 2, t, t), lambda h: (h, 0, 0, 0)),
                   pl.BlockSpec((2, 8, LANES), lambda h: (0, 0, 0))],
        out_shape=[jax.ShapeDtypeStruct((nh, 2, t, t), F32),
                   jax.ShapeDtypeStruct((2, 8, LANES), F32)],
        compiler_params=pltpu.CompilerParams(dimension_semantics=("arbitrary",)),
        name="bias_tiles",
    )(rel_bias, rel_bias, gains)


def _modulated_norm(x, g, scale, shift):
    y = x * lax.rsqrt(jnp.mean(x * x, axis=-1, keepdims=True) + EPS)
    return (y * g) * (1.0 + scale) + shift


def _head_norm(y, gain, bd):
    y2 = (y * y).astype(BF16)
    w = bd.shape[0]
    ss = jnp.concatenate([_dot(y2[:, c:c + w], bd) for c in range(0, y.shape[1], w)], axis=1)
    return y * lax.rsqrt(ss * (1.0 / HEAD_DIM) + EPS) * gain


def _inproj_kernel(x_ref, mod_ref, g_ref, waq_ref, wak_ref, wav_ref, wbq_ref, wbkv_ref, wiq_ref,
                   wik_ref, gqa_ref, gka_ref, gqb_ref, gkb_ref, bd_ref,
                   aq_ref, ak_ref, av_ref, bq_ref, bkv_ref, iq_ref, ik_ref, iw_ref):
    x = x_ref[0]
    h = _modulated_norm(x, g_ref[...], mod_ref[0, 1:2, :], mod_ref[0, 0:1, :]).astype(BF16)
    bd = bd_ref[...]
    aq_ref[0] = _head_norm(_dot(h, waq_ref[...]), gqa_ref[...], bd).astype(BF16)
    ak_ref[0] = _head_norm(_dot(h, wak_ref[...]), gka_ref[...], bd).astype(BF16)
    av_ref[0] = _dot(h, wav_ref[...]).astype(BF16)
    bq_ref[0] = _head_norm(_dot(h, wbq_ref[...]), gqb_ref[...], bd).astype(BF16)
    kv = _dot(h, wbkv_ref[...])
    nk = B_KV_HEADS * HEAD_DIM
    bk = _head_norm(kv[:, :nk], gkb_ref[...], bd[:nk, :nk])
    bkv_ref[0] = jnp.concatenate([bk, kv[:, nk:]], axis=1).astype(BF16)
    iq_ref[0] = _dot(h, wiq_ref[...]).astype(BF16)
    ikw = _dot(h, wik_ref[...])
    ik_ref[0] = ikw[:, :2 * IDX_DIM].astype(BF16)
    iw_ref[0] = ikw[:, 2 * IDX_DIM:2 * IDX_DIM + IDX_HEADS] * ((IDX_HEADS * IDX_DIM) ** -0.5)


def _inproj_call(x, mod, g_attn, w_in, gqa, gka, gqb, gkb, ts):
    bsz, s, d = x.shape
    na = A_HEADS * 2 * HEAD_DIM
    nb = B_HEADS * HEAD_DIM
    nkv = B_KV_HEADS * HEAD_DIM
    ni = IDX_HEADS * IDX_DIM
    o = 0
    parts = []
    for width in (na, na, A_HEADS * A_VDIM, nb, 2 * nkv, ni, IDX_DIM + IDX_HEADS):
        parts.append(w_in[:, o:o + width].astype(BF16))
        o += width
    wik = jnp.concatenate([parts[-1][:, :IDX_DIM], parts[-1]], axis=1)
    parts[-1] = jnp.pad(wik, ((0, 0), (0, 2 * LANES - wik.shape[1])))
    grp = jnp.arange(2 * LANES) // HEAD_DIM
    bd = (grp[:, None] == grp[None, :]).astype(BF16)

    def full(a):
        return pl.BlockSpec(a.shape, lambda b, i: (0,) * a.ndim)

    def tok(n):
        return pl.BlockSpec((1, ts, n), lambda b, i: (b, i, 0))

    consts = parts + [gqa, gka, gqb, gkb, bd]
    out_w = [(na, BF16), (na, BF16), (A_HEADS * A_VDIM, BF16), (nb, BF16), (2 * nkv, BF16),
             (ni, BF16), (2 * IDX_DIM, BF16), (IDX_HEADS, F32)]
    return pl.pallas_call(
        _inproj_kernel,
        grid=(bsz, s // ts),
        in_specs=[tok(d), pl.BlockSpec((1, 6, d), lambda b, i: (b, 0, 0)), full(g_attn)]
                 + [full(a) for a in consts],
        out_specs=[tok(n) for n, _ in out_w],
        out_shape=[jax.ShapeDtypeStruct((bsz, s, n), dt) for n, dt in out_w],
        compiler_params=pltpu.CompilerParams(dimension_semantics=("arbitrary", "arbitrary"),
                                             vmem_limit_bytes=VMEM_LIMIT),
        name="in_proj",
    )(x, mod, g_attn, *consts)


def _diff_kernel(lam_ref, g_ref, bound_ref, q_ref, k_ref, v_ref, bias_ref, o_ref, acc_ref, *,
                 t, lam_init):
    i = pl.program_id(2)
    q = q_ref[0]
    lane = lax.broadcasted_iota(I32, q.shape, 1)
    zero = jnp.zeros_like(q)
    qq = jnp.concatenate([jnp.where(lane < HEAD_DIM, q, zero),
                          jnp.where(lane >= HEAD_DIM, q, zero)], axis=0)

    def logits(start, width):
        k = k_ref[0, pl.ds(pl.multiple_of(start, t), width), :]
        return _dot_nt(qq, k)

    def v_ones(start, width):
        v = v_ref[0, pl.ds(pl.multiple_of(start, t), width), :]
        return jnp.concatenate([v, jnp.ones((width, LANES), BF16)], axis=1)

    def near_logits():
        b = jnp.concatenate([bias_ref[0, 1], bias_ref[0, 0]], axis=1)
        return logits((i - 1) * t, 2 * t) + jnp.concatenate([b, b], axis=0)

    def first_logits():
        b = bias_ref[0, 0]
        return logits(0, t) + jnp.concatenate([b, b], axis=0)

    def for_far_blocks(fn):
        nfar = jnp.maximum(i - 1, 0)
        nsup = nfar // FAR_BLOCKS

        def sup(j, c):
            fn(j * (FAR_BLOCKS * t), FAR_BLOCKS * t)
            return c

        def single(j, c):
            fn(j * t, t)
            return c

        lax.fori_loop(0, nsup, sup, 0)
        lax.fori_loop(nsup * FAR_BLOCKS, nfar, single, 0)

    def accumulate(m):
        acc_ref[...] = jnp.zeros(acc_ref.shape, F32)

        def far(start, width):
            p = jnp.exp2(logits(start, width) - m).astype(BF16)
            acc_ref[...] += _dot(p, v_ones(start, width))

        for_far_blocks(far)

        @pl.when(i > 0)
        def _():
            p = jnp.exp2(near_logits() - m).astype(BF16)
            acc_ref[...] += _dot(p, v_ones((i - 1) * t, 2 * t))

        @pl.when(i == 0)
        def _():
            p = jnp.exp2(first_logits() - m).astype(BF16)
            acc_ref[...] += _dot(p, v_ones(0, t))

    accumulate(bound_ref[0, 0:1, 0:1])

    @pl.when(jnp.min(acc_ref[:, LANES:LANES + 1]) < L_TINY)
    def _():
        acc_ref[...] = jnp.full(acc_ref.shape, MASKED, F32)

        def fold_max(s):
            w = acc_ref.shape[1]
            for c in range(s.shape[1] // w):
                acc_ref[...] = jnp.maximum(acc_ref[...], s[:, c * w:(c + 1) * w])

        for_far_blocks(lambda start, width: fold_max(logits(start, width)))

        @pl.when(i > 0)
        def _():
            fold_max(near_logits())

        @pl.when(i == 0)
        def _():
            fold_max(first_logits())

        accumulate(jnp.max(acc_ref[...], axis=1, keepdims=True))

    lv = lam_ref[...]
    lam = (jnp.exp(jnp.sum(lv[0:1] * lv[1:2], axis=1, keepdims=True))
           - jnp.exp(jnp.sum(lv[2:3] * lv[3:4], axis=1, keepdims=True)) + lam_init)
    acc = acc_ref[...]
    o = acc[:, :LANES] / acc[:, LANES:]
    o = o[:t] - lam * o[t:]
    o = o * lax.rsqrt(jnp.mean(o * o, axis=-1, keepdims=True) + EPS)
    o_ref[0] = (o * g_ref[...] * (1.0 - lam_init)).astype(BF16)


def _diff_call(lam_vecs, subln, bound, aq, ak, av, bias, t, lam_init):
    bsz, s, _ = aq.shape
    assert t % (2 * LANES) == 0
    return pl.pallas_call(
        functools.partial(_diff_kernel, t=t, lam_init=lam_init),
        grid=(bsz, A_HEADS, s // t),
        in_specs=[pl.BlockSpec(lam_vecs.shape, lambda b, h, i: (0, 0)),
                  pl.BlockSpec(subln.shape, lambda b, h, i: (0, 0)),
                  pl.BlockSpec((1, 8, LANES), lambda b, h, i: (0, 0, 0)),
                  pl.BlockSpec((1, t, A_VDIM), lambda b, h, i: (b, i, h)),
                  pl.BlockSpec((1, s, A_VDIM), lambda b, h, i: (b, 0, h)),
                  pl.BlockSpec((1, s, A_VDIM), lambda b, h, i: (b, 0, h)),
                  pl.BlockSpec((1, 2, t, t), lambda b, h, i: (h, 0, 0, 0))],
        out_specs=pl.BlockSpec((1, t, A_VDIM), lambda b, h, i: (b, i, h)),
        out_shape=jax.ShapeDtypeStruct((bsz, s, A_HEADS * A_VDIM), BF16),
        scratch_shapes=[pltpu.VMEM((2 * t, 2 * LANES), F32)],
        compiler_params=pltpu.CompilerParams(
            dimension_semantics=("arbitrary", "arbitrary", "arbitrary"),
            vmem_limit_bytes=VMEM_LIMIT),
        name="diff_attn",
    )(lam_vecs, subln, bound, aq, ak, av, bias)


def _sortable_key(x):
    b = pltpu.bitcast(x, I32)
    return b ^ ((b >> 31) & 0x7FFFFFFF)


def _dsa_kernel(bound_ref, iq_ref, iw_ref, ik_ref, q_ref, kv_ref, bias_ref, o_ref,
                hk_ref, lk_ref, iqz_ref, wb_ref, thr_ref, th_ref, tl_ref, tie_ref, acc_ref, *,
                t, topk):
    i = pl.program_id(1)
    row = lax.broadcasted_iota(I32, (t, t), 0)
    col = lax.broadcasted_iota(I32, (t, t), 1)

    iq = iq_ref[0]
    iw = iw_ref[0]
    lane = lax.broadcasted_iota(I32, (t, LANES), 1)
    for h in range(IDX_HEADS):
        pair = iq[:, (h // 2) * LANES:(h // 2 + 1) * LANES]
        mine = (lane < IDX_DIM) if h % 2 == 0 else (lane >= IDX_DIM)
        iqz_ref[h] = jnp.where(mine, pair, jnp.zeros_like(pair))
        wb_ref[h] = jnp.broadcast_to(iw[:, h:h + 1], (t, LANES))

    def score_block(j, causal):
        kc = ik_ref[0, pl.ds(pl.multiple_of(j * t, t), t), :]
        acc = jnp.zeros((t, t), F32)
        for h in range(IDX_HEADS):
            wb = wb_ref[h]
            wb = jnp.concatenate([wb] * (t // LANES), axis=1)
            acc = acc + wb * jnp.maximum(_dot_nt(iqz_ref[h], kc), 0.0)
        if causal:
            acc = jnp.where(row >= col, acc, -jnp.inf)
        key = _sortable_key(acc)
        hk_ref[j] = (key >> 16).astype(I16)
        lk_ref[j] = ((key & 0xFFFF) + I16_MIN).astype(I16)

    def score_body(j, carry):
        score_block(j, False)
        return carry

    lax.fori_loop(0, i, score_body, 0)
    score_block(i, True)

    ngroups = t // SEL_ROWS
    lane_groups = t // LANES
    one16 = jnp.int16(1)
    zero16 = jnp.int16(0)

    def group_count(pred, use, r, nfull):
        rs = pl.ds(r * SEL_ROWS, SEL_ROWS)

        def add_block(j, cnt, groups):
            h = hk_ref[j, rs, :] if "h" in use else None
            lo = lk_ref[j, rs, :] if "l" in use else None
            for w in range(groups):
                sl = slice(w * LANES, (w + 1) * LANES)
                m = pred(None if h is None else h[:, sl], None if lo is None else lo[:, sl])
                cnt = cnt + jnp.where(m, one16, zero16)
            return cnt

        cnt = jnp.zeros((SEL_ROWS, LANES), I16)
        if isinstance(nfull, int):
            for j in range(nfull):
                cnt = add_block(j, cnt, lane_groups)
        else:
            cnt = lax.fori_loop(0, nfull, lambda j, cnt: add_block(j, cnt, lane_groups), cnt)
        return add_block(nfull, cnt, -(-(r + 1) * SEL_ROWS // LANES))

    def count(make_pred, use, nfull):
        outs = []
        for r in range(ngroups):
            cnt = group_count(make_pred(r), use, r, nfull).astype(F32)
            outs.append(jnp.broadcast_to(jnp.sum(cnt, axis=1, keepdims=True), cnt.shape))
        return jnp.concatenate(outs, axis=0)

    def rows16(v32, r):
        return v32[r * SEL_ROWS:(r + 1) * SEL_ROWS].astype(I16)

    def bisect16(use, need, nfull):
        def body(it, thr):
            cand = thr + jnp.left_shift(jnp.int32(1), 15 - it)
            if use == "h":
                mk = lambda r: (lambda h, lo, c=rows16(cand, r): h >= c)
            else:
                mk = lambda r: (lambda h, lo, c=rows16(cand, r): lo >= c)
            return jnp.where(count(mk, use, nfull) >= need, cand, thr)

        thr_ref[...] = jnp.full(thr_ref.shape, I16_MIN, I32)
        return lax.fori_loop(0, 16, body, thr_ref[:, :LANES])

    def wide16(v32):
        return jnp.concatenate([v32.astype(I16)] * lane_groups, axis=1)

    def select(nfull):
        t_hi = bisect16("h", float(topk), nfull)
        above = count(lambda r: (lambda h, lo, c=rows16(t_hi, r): h > c), "h", nfull)
        th = wide16(t_hi)
        for j in range(nfull + 1):
            lk_ref[j] = jnp.where(hk_ref[j] == th, lk_ref[j], jnp.int16(I16_MIN))
        t_lo = bisect16("l", float(topk) - above, nfull)
        in_bucket = count(
            lambda r: (lambda h, lo, ch=rows16(t_hi, r), cl=rows16(t_lo, r): (h == ch) & (lo >= cl)),
            "hl", nfull)
        thr = t_hi * 65536 + (t_lo - I16_MIN)
        masked = thr <= NEG_KEY
        tie_ref[...] = jnp.where((above + in_bucket > float(topk)) & ~masked, 1.0, 0.0)
        th_ref[...] = wide16(jnp.where(masked, (NEG_KEY + 1) >> 16, t_hi))
        tl_ref[...] = wide16(jnp.where(masked, I16_MIN, t_lo))

    for n in range(hk_ref.shape[0]):
        pl.when(i == n)(functools.partial(select, n))

    @pl.when(jnp.max(tie_ref[...]) > 0.0)
    def _():
        tri = (row <= col).astype(BF16)
        th = th_ref[...]
        tl = tl_ref[...]
        th32 = th[:, :LANES].astype(I32)
        tl32 = tl[:, :LANES].astype(I32)
        greater = count(
            lambda r: (lambda h, lo, ch=rows16(th32, r), cl=rows16(tl32, r):
                       (h > ch) | ((h == ch) & (lo > cl))), "hl", i)
        need = float(topk) - greater
        need2 = jnp.concatenate([need] * lane_groups, axis=1)
        excess2 = jnp.concatenate([tie_ref[...] > 0.0] * lane_groups, axis=1)

        def body(j, seen):
            h = hk_ref[j]
            eqf = jnp.where((h == th) & (lk_ref[j] == tl), one16, zero16).astype(F32)
            rank = seen + _dot(eqf.astype(BF16), tri)
            drop = (eqf > 0.5) & (rank > need2) & excess2
            drop16 = jnp.where(drop, 1, 0).astype(I16)
            hk_ref[j] = jnp.where(drop16 > zero16, jnp.int16(I16_MIN), h)
            return seen + jnp.sum(eqf, axis=1, keepdims=True)

        lax.fori_loop(0, i + 1, body, jnp.zeros((t, 1), F32))

    q = q_ref[0]
    qs = []
    for g in range(B_KV_HEADS):
        parts = []
        for r in range(B_GROUP):
            hq = q[:, (g * B_GROUP + r) * HEAD_DIM:(g * B_GROUP + r + 1) * HEAD_DIM]
            z = jnp.zeros_like(hq)
            parts.append(jnp.concatenate([hq, z] if g == 0 else [z, hq], axis=1))
        qs.append(jnp.concatenate(parts, axis=0))

    def logits(j, g, near):
        k = kv_ref[0, pl.ds(pl.multiple_of(j * t, t), t), :LANES]
        h = hk_ref[j]
        th = th_ref[...]
        picked = (h > th) | ((h == th) & (lk_ref[j] >= tl_ref[...]))
        madd = jnp.where(jnp.where(picked, one16, zero16).astype(F32) > 0.5, 0.0, MASKED)
        s = _dot_nt(qs[g], k).reshape(B_GROUP, t, t)
        if near is None:
            s = s + madd[None]
        else:
            s = s + (bias_ref[g * B_GROUP:(g + 1) * B_GROUP, near] + madd[None])
        return s.reshape(B_GROUP * t, t)

    def for_blocks(fn):
        def far(j, c):
            fn(j, None)
            return c

        lax.fori_loop(0, i - 1, far, 0)

        @pl.when(i > 0)
        def _():
            fn(i - 1, 1)

        fn(i, 0)

    def accumulate(ms):
        acc_ref[...] = jnp.zeros(acc_ref.shape, F32)

        def step(j, near):
            v = kv_ref[0, pl.ds(pl.multiple_of(j * t, t), t), LANES:]
            vx = jnp.concatenate([v, jnp.ones((t, LANES), BF16)], axis=1)
            for g in range(B_KV_HEADS):
                p = jnp.exp2(logits(j, g, near) - ms[g]).astype(BF16)
                acc_ref[g] += _dot(p, vx)

        for_blocks(step)

    m0 = bound_ref[0, 0:1, 0:1]
    accumulate([m0, m0])

    @pl.when(jnp.min(acc_ref[:, :, LANES:LANES + 1]) < L_TINY)
    def _():
        acc_ref[...] = jnp.full(acc_ref.shape, MASKED, F32)

        def step(j, near):
            w = acc_ref.shape[2]
            for g in range(B_KV_HEADS):
                s = logits(j, g, near)
                for c in range(t // w):
                    acc_ref[g] = jnp.maximum(acc_ref[g], s[:, c * w:(c + 1) * w])

        for_blocks(step)
        accumulate([jnp.max(acc_ref[g], axis=1, keepdims=True) for g in range(B_KV_HEADS)])

    outs = []
    for g in range(B_KV_HEADS):
        a = acc_ref[g]
        o = a[:, g * HEAD_DIM:(g + 1) * HEAD_DIM] / a[:, LANES:LANES + HEAD_DIM]
        for r in range(B_GROUP):
            outs.append(o[r * t:(r + 1) * t])
    o_ref[0] = jnp.concatenate(outs, axis=1).astype(BF16)


def _dsa_call(bound, iq, iw, ik, bq, bkv, bias, t, topk):
    bsz, s, _ = bq.shape
    nq = s // t
    assert t % SEL_ROWS == 0 and t % (2 * LANES) == 0
    return pl.pallas_call(
        functools.partial(_dsa_kernel, t=t, topk=topk),
        grid=(bsz, nq),
        in_specs=[pl.BlockSpec((1, 8, LANES), lambda b, i: (1, 0, 0)),
                  pl.BlockSpec((1, t, iq.shape[2]), lambda b, i: (b, i, 0)),
                  pl.BlockSpec((1, t, iw.shape[2]), lambda b, i: (b, i, 0)),
                  pl.BlockSpec((1, s, ik.shape[2]), lambda b, i: (b, 0, 0)),
                  pl.BlockSpec((1, t, bq.shape[2]), lambda b, i: (b, i, 0)),
                  pl.BlockSpec((1, s, bkv.shape[2]), lambda b, i: (b, 0, 0)),
                  pl.BlockSpec(bias.shape, lambda b, i: (0, 0, 0, 0),
                               pipeline_mode=pl.Buffered(1))],
        out_specs=pl.BlockSpec((1, t, B_HEADS * HEAD_DIM), lambda b, i: (b, i, 0)),
        out_shape=jax.ShapeDtypeStruct((bsz, s, B_HEADS * HEAD_DIM), BF16),
        scratch_shapes=[pltpu.VMEM((nq, t, t), I16), pltpu.VMEM((nq, t, t), I16),
                        pltpu.VMEM((IDX_HEADS, t, LANES), BF16),
                        pltpu.VMEM((IDX_HEADS, t, LANES), F32),
                        pltpu.VMEM((t, t), I32), pltpu.VMEM((t, t), I16), pltpu.VMEM((t, t), I16),
                        pltpu.VMEM((t, LANES), F32),
                        pltpu.VMEM((B_KV_HEADS, B_GROUP * t, 2 * LANES), F32)],
        compiler_params=pltpu.CompilerParams(dimension_semantics=("arbitrary", "arbitrary"),
                                             vmem_limit_bytes=VMEM_LIMIT),
        name="dsa_attn",
    )(bound, iq, iw, ik, bq, bkv, bias)


def _ffn_kernel(x_ref, oa_ref, ob_ref, mod_ref, g_ref, wo_ref, wgv_ref, cwg_ref, cwv_ref,
                cbg_ref, cbv_ref, wd_ref, o_ref, carry_ref, h_ref, y_ref, ua_ref, ub_ref, a_ref, *, nf):
    i = pl.program_id(1)
    x = x_ref[0]
    ts = x.shape[0]
    mixed = _dot(oa_ref[0], wo_ref[0]) + _dot(ob_ref[0], wo_ref[1])
    x1 = x + mod_ref[0, 2:3, :] * mixed
    h_ref[...] = _modulated_norm(x1, g_ref[...], mod_ref[0, 4:5, :], mod_ref[0, 3:4, :]).astype(BF16)
    y_ref[...] = x1

    @pl.when(i == 0)
    def _():
        carry_ref[...] = jnp.zeros(carry_ref.shape, F32)

    def up(f, buf_ref):
        u = _dot(h_ref[...], wgv_ref[f])
        fc = u.shape[1] // 2
        buf_ref[0, 8:8 + ts] = u[:, :fc]
        buf_ref[1, 8:8 + ts] = u[:, fc:]

    def conv(buf_ref, part, prev8, cw, cb):
        buf_ref[part, 0:8] = prev8
        return (cb + cw[2:3] * buf_ref[part, 8:8 + ts] + cw[1:2] * buf_ref[part, 7:7 + ts]
                + cw[0:1] * buf_ref[part, 6:6 + ts])

    def gate(f, buf_ref):
        yg = conv(buf_ref, 0, carry_ref[f, 0], cwg_ref[f], cbg_ref[f])
        yv = conv(buf_ref, 1, carry_ref[f, 1], cwv_ref[f], cbv_ref[f])
        carry_ref[f, 0] = buf_ref[0, ts:ts + 8]
        carry_ref[f, 1] = buf_ref[1, ts:ts + 8]
        a_ref[f] = ((yg / (1.0 + jnp.exp(-yg))) * yv).astype(BF16)

    def stage(f, cur_ref, nxt_ref):
        gate(f, cur_ref)
        up(f + 1, nxt_ref)

    up(0, ua_ref)

    assert nf % 2 == 1
    for k in range((nf - 1) // 2):
        stage(2 * k, ua_ref, ub_ref)
        stage(2 * k + 1, ub_ref, ua_ref)
    gate(nf - 1, ua_ref)
    a = jnp.concatenate([a_ref[f] for f in range(nf)], axis=1)
    o_ref[0] = y_ref[...] + mod_ref[0, 5:6, :] * _dot(a, wd_ref[...])


def _ffn_call(x, oa, ob, mod, g_ffn, w_out, w_up, conv_w, conv_b, w_down, ts, fc):
    bsz, s, d = x.shape
    dff = w_down.shape[0]
    nf = dff // fc
    na = oa.shape[2]
    wo = w_out.astype(BF16).reshape(2, na, d)
    wup = w_up.astype(BF16)
    wgv = jnp.concatenate([wup[:, :dff].reshape(d, nf, fc), wup[:, dff:].reshape(d, nf, fc)],
                          axis=2).transpose(1, 0, 2)
    cwg = conv_w[:, :dff].reshape(CONV_W, nf, fc).transpose(1, 0, 2)
    cwv = conv_w[:, dff:].reshape(CONV_W, nf, fc).transpose(1, 0, 2)
    cbg = conv_b[:dff].reshape(nf, 1, fc)
    cbv = conv_b[dff:].reshape(nf, 1, fc)
    wd = w_down.astype(BF16)

    def full(a):
        return pl.BlockSpec(a.shape, lambda b, i: (0,) * a.ndim, pipeline_mode=pl.Buffered(1))

    def tok(n):
        return pl.BlockSpec((1, ts, n), lambda b, i: (b, i, 0))

    consts = [wo, wgv, cwg, cwv, cbg, cbv, wd]
    return pl.pallas_call(
        functools.partial(_ffn_kernel, nf=nf),
        grid=(bsz, s // ts),
        in_specs=[tok(d), tok(na), tok(ob.shape[2]),
                  pl.BlockSpec((1, 6, d), lambda b, i: (b, 0, 0)), full(g_ffn)]
                 + [full(a) for a in consts],
        out_specs=tok(d),
        out_shape=jax.ShapeDtypeStruct((bsz, s, d), F32),
        scratch_shapes=[pltpu.VMEM((nf, 2, 8, fc), F32), pltpu.VMEM((ts, d), BF16),
                        pltpu.VMEM((ts, d), F32), pltpu.VMEM((2, ts + 8, fc), F32),
                        pltpu.VMEM((2, ts + 8, fc), F32), pltpu.VMEM((nf, ts, fc), BF16)],
        compiler_params=pltpu.CompilerParams(dimension_semantics=("arbitrary", "arbitrary"),
                                             vmem_limit_bytes=VMEM_LIMIT),
        name="outproj_ffn",
    )(x, oa, ob, mod, g_ffn, *consts)


def kernel(x, c, w_ada, b_ada, g_attn, w_in, q_norm_a, k_norm_a, q_norm_b, k_norm_b, lam_vecs,
           subln_a, w_out, g_ffn, w_up, conv_w, conv_b, w_down, rel_bias):
    bsz, s, d = x.shape
    t = min(ATT_T, s)
    ta = min(DIFF_T, s)
    ts = min(512, s)
    topk = min(TOPK_MAX, s // 4)
    scale = HEAD_DIM ** -0.5
    for l in range(w_ada.shape[0]):
        mod = _ada_call(c, w_ada[l], b_ada[l]).reshape(bsz, 6, d)
        gains = jnp.stack([q_norm_a[l], k_norm_a[l], q_norm_b[l], k_norm_b[l]])
        bias_a, bound = _bias_call(rel_bias, gains, ta, 0, A_HEADS)
        bias_b, _ = _bias_call(rel_bias, gains, t, A_HEADS, B_HEADS)
        gqa = (jnp.tile(q_norm_a[l], 2 * A_HEADS) * (scale * LOG2E))[None]
        gka = jnp.tile(k_norm_a[l], 2 * A_HEADS)[None]
        gqb = (jnp.tile(q_norm_b[l], B_HEADS) * (scale * LOG2E))[None]
        gkb = jnp.tile(k_norm_b[l], B_KV_HEADS)[None]
        aq, ak, av, bq, bkv, iq, ik, iw = _inproj_call(
            x, mod, g_attn[l][None], w_in[l], gqa, gka, gqb, gkb, ts)
        lam_init = 0.8 - 0.6 * math.exp(-0.3 * l)
        o_a = _diff_call(lam_vecs[l], subln_a[l][None], bound, aq, ak, av, bias_a, ta, lam_init)
        o_b = _dsa_call(bound, iq, iw, ik, bq, bkv, bias_b, t, topk)
        x = _ffn_call(x, o_a, o_b, mod, g_ffn[l][None], w_out[l], w_up[l], conv_w[l], conv_b[l],
                      w_down[l], ts, 256)
    return x
```

```python
import functools
import math

import jax
import jax.numpy as jnp
from jax import lax
from jax.experimental import pallas as pl
from jax.experimental.pallas import tpu as pltpu

F32 = jnp.float32
BF16 = jnp.bfloat16
I32 = jnp.int32

HEAD_DIM = 64
A_HEADS = 4
A_VDIM = 2 * HEAD_DIM
B_HEADS = 8
B_KV_HEADS = 2
B_GROUP = B_HEADS // B_KV_HEADS
IDX_HEADS = 8
IDX_DIM = 64
TOPK_MAX = 256
N_BUCKETS = 32
MAX_DISTANCE = 128
CONV_W = 3
EPS = 1e-6

LANES = 128
ATT_T = 512
SEL_ROWS = 128
DIFF_T = 512
FAR_BLOCKS = 2
MASKED = -1e30
LOG2E = math.log2(math.e)
L_TINY = 2.0 ** -60
NEG_KEY = -2139095041
INT_MIN = -2147483648
VMEM_LIMIT = 56 * 1024 * 1024


def _dot(a, b):
    return jnp.dot(a, b, preferred_element_type=F32)


def _dot_nt(a, b):
    return lax.dot_general(a, b, (((1,), (1,)), ((), ())), preferred_element_type=F32)


def _split_bf16(a):
    hi = a.astype(BF16)
    lo = (a - hi.astype(F32)).astype(BF16)
    return hi, lo


def _ada_kernel(c_ref, w_ref, b_ref, o_ref):
    c = c_ref[...]
    a = c / (1.0 + jnp.exp(-c))
    a_hi, a_lo = _split_bf16(a)
    w_hi, w_lo = _split_bf16(w_ref[...])
    o_ref[...] = _dot(a_hi, w_hi) + _dot(a_hi, w_lo) + _dot(a_lo, w_hi) + b_ref[...]


def _ada_call(c, w, b):
    bsz, d = c.shape
    n = w.shape[1]
    tn = 1024
    return pl.pallas_call(
        _ada_kernel,
        grid=(n // tn,),
        in_specs=[pl.BlockSpec((bsz, d), lambda j: (0, 0)),
                  pl.BlockSpec((d, tn), lambda j: (0, j)),
                  pl.BlockSpec((1, tn), lambda j: (0, j))],
        out_specs=pl.BlockSpec((bsz, tn), lambda j: (0, j)),
        out_shape=jax.ShapeDtypeStruct((bsz, n), F32),
        compiler_params=pltpu.CompilerParams(dimension_semantics=("arbitrary",),
                                             vmem_limit_bytes=VMEM_LIMIT),
        name="ada_mod",
    )(c, w, b.reshape(1, n))


def _bias_kernel(tab_ref, tabv_ref, gains_ref, o_ref, bound_ref, *, t, h0):
    h = pl.program_id(0) + h0
    max_exact = N_BUCKETS // 2
    b = MAX_DISTANCE
    row = lax.broadcasted_iota(I32, (b, b), 0)
    col = lax.broadcasted_iota(I32, (b, b), 1)
    last = tab_ref[N_BUCKETS - 1, h]
    base = []
    for d in range(2):
        n = jnp.maximum(row - col + d * b, 0)
        nf = jnp.maximum(n, 1).astype(F32)
        large = max_exact + (jnp.log(nf / max_exact) / math.log(MAX_DISTANCE / max_exact)
                             * (N_BUCKETS - max_exact)).astype(I32)
        large = jnp.minimum(large, N_BUCKETS - 1)
        bucket = jnp.where(n < max_exact, n, large)
        acc = jnp.zeros((b, b), F32)
        for k in range(N_BUCKETS - 1):
            acc = jnp.where(bucket == k, (tab_ref[k, h] - last) * LOG2E, acc)
        base.append(acc)
    by_distance = {0: jnp.where(row >= col, base[0], MASKED), 1: base[1]}
    masked = jnp.full((b, b), MASKED, F32)
    zero = jnp.zeros((b, b), F32)
    nb = t // b
    for d in range(2):
        o_ref[0, d] = jnp.concatenate(
            [jnp.concatenate([masked if d * nb + a - c < 0 else by_distance.get(d * nb + a - c, zero)
                              for c in range(nb)], axis=1) for a in range(nb)], axis=0)
    tab = tabv_ref[...]
    rel = (tab - tab[N_BUCKETS - 1:N_BUCKETS, :]) * LOG2E
    hcol = lax.broadcasted_iota(I32, rel.shape, 1)
    g = jnp.max(jnp.abs(gains_ref[...]), axis=1, keepdims=True)
    for grp in range(2):
        in_grp = (hcol < A_HEADS) if grp == 0 else (hcol >= A_HEADS)
        bmax = jnp.max(jnp.where(in_grp, rel, 0.0), keepdims=True)
        qk = g[2 * grp:2 * grp + 1] * g[2 * grp + 1:2 * grp + 2] * (HEAD_DIM ** 0.5 * LOG2E)
        bound_ref[grp] = jnp.broadcast_to(qk + bmax, bound_ref.shape[1:])


def _bias_call(rel_bias, gains, t, h0, nh):
    return pl.pallas_call(
        functools.partial(_bias_kernel, t=t, h0=h0),
        grid=(nh,),
        in_specs=[pl.BlockSpec(memory_space=pltpu.SMEM),
                  pl.BlockSpec(rel_bias.shape, lambda h: (0, 0)),
                  pl.BlockSpec(gains.shape, lambda h: (0, 0))],
        out_specs=[pl.BlockSpec((1, 2, t, t), lambda h: (h, 0, 0, 0)),
                   pl.BlockSpec((2, 8, LANES), lambda h: (0, 0, 0))],
        out_shape=[jax.ShapeDtypeStruct((nh, 2, t, t), F32),
                   jax.ShapeDtypeStruct((2, 8, LANES), F32)],
        compiler_params=pltpu.CompilerParams(dimension_semantics=("arbitrary",)),
        name="bias_tiles",
    )(rel_bias, rel_bias, gains)


def _modulated_norm(x, g, scale, shift):
    y = x * lax.rsqrt(jnp.mean(x * x, axis=-1, keepdims=True) + EPS)
    return (y * g) * (1.0 + scale) + shift


def _head_norm(y, gain, bd):
    y2 = (y * y).astype(BF16)
    w = bd.shape[0]
    ss = jnp.concatenate([_dot(y2[:, c:c + w], bd) for c in range(0, y.shape[1], w)], axis=1)
    return y * lax.rsqrt(ss * (1.0 / HEAD_DIM) + EPS) * gain


def _inproj_kernel(x_ref, mod_ref, g_ref, waq_ref, wak_ref, wav_ref, wbq_ref, wbkv_ref, wiq_ref,
                   wik_ref, gqa_ref, gka_ref, gqb_ref, gkb_ref, bd_ref,
                   aq_ref, ak_ref, av_ref, bq_ref, bkv_ref, iq_ref, ik_ref, iw_ref):
    x = x_ref[0]
    h = _modulated_norm(x, g_ref[...], mod_ref[0, 1:2, :], mod_ref[0, 0:1, :]).astype(BF16)
    bd = bd_ref[...]
    aq_ref[0] = _head_norm(_dot(h, waq_ref[...]), gqa_ref[...], bd).astype(BF16)
    ak_ref[0] = _head_norm(_dot(h, wak_ref[...]), gka_ref[...], bd).astype(BF16)
    av_ref[0] = _dot(h, wav_ref[...]).astype(BF16)
    bq_ref[0] = _head_norm(_dot(h, wbq_ref[...]), gqb_ref[...], bd).astype(BF16)
    kv = _dot(h, wbkv_ref[...])
    nk = B_KV_HEADS * HEAD_DIM
    bk = _head_norm(kv[:, :nk], gkb_ref[...], bd[:nk, :nk])
    bkv_ref[0] = jnp.concatenate([bk, kv[:, nk:]], axis=1).astype(BF16)
    iq_ref[0] = _dot(h, wiq_ref[...]).astype(BF16)
    ikw = _dot(h, wik_ref[...])
    ik_ref[0] = ikw[:, :2 * IDX_DIM].astype(BF16)
    iw_ref[0] = ikw[:, 2 * IDX_DIM:2 * IDX_DIM + IDX_HEADS] * ((IDX_HEADS * IDX_DIM) ** -0.5)


def _inproj_call(x, mod, g_attn, w_in, gqa, gka, gqb, gkb, ts):
    bsz, s, d = x.shape
    na = A_HEADS * 2 * HEAD_DIM
    nb = B_HEADS * HEAD_DIM
    nkv = B_KV_HEADS * HEAD_DIM
    ni = IDX_HEADS * IDX_DIM
    o = 0
    parts = []
    for width in (na, na, A_HEADS * A_VDIM, nb, 2 * nkv, ni, IDX_DIM + IDX_HEADS):
        parts.append(w_in[:, o:o + width].astype(BF16))
        o += width
    wik = jnp.concatenate([parts[-1][:, :IDX_DIM], parts[-1]], axis=1)
    parts[-1] = jnp.pad(wik, ((0, 0), (0, 2 * LANES - wik.shape[1])))
    grp = jnp.arange(2 * LANES) // HEAD_DIM
    bd = (grp[:, None] == grp[None, :]).astype(BF16)

    def full(a):
        return pl.BlockSpec(a.shape, lambda b, i: (0,) * a.ndim)

    def tok(n):
        return pl.BlockSpec((1, ts, n), lambda b, i: (b, i, 0))

    consts = parts + [gqa, gka, gqb, gkb, bd]
    out_w = [(na, BF16), (na, BF16), (A_HEADS * A_VDIM, BF16), (nb, BF16), (2 * nkv, BF16),
             (ni, BF16), (2 * IDX_DIM, BF16), (IDX_HEADS, F32)]
    return pl.pallas_call(
        _inproj_kernel,
        grid=(bsz, s // ts),
        in_specs=[tok(d), pl.BlockSpec((1, 6, d), lambda b, i: (b, 0, 0)), full(g_attn)]
                 + [full(a) for a in consts],
        out_specs=[tok(n) for n, _ in out_w],
        out_shape=[jax.ShapeDtypeStruct((bsz, s, n), dt) for n, dt in out_w],
        compiler_params=pltpu.CompilerParams(dimension_semantics=("arbitrary", "arbitrary"),
                                             vmem_limit_bytes=VMEM_LIMIT),
        name="in_proj",
    )(x, mod, g_attn, *consts)


def _diff_kernel(lam_ref, g_ref, bound_ref, q_ref, k_ref, v_ref, bias_ref, o_ref, acc_ref, *,
                 t, lam_init):
    i = pl.program_id(2)
    q = q_ref[0]
    lane = lax.broadcasted_iota(I32, q.shape, 1)
    zero = jnp.zeros_like(q)
    qq = jnp.concatenate([jnp.where(lane < HEAD_DIM, q, zero),
                          jnp.where(lane >= HEAD_DIM, q, zero)], axis=0)

    def logits(start, width):
        k = k_ref[0, pl.ds(pl.multiple_of(start, t), width), :]
        return _dot_nt(qq, k)

    def v_ones(start, width):
        v = v_ref[0, pl.ds(pl.multiple_of(start, t), width), :]
        return jnp.concatenate([v, jnp.ones((width, LANES), BF16)], axis=1)

    def near_logits():
        b = jnp.concatenate([bias_ref[0, 1], bias_ref[0, 0]], axis=1)
        return logits((i - 1) * t, 2 * t) + jnp.concatenate([b, b], axis=0)

    def first_logits():
        b = bias_ref[0, 0]
        return logits(0, t) + jnp.concatenate([b, b], axis=0)

    def for_far_blocks(fn):
        nfar = jnp.maximum(i - 1, 0)
        nsup = nfar // FAR_BLOCKS

        def sup(j, c):
            fn(j * (FAR_BLOCKS * t), FAR_BLOCKS * t)
            return c

        def single(j, c):
            fn(j * t, t)
            return c

        lax.fori_loop(0, nsup, sup, 0)
        lax.fori_loop(nsup * FAR_BLOCKS, nfar, single, 0)

    def accumulate(m):
        acc_ref[...] = jnp.zeros(acc_ref.shape, F32)

        def far(start, width):
            p = jnp.exp2(logits(start, width) - m).astype(BF16)
            acc_ref[...] += _dot(p, v_ones(start, width))

        for_far_blocks(far)

        @pl.when(i > 0)
        def _():
            p = jnp.exp2(near_logits() - m).astype(BF16)
            acc_ref[...] += _dot(p, v_ones((i - 1) * t, 2 * t))

        @pl.when(i == 0)
        def _():
            p = jnp.exp2(first_logits() - m).astype(BF16)
            acc_ref[...] += _dot(p, v_ones(0, t))

    accumulate(bound_ref[0, 0:1, 0:1])

    @pl.when(jnp.min(acc_ref[:, LANES:LANES + 1]) < L_TINY)
    def _():
        acc_ref[...] = jnp.full(acc_ref.shape, MASKED, F32)

        def fold_max(s):
            w = acc_ref.shape[1]
            for c in range(s.shape[1] // w):
                acc_ref[...] = jnp.maximum(acc_ref[...], s[:, c * w:(c + 1) * w])

        for_far_blocks(lambda start, width: fold_max(logits(start, width)))

        @pl.when(i > 0)
        def _():
            fold_max(near_logits())

        @pl.when(i == 0)
        def _():
            fold_max(first_logits())

        accumulate(jnp.max(acc_ref[...], axis=1, keepdims=True))

    lv = lam_ref[...]
    lam = (jnp.exp(jnp.sum(lv[0:1] * lv[1:2], axis=1, keepdims=True))
           - jnp.exp(jnp.sum(lv[2:3] * lv[3:4], axis=1, keepdims=True)) + lam_init)
    acc = acc_ref[...]
    o = acc[:, :LANES] / acc[:, LANES:]
    o = o[:t] - lam * o[t:]
    o = o * lax.rsqrt(jnp.mean(o * o, axis=-1, keepdims=True) + EPS)
    o_ref[0] = (o * g_ref[...] * (1.0 - lam_init)).astype(BF16)


def _diff_call(lam_vecs, subln, bound, aq, ak, av, bias, t, lam_init):
    bsz, s, _ = aq.shape
    assert t % (2 * LANES) == 0
    return pl.pallas_call(
        functools.partial(_diff_kernel, t=t, lam_init=lam_init),
        grid=(bsz, A_HEADS, s // t),
        in_specs=[pl.BlockSpec(lam_vecs.shape, lambda b, h, i: (0, 0)),
                  pl.BlockSpec(subln.shape, lambda b, h, i: (0, 0)),
                  pl.BlockSpec((1, 8, LANES), lambda b, h, i: (0, 0, 0)),
                  pl.BlockSpec((1, t, A_VDIM), lambda b, h, i: (b, i, h)),
                  pl.BlockSpec((1, s, A_VDIM), lambda b, h, i: (b, 0, h)),
                  pl.BlockSpec((1, s, A_VDIM), lambda b, h, i: (b, 0, h)),
                  pl.BlockSpec((1, 2, t, t), lambda b, h, i: (h, 0, 0, 0))],
        out_specs=pl.BlockSpec((1, t, A_VDIM), lambda b, h, i: (b, i, h)),
        out_shape=jax.ShapeDtypeStruct((bsz, s, A_HEADS * A_VDIM), BF16),
        scratch_shapes=[pltpu.VMEM((2 * t, 2 * LANES), F32)],
        compiler_params=pltpu.CompilerParams(
            dimension_semantics=("arbitrary", "arbitrary", "arbitrary"),
            vmem_limit_bytes=VMEM_LIMIT),
        name="diff_attn",
    )(lam_vecs, subln, bound, aq, ak, av, bias)


def _sortable_key(x):
    b = pltpu.bitcast(x, I32)
    return b ^ ((b >> 31) & 0x7FFFFFFF)


def _dsa_kernel(bound_ref, iq_ref, iw_ref, ik_ref, q_ref, kv_ref, bias_ref, o_ref,
                keys_ref, iqz_ref, wb_ref, thr_ref, tie_ref, acc_ref, *, t, topk):
    i = pl.program_id(1)
    row = lax.broadcasted_iota(I32, (t, t), 0)
    col = lax.broadcasted_iota(I32, (t, t), 1)

    iq = iq_ref[0]
    iw = iw_ref[0]
    lane = lax.broadcasted_iota(I32, (t, LANES), 1)
    for h in range(IDX_HEADS):
        pair = iq[:, (h // 2) * LANES:(h // 2 + 1) * LANES]
        mine = (lane < IDX_DIM) if h % 2 == 0 else (lane >= IDX_DIM)
        iqz_ref[h] = jnp.where(mine, pair, jnp.zeros_like(pair))
        wb_ref[h] = jnp.broadcast_to(iw[:, h:h + 1], (t, LANES))

    def score_block(j, causal):
        kc = ik_ref[0, pl.ds(pl.multiple_of(j * t, t), t), :]
        acc = jnp.zeros((t, t), F32)
        for h in range(IDX_HEADS):
            wb = wb_ref[h]
            wb = jnp.concatenate([wb] * (t // LANES), axis=1)
            acc = acc + wb * jnp.maximum(_dot_nt(iqz_ref[h], kc), 0.0)
        if causal:
            acc = jnp.where(row >= col, acc, -jnp.inf)
        keys_ref[j] = _sortable_key(acc)

    def score_body(j, carry):
        score_block(j, False)
        return carry

    lax.fori_loop(0, i, score_body, 0)
    score_block(i, True)

    ngroups = t // SEL_ROWS

    def group_count(pred, c, r, nfull):
        def add_block(j, cnt, lane_groups):
            k = keys_ref[j, pl.ds(r * SEL_ROWS, SEL_ROWS), :]
            for w in range(lane_groups):
                cnt = cnt + jnp.where(pred(k[:, w * LANES:(w + 1) * LANES], c), 1.0, 0.0)
            return cnt

        cnt = jnp.zeros((SEL_ROWS, LANES), F32)
        if isinstance(nfull, int):
            for j in range(nfull):
                cnt = add_block(j, cnt, t // LANES)
        else:
            cnt = lax.fori_loop(0, nfull, lambda j, cnt: add_block(j, cnt, t // LANES), cnt)
        return add_block(nfull, cnt, -(-(r + 1) * SEL_ROWS // LANES))

    def row_total(cnt):
        return jnp.broadcast_to(jnp.sum(cnt, axis=1, keepdims=True), cnt.shape)

    def count(pred, ref_val, nfull):
        return jnp.concatenate(
            [row_total(group_count(pred, ref_val[r * SEL_ROWS:(r + 1) * SEL_ROWS], r, nfull))
             for r in range(ngroups)], axis=0)

    def ge(k, c):
        return k >= c

    def select(nfull):
        head = (ngroups - 1) * SEL_ROWS

        def resolve(thr_last, cand_last, pending):
            return jnp.where(row_total(pending) >= float(topk), cand_last, thr_last)

        def bit_body(it, carry):
            thr_head, thr_last, cand_last, pending = carry
            bit = jnp.left_shift(jnp.int32(1), 31 - it)
            thr_last = resolve(thr_last, cand_last, pending)
            cand_head = thr_head + bit
            cand_last = thr_last + bit
            keep = count_head(cand_head) >= float(topk)
            return (jnp.where(keep, cand_head, thr_head), thr_last, cand_last,
                    group_count(ge, cand_last, ngroups - 1, nfull))

        def count_head(cand_head):
            return jnp.concatenate(
                [row_total(group_count(ge, cand_head[r * SEL_ROWS:(r + 1) * SEL_ROWS], r, nfull))
                 for r in range(ngroups - 1)], axis=0)

        thr_ref[...] = jnp.full(thr_ref.shape, INT_MIN, I32)
        tie_ref[...] = jnp.zeros(tie_ref.shape, F32)
        start_last = thr_ref[head:, :LANES]
        thr_head, thr_last, cand_last, pending = lax.fori_loop(
            0, 32, bit_body, (thr_ref[:head, :LANES], start_last, start_last, tie_ref[head:]))
        thr = jnp.concatenate([thr_head, resolve(thr_last, cand_last, pending)], axis=0)
        excess = (count(lambda k, c: k >= c, thr, nfull) > float(topk)) & (thr > NEG_KEY)
        thr_ref[...] = jnp.concatenate([thr] * (t // LANES), axis=1)
        tie_ref[...] = jnp.where(excess, 1.0, 0.0)

    for n in range(keys_ref.shape[0]):
        pl.when(i == n)(functools.partial(select, n))

    @pl.when(jnp.max(tie_ref[...]) > 0.0)
    def _():
        tri = (row <= col).astype(BF16)
        thr2 = thr_ref[...]
        excess = tie_ref[...] > 0.0
        need = float(topk) - count(lambda k, c: k > c, thr2[:, :LANES], i)
        need2 = jnp.concatenate([need] * (t // LANES), axis=1)
        excess2 = jnp.concatenate([excess] * (t // LANES), axis=1)

        def body(j, seen):
            k = keys_ref[j]
            eq = k == thr2
            eqf = jnp.where(eq, 1.0, 0.0)
            rank = seen + _dot(eqf.astype(BF16), tri)
            keys_ref[j] = jnp.where(eq & (rank > need2) & excess2, INT_MIN, k)
            return seen + jnp.sum(eqf, axis=1, keepdims=True)

        lax.fori_loop(0, i + 1, body, jnp.zeros((t, 1), F32))

    thr_ref[...] = jnp.maximum(thr_ref[...], NEG_KEY + 1)

    q = q_ref[0]
    qs = []
    for g in range(B_KV_HEADS):
        parts = []
        for r in range(B_GROUP):
            hq = q[:, (g * B_GROUP + r) * HEAD_DIM:(g * B_GROUP + r + 1) * HEAD_DIM]
            z = jnp.zeros_like(hq)
            parts.append(jnp.concatenate([hq, z] if g == 0 else [z, hq], axis=1))
        qs.append(jnp.concatenate(parts, axis=0))

    def logits(j, g, near):
        k = kv_ref[0, pl.ds(pl.multiple_of(j * t, t), t), :LANES]
        madd = jnp.where(keys_ref[j] >= thr_ref[...], 0.0, MASKED)
        s = _dot_nt(qs[g], k).reshape(B_GROUP, t, t)
        if near is None:
            s = s + madd[None]
        else:
            s = s + (bias_ref[g * B_GROUP:(g + 1) * B_GROUP, near] + madd[None])
        return s.reshape(B_GROUP * t, t)

    def for_blocks(fn):
        def far(j, c):
            fn(j, None)
            return c

        lax.fori_loop(0, i - 1, far, 0)

        @pl.when(i > 0)
        def _():
            fn(i - 1, 1)

        fn(i, 0)

    def accumulate(ms):
        acc_ref[...] = jnp.zeros(acc_ref.shape, F32)

        def step(j, near):
            v = kv_ref[0, pl.ds(pl.multiple_of(j * t, t), t), LANES:]
            vx = jnp.concatenate([v, jnp.ones((t, LANES), BF16)], axis=1)
            for g in range(B_KV_HEADS):
                p = jnp.exp2(logits(j, g, near) - ms[g]).astype(BF16)
                acc_ref[g] += _dot(p, vx)

        for_blocks(step)

    m0 = bound_ref[0, 0:1, 0:1]
    accumulate([m0, m0])

    @pl.when(jnp.min(acc_ref[:, :, LANES:LANES + 1]) < L_TINY)
    def _():
        acc_ref[...] = jnp.full(acc_ref.shape, MASKED, F32)

        def step(j, near):
            w = acc_ref.shape[2]
            for g in range(B_KV_HEADS):
                s = logits(j, g, near)
                for c in range(t // w):
                    acc_ref[g] = jnp.maximum(acc_ref[g], s[:, c * w:(c + 1) * w])

        for_blocks(step)
        accumulate([jnp.max(acc_ref[g], axis=1, keepdims=True) for g in range(B_KV_HEADS)])

    outs = []
    for g in range(B_KV_HEADS):
        a = acc_ref[g]
        o = a[:, g * HEAD_DIM:(g + 1) * HEAD_DIM] / a[:, LANES:LANES + HEAD_DIM]
        for r in range(B_GROUP):
            outs.append(o[r * t:(r + 1) * t])
    o_ref[0] = jnp.concatenate(outs, axis=1).astype(BF16)


def _dsa_call(bound, iq, iw, ik, bq, bkv, bias, t, topk):
    bsz, s, _ = bq.shape
    nq = s // t
    assert t % SEL_ROWS == 0 and t % (2 * LANES) == 0
    return pl.pallas_call(
        functools.partial(_dsa_kernel, t=t, topk=topk),
        grid=(bsz, nq),
        in_specs=[pl.BlockSpec((1, 8, LANES), lambda b, i: (1, 0, 0)),
                  pl.BlockSpec((1, t, iq.shape[2]), lambda b, i: (b, i, 0)),
                  pl.BlockSpec((1, t, iw.shape[2]), lambda b, i: (b, i, 0)),
                  pl.BlockSpec((1, s, ik.shape[2]), lambda b, i: (b, 0, 0)),
                  pl.BlockSpec((1, t, bq.shape[2]), lambda b, i: (b, i, 0)),
                  pl.BlockSpec((1, s, bkv.shape[2]), lambda b, i: (b, 0, 0)),
                  pl.BlockSpec(bias.shape, lambda b, i: (0, 0, 0, 0),
                               pipeline_mode=pl.Buffered(1))],
        out_specs=pl.BlockSpec((1, t, B_HEADS * HEAD_DIM), lambda b, i: (b, i, 0)),
        out_shape=jax.ShapeDtypeStruct((bsz, s, B_HEADS * HEAD_DIM), BF16),
        scratch_shapes=[pltpu.VMEM((nq, t, t), I32),
                        pltpu.VMEM((IDX_HEADS, t, LANES), BF16),
                        pltpu.VMEM((IDX_HEADS, t, LANES), F32),
                        pltpu.VMEM((t, t), I32),
                        pltpu.VMEM((t, LANES), F32),
                        pltpu.VMEM((B_KV_HEADS, B_GROUP * t, 2 * LANES), F32)],
        compiler_params=pltpu.CompilerParams(dimension_semantics=("arbitrary", "arbitrary"),
                                             vmem_limit_bytes=VMEM_LIMIT),
        name="dsa_attn",
    )(bound, iq, iw, ik, bq, bkv, bias)


def _ffn_kernel(x_ref, oa_ref, ob_ref, mod_ref, g_ref, wo_ref, wgv_ref, cwg_ref, cwv_ref,
                cbg_ref, cbv_ref, wd_ref, o_ref, carry_ref, h_ref, y_ref, ua_ref, ub_ref, a_ref, *, nf):
    i = pl.program_id(1)
    x = x_ref[0]
    ts = x.shape[0]
    mixed = _dot(oa_ref[0], wo_ref[0]) + _dot(ob_ref[0], wo_ref[1])
    x1 = x + mod_ref[0, 2:3, :] * mixed
    h_ref[...] = _modulated_norm(x1, g_ref[...], mod_ref[0, 4:5, :], mod_ref[0, 3:4, :]).astype(BF16)
    y_ref[...] = x1

    @pl.when(i == 0)
    def _():
        carry_ref[...] = jnp.zeros(carry_ref.shape, F32)

    def up(f, buf_ref):
        fc = a_ref.shape[2]
        dff = wgv_ref.shape[1] // 2
        h = h_ref[...]
        buf_ref[0, 8:8 + ts] = _dot(h, wgv_ref[:, f * fc:(f + 1) * fc])
        buf_ref[1, 8:8 + ts] = _dot(h, wgv_ref[:, dff + f * fc:dff + (f + 1) * fc])

    def conv(buf_ref, part, prev8, cw, cb):
        buf_ref[part, 0:8] = prev8
        return (cb + cw[2:3] * buf_ref[part, 8:8 + ts] + cw[1:2] * buf_ref[part, 7:7 + ts]
                + cw[0:1] * buf_ref[part, 6:6 + ts])

    def gate(f, buf_ref):
        yg = conv(buf_ref, 0, carry_ref[f, 0], cwg_ref[f], cbg_ref[f])
        yv = conv(buf_ref, 1, carry_ref[f, 1], cwv_ref[f], cbv_ref[f])
        carry_ref[f, 0] = buf_ref[0, ts:ts + 8]
        carry_ref[f, 1] = buf_ref[1, ts:ts + 8]
        a_ref[f] = ((yg / (1.0 + jnp.exp(-yg))) * yv).astype(BF16)

    def stage(f, cur_ref, nxt_ref):
        gate(f, cur_ref)
        up(f + 1, nxt_ref)

    up(0, ua_ref)

    assert nf % 2 == 1
    for k in range((nf - 1) // 2):
        stage(2 * k, ua_ref, ub_ref)
        stage(2 * k + 1, ub_ref, ua_ref)
    gate(nf - 1, ua_ref)
    a = jnp.concatenate([a_ref[f] for f in range(nf)], axis=1)
    o_ref[0] = y_ref[...] + mod_ref[0, 5:6, :] * _dot(a, wd_ref[...])


def _ffn_call(x, oa, ob, mod, g_ffn, w_out, w_up, conv_w, conv_b, w_down, ts, fc):
    bsz, s, d = x.shape
    dff = w_down.shape[0]
    nf = dff // fc
    na = oa.shape[2]
    wo = w_out.astype(BF16).reshape(2, na, d)
    wgv = w_up.astype(BF16)
    cwg = conv_w[:, :dff].reshape(CONV_W, nf, fc).transpose(1, 0, 2)
    cwv = conv_w[:, dff:].reshape(CONV_W, nf, fc).transpose(1, 0, 2)
    cbg = conv_b[:dff].reshape(nf, 1, fc)
    cbv = conv_b[dff:].reshape(nf, 1, fc)
    wd = w_down.astype(BF16)

    def full(a):
        return pl.BlockSpec(a.shape, lambda b, i: (0,) * a.ndim, pipeline_mode=pl.Buffered(1))

    def tok(n):
        return pl.BlockSpec((1, ts, n), lambda b, i: (b, i, 0))

    consts = [wo, wgv, cwg, cwv, cbg, cbv, wd]
    return pl.pallas_call(
        functools.partial(_ffn_kernel, nf=nf),
        grid=(bsz, s // ts),
        in_specs=[tok(d), tok(na), tok(ob.shape[2]),
                  pl.BlockSpec((1, 6, d), lambda b, i: (b, 0, 0)), full(g_ffn)]
                 + [full(a) for a in consts],
        out_specs=tok(d),
        out_shape=jax.ShapeDtypeStruct((bsz, s, d), F32),
        scratch_shapes=[pltpu.VMEM((nf, 2, 8, fc), F32), pltpu.VMEM((ts, d), BF16),
                        pltpu.VMEM((ts, d), F32), pltpu.VMEM((2, ts + 8, fc), F32),
                        pltpu.VMEM((2, ts + 8, fc), F32), pltpu.VMEM((nf, ts, fc), BF16)],
        compiler_params=pltpu.CompilerParams(dimension_semantics=("arbitrary", "arbitrary"),
                                             vmem_limit_bytes=VMEM_LIMIT),
        name="outproj_ffn",
    )(x, oa, ob, mod, g_ffn, *consts)


def kernel(x, c, w_ada, b_ada, g_attn, w_in, q_norm_a, k_norm_a, q_norm_b, k_norm_b, lam_vecs,
           subln_a, w_out, g_ffn, w_up, conv_w, conv_b, w_down, rel_bias):
    bsz, s, d = x.shape
    t = min(ATT_T, s)
    ta = min(DIFF_T, s)
    ts = min(512, s)
    topk = min(TOPK_MAX, s // 4)
    scale = HEAD_DIM ** -0.5
    for l in range(w_ada.shape[0]):
        mod = _ada_call(c, w_ada[l], b_ada[l]).reshape(bsz, 6, d)
        gains = jnp.stack([q_norm_a[l], k_norm_a[l], q_norm_b[l], k_norm_b[l]])
        bias_a, bound = _bias_call(rel_bias, gains, ta, 0, A_HEADS)
        bias_b, _ = _bias_call(rel_bias, gains, t, A_HEADS, B_HEADS)
        gqa = (jnp.tile(q_norm_a[l], 2 * A_HEADS) * (scale * LOG2E))[None]
        gka = jnp.tile(k_norm_a[l], 2 * A_HEADS)[None]
        gqb = (jnp.tile(q_norm_b[l], B_HEADS) * (scale * LOG2E))[None]
        gkb = jnp.tile(k_norm_b[l], B_KV_HEADS)[None]
        aq, ak, av, bq, bkv, iq, ik, iw = _inproj_call(
            x, mod, g_attn[l][None], w_in[l], gqa, gka, gqb, gkb, ts)
        lam_init = 0.8 - 0.6 * math.exp(-0.3 * l)
        o_a = _diff_call(lam_vecs[l], subln_a[l][None], bound, aq, ak, av, bias_a, ta, lam_init)
        o_b = _dsa_call(bound, iq, iw, ik, bq, bkv, bias_b, t, topk)
        x = _ffn_call(x, o_a, o_b, mod, g_ffn[l][None], w_out[l], w_up[l], conv_w[l], conv_b[l],
                      w_down[l], ts, 256)
    return x
```

```python
import functools
import math

import jax
import jax.numpy as jnp
from jax import lax
from jax.experimental import pallas as pl
from jax.experimental.pallas import tpu as pltpu

F32 = jnp.float32
BF16 = jnp.bfloat16
I32 = jnp.int32

HEAD_DIM = 64
A_HEADS = 4
A_VDIM = 2 * HEAD_DIM
B_HEADS = 8
B_KV_HEADS = 2
B_GROUP = B_HEADS // B_KV_HEADS
IDX_HEADS = 8
IDX_DIM = 64
TOPK_MAX = 256
N_BUCKETS = 32
MAX_DISTANCE = 128
CONV_W = 3
EPS = 1e-6

LANES = 128
ATT_T = 512
SEL_ROWS = 128
DIFF_T = 512
FAR_BLOCKS = 2
MASKED = -1e30
LOG2E = math.log2(math.e)
L_TINY = 2.0 ** -60
NEG_KEY = -2139095041
INT_MIN = -2147483648
VMEM_LIMIT = 56 * 1024 * 1024


def _dot(a, b):
    return jnp.dot(a, b, preferred_element_type=F32)


def _dot_nt(a, b):
    return lax.dot_general(a, b, (((1,), (1,)), ((), ())), preferred_element_type=F32)


def _split_bf16(a):
    hi = a.astype(BF16)
    lo = (a - hi.astype(F32)).astype(BF16)
    return hi, lo


def _ada_kernel(c_ref, w_ref, b_ref, o_ref):
    c = c_ref[...]
    a = c / (1.0 + jnp.exp(-c))
    a_hi, a_lo = _split_bf16(a)
    w_hi, w_lo = _split_bf16(w_ref[...])
    o_ref[...] = _dot(a_hi, w_hi) + _dot(a_hi, w_lo) + _dot(a_lo, w_hi) + b_ref[...]


def _ada_call(c, w, b):
    bsz, d = c.shape
    n = w.shape[1]
    tn = 1024
    return pl.pallas_call(
        _ada_kernel,
        grid=(n // tn,),
        in_specs=[pl.BlockSpec((bsz, d), lambda j: (0, 0)),
                  pl.BlockSpec((d, tn), lambda j: (0, j)),
                  pl.BlockSpec((1, tn), lambda j: (0, j))],
        out_specs=pl.BlockSpec((bsz, tn), lambda j: (0, j)),
        out_shape=jax.ShapeDtypeStruct((bsz, n), F32),
        compiler_params=pltpu.CompilerParams(dimension_semantics=("arbitrary",),
                                             vmem_limit_bytes=VMEM_LIMIT),
        name="ada_mod",
    )(c, w, b.reshape(1, n))


def _bias_kernel(tab_ref, tabv_ref, gains_ref, o_ref, bound_ref, *, t, h0):
    h = pl.program_id(0) + h0
    max_exact = N_BUCKETS // 2
    b = MAX_DISTANCE
    row = lax.broadcasted_iota(I32, (b, b), 0)
    col = lax.broadcasted_iota(I32, (b, b), 1)
    last = tab_ref[N_BUCKETS - 1, h]
    base = []
    for d in range(2):
        n = jnp.maximum(row - col + d * b, 0)
        nf = jnp.maximum(n, 1).astype(F32)
        large = max_exact + (jnp.log(nf / max_exact) / math.log(MAX_DISTANCE / max_exact)
                             * (N_BUCKETS - max_exact)).astype(I32)
        large = jnp.minimum(large, N_BUCKETS - 1)
        bucket = jnp.where(n < max_exact, n, large)
        acc = jnp.zeros((b, b), F32)
        for k in range(N_BUCKETS - 1):
            acc = jnp.where(bucket == k, (tab_ref[k, h] - last) * LOG2E, acc)
        base.append(acc)
    by_distance = {0: jnp.where(row >= col, base[0], MASKED), 1: base[1]}
    masked = jnp.full((b, b), MASKED, F32)
    zero = jnp.zeros((b, b), F32)
    nb = t // b
    for d in range(2):
        o_ref[0, d] = jnp.concatenate(
            [jnp.concatenate([masked if d * nb + a - c < 0 else by_distance.get(d * nb + a - c, zero)
                              for c in range(nb)], axis=1) for a in range(nb)], axis=0)
    tab = tabv_ref[...]
    rel = (tab - tab[N_BUCKETS - 1:N_BUCKETS, :]) * LOG2E
    hcol = lax.broadcasted_iota(I32, rel.shape, 1)
    g = jnp.max(jnp.abs(gains_ref[...]), axis=1, keepdims=True)
    for grp in range(2):
        in_grp = (hcol < A_HEADS) if grp == 0 else (hcol >= A_HEADS)
        bmax = jnp.max(jnp.where(in_grp, rel, 0.0), keepdims=True)
        qk = g[2 * grp:2 * grp + 1] * g[2 * grp + 1:2 * grp + 2] * (HEAD_DIM ** 0.5 * LOG2E)
        bound_ref[grp] = jnp.broadcast_to(qk + bmax, bound_ref.shape[1:])


def _bias_call(rel_bias, gains, t, h0, nh):
    return pl.pallas_call(
        functools.partial(_bias_kernel, t=t, h0=h0),
        grid=(nh,),
        in_specs=[pl.BlockSpec(memory_space=pltpu.SMEM),
                  pl.BlockSpec(rel_bias.shape, lambda h: (0, 0)),
                  pl.BlockSpec(gains.shape, lambda h: (0, 0))],
        out_specs=[pl.BlockSpec((1, 2, t, t), lambda h: (h, 0, 0, 0)),
                   pl.BlockSpec((2, 8, LANES), lambda h: (0, 0, 0))],
        out_shape=[jax.ShapeDtypeStruct((nh, 2, t, t), F32),
                   jax.ShapeDtypeStruct((2, 8, LANES), F32)],
        compiler_params=pltpu.CompilerParams(dimension_semantics=("arbitrary",)),
        name="bias_tiles",
    )(rel_bias, rel_bias, gains)


def _modulated_norm(x, g, scale, shift):
    y = x * lax.rsqrt(jnp.mean(x * x, axis=-1, keepdims=True) + EPS)
    return (y * g) * (1.0 + scale) + shift


def _head_norm(y, gain, bd):
    y2 = (y * y).astype(BF16)
    w = bd.shape[0]
    ss = jnp.concatenate([_dot(y2[:, c:c + w], bd) for c in range(0, y.shape[1], w)], axis=1)
    return y * lax.rsqrt(ss * (1.0 / HEAD_DIM) + EPS) * gain


def _inproj_kernel(x_ref, mod_ref, g_ref, waq_ref, wak_ref, wav_ref, wbq_ref, wbkv_ref, wiq_ref,
                   wik_ref, gqa_ref, gka_ref, gqb_ref, gkb_ref, bd_ref,
                   aq_ref, ak_ref, av_ref, bq_ref, bkv_ref, iq_ref, ik_ref, iw_ref):
    x = x_ref[0]
    h = _modulated_norm(x, g_ref[...], mod_ref[0, 1:2, :], mod_ref[0, 0:1, :]).astype(BF16)
    bd = bd_ref[...]
    aq_ref[0] = _head_norm(_dot(h, waq_ref[...]), gqa_ref[...], bd).astype(BF16)
    ak_ref[0] = _head_norm(_dot(h, wak_ref[...]), gka_ref[...], bd).astype(BF16)
    av_ref[0] = _dot(h, wav_ref[...]).astype(BF16)
    bq_ref[0] = _head_norm(_dot(h, wbq_ref[...]), gqb_ref[...], bd).astype(BF16)
    kv = _dot(h, wbkv_ref[...])
    nk = B_KV_HEADS * HEAD_DIM
    bk = _head_norm(kv[:, :nk], gkb_ref[...], bd[:nk, :nk])
    bkv_ref[0] = jnp.concatenate([bk, kv[:, nk:]], axis=1).astype(BF16)
    iq_ref[0] = _dot(h, wiq_ref[...]).astype(BF16)
    ikw = _dot(h, wik_ref[...])
    ik_ref[0] = ikw[:, :2 * IDX_DIM].astype(BF16)
    iw_ref[0] = ikw[:, 2 * IDX_DIM:2 * IDX_DIM + IDX_HEADS] * ((IDX_HEADS * IDX_DIM) ** -0.5)


def _inproj_call(x, mod, g_attn, w_in, gqa, gka, gqb, gkb, ts):
    bsz, s, d = x.shape
    na = A_HEADS * 2 * HEAD_DIM
    nb = B_HEADS * HEAD_DIM
    nkv = B_KV_HEADS * HEAD_DIM
    ni = IDX_HEADS * IDX_DIM
    o = 0
    parts = []
    for width in (na, na, A_HEADS * A_VDIM, nb, 2 * nkv, ni, IDX_DIM + IDX_HEADS):
        parts.append(w_in[:, o:o + width].astype(BF16))
        o += width
    wik = jnp.concatenate([parts[-1][:, :IDX_DIM], parts[-1]], axis=1)
    parts[-1] = jnp.pad(wik, ((0, 0), (0, 2 * LANES - wik.shape[1])))
    grp = jnp.arange(2 * LANES) // HEAD_DIM
    bd = (grp[:, None] == grp[None, :]).astype(BF16)

    def full(a):
        return pl.BlockSpec(a.shape, lambda b, i: (0,) * a.ndim)

    def tok(n):
        return pl.BlockSpec((1, ts, n), lambda b, i: (b, i, 0))

    consts = parts + [gqa, gka, gqb, gkb, bd]
    out_w = [(na, BF16), (na, BF16), (A_HEADS * A_VDIM, BF16), (nb, BF16), (2 * nkv, BF16),
             (ni, BF16), (2 * IDX_DIM, BF16), (IDX_HEADS, F32)]
    return pl.pallas_call(
        _inproj_kernel,
        grid=(bsz, s // ts),
        in_specs=[tok(d), pl.BlockSpec((1, 6, d), lambda b, i: (b, 0, 0)), full(g_attn)]
                 + [full(a) for a in consts],
        out_specs=[tok(n) for n, _ in out_w],
        out_shape=[jax.ShapeDtypeStruct((bsz, s, n), dt) for n, dt in out_w],
        compiler_params=pltpu.CompilerParams(dimension_semantics=("arbitrary", "arbitrary"),
                                             vmem_limit_bytes=VMEM_LIMIT),
        name="in_proj",
    )(x, mod, g_attn, *consts)


def _diff_kernel(lam_ref, g_ref, bound_ref, q_ref, k_ref, v_ref, bias_ref, o_ref, acc_ref, *,
                 t, lam_init):
    i = pl.program_id(2)
    q = q_ref[0]
    lane = lax.broadcasted_iota(I32, q.shape, 1)
    zero = jnp.zeros_like(q)
    qq = jnp.concatenate([jnp.where(lane < HEAD_DIM, q, zero),
                          jnp.where(lane >= HEAD_DIM, q, zero)], axis=0)

    def logits(start, width):
        k = k_ref[0, pl.ds(pl.multiple_of(start, t), width), :]
        return _dot_nt(qq, k)

    def v_ones(start, width):
        v = v_ref[0, pl.ds(pl.multiple_of(start, t), width), :]
        return jnp.concatenate([v, jnp.ones((width, LANES), BF16)], axis=1)

    def near_logits():
        b = jnp.concatenate([bias_ref[0, 1], bias_ref[0, 0]], axis=1)
        return logits((i - 1) * t, 2 * t) + jnp.concatenate([b, b], axis=0)

    def first_logits():
        b = bias_ref[0, 0]
        return logits(0, t) + jnp.concatenate([b, b], axis=0)

    def for_far_blocks(fn):
        nfar = jnp.maximum(i - 1, 0)
        nsup = nfar // FAR_BLOCKS

        def sup(j, c):
            fn(j * (FAR_BLOCKS * t), FAR_BLOCKS * t)
            return c

        def single(j, c):
            fn(j * t, t)
            return c

        lax.fori_loop(0, nsup, sup, 0)
        lax.fori_loop(nsup * FAR_BLOCKS, nfar, single, 0)

    def accumulate(m):
        acc_ref[...] = jnp.zeros(acc_ref.shape, F32)

        def far(start, width):
            p = jnp.exp2(logits(start, width) - m).astype(BF16)
            acc_ref[...] += _dot(p, v_ones(start, width))

        for_far_blocks(far)

        @pl.when(i > 0)
        def _():
            p = jnp.exp2(near_logits() - m).astype(BF16)
            acc_ref[...] += _dot(p, v_ones((i - 1) * t, 2 * t))

        @pl.when(i == 0)
        def _():
            p = jnp.exp2(first_logits() - m).astype(BF16)
            acc_ref[...] += _dot(p, v_ones(0, t))

    accumulate(bound_ref[0, 0:1, 0:1])

    @pl.when(jnp.min(acc_ref[:, LANES:LANES + 1]) < L_TINY)
    def _():
        acc_ref[...] = jnp.full(acc_ref.shape, MASKED, F32)

        def fold_max(s):
            w = acc_ref.shape[1]
            for c in range(s.shape[1] // w):
                acc_ref[...] = jnp.maximum(acc_ref[...], s[:, c * w:(c + 1) * w])

        for_far_blocks(lambda start, width: fold_max(logits(start, width)))

        @pl.when(i > 0)
        def _():
            fold_max(near_logits())

        @pl.when(i == 0)
        def _():
            fold_max(first_logits())

        accumulate(jnp.max(acc_ref[...], axis=1, keepdims=True))

    lv = lam_ref[...]
    lam = (jnp.exp(jnp.sum(lv[0:1] * lv[1:2], axis=1, keepdims=True))
           - jnp.exp(jnp.sum(lv[2:3] * lv[3:4], axis=1, keepdims=True)) + lam_init)
    acc = acc_ref[...]
    o = acc[:, :LANES] / acc[:, LANES:]
    o = o[:t] - lam * o[t:]
    o = o * lax.rsqrt(jnp.mean(o * o, axis=-1, keepdims=True) + EPS)
    o_ref[0] = (o * g_ref[...] * (1.0 - lam_init)).astype(BF16)


def _diff_call(lam_vecs, subln, bound, aq, ak, av, bias, t, lam_init):
    bsz, s, _ = aq.shape
    assert t % (2 * LANES) == 0
    return pl.pallas_call(
        functools.partial(_diff_kernel, t=t, lam_init=lam_init),
        grid=(bsz, A_HEADS, s // t),
        in_specs=[pl.BlockSpec(lam_vecs.shape, lambda b, h, i: (0, 0)),
                  pl.BlockSpec(subln.shape, lambda b, h, i: (0, 0)),
                  pl.BlockSpec((1, 8, LANES), lambda b, h, i: (0, 0, 0)),
                  pl.BlockSpec((1, t, A_VDIM), lambda b, h, i: (b, i, h)),
                  pl.BlockSpec((1, s, A_VDIM), lambda b, h, i: (b, 0, h)),
                  pl.BlockSpec((1, s, A_VDIM), lambda b, h, i: (b, 0, h)),
                  pl.BlockSpec((1, 2, t, t), lambda b, h, i: (h, 0, 0, 0))],
        out_specs=pl.BlockSpec((1, t, A_VDIM), lambda b, h, i: (b, i, h)),
        out_shape=jax.ShapeDtypeStruct((bsz, s, A_HEADS * A_VDIM), BF16),
        scratch_shapes=[pltpu.VMEM((2 * t, 2 * LANES), F32)],
        compiler_params=pltpu.CompilerParams(
            dimension_semantics=("arbitrary", "arbitrary", "arbitrary"),
            vmem_limit_bytes=VMEM_LIMIT),
        name="diff_attn",
    )(lam_vecs, subln, bound, aq, ak, av, bias)


def _sortable_key(x):
    b = pltpu.bitcast(x, I32)
    return b ^ ((b >> 31) & 0x7FFFFFFF)


def _dsa_kernel(bound_ref, iq_ref, iw_ref, ik_ref, q_ref, kv_ref, bias_ref, o_ref,
                keys_ref, iqz_ref, wb_ref, thr_ref, tie_ref, acc_ref, *, t, topk):
    i = pl.program_id(1)
    row = lax.broadcasted_iota(I32, (t, t), 0)
    col = lax.broadcasted_iota(I32, (t, t), 1)

    iq = iq_ref[0]
    iw = iw_ref[0]
    lane = lax.broadcasted_iota(I32, (t, LANES), 1)
    for h in range(IDX_HEADS):
        pair = iq[:, (h // 2) * LANES:(h // 2 + 1) * LANES]
        mine = (lane < IDX_DIM) if h % 2 == 0 else (lane >= IDX_DIM)
        iqz_ref[h] = jnp.where(mine, pair, jnp.zeros_like(pair))
        wb_ref[h] = jnp.broadcast_to(iw[:, h:h + 1], (t, LANES))

    def score_block(j, causal):
        kc = ik_ref[0, pl.ds(pl.multiple_of(j * t, t), t), :]
        acc = jnp.zeros((t, t), F32)
        for h in range(IDX_HEADS):
            wb = wb_ref[h]
            wb = jnp.concatenate([wb] * (t // LANES), axis=1)
            acc = acc + wb * jnp.maximum(_dot_nt(iqz_ref[h], kc), 0.0)
        if causal:
            acc = jnp.where(row >= col, acc, -jnp.inf)
        keys_ref[j] = _sortable_key(acc)

    def score_body(j, carry):
        score_block(j, False)
        return carry

    lax.fori_loop(0, i, score_body, 0)
    score_block(i, True)

    ngroups = t // SEL_ROWS

    def group_count(pred, c, r, nfull):
        def add_block(j, cnt, lane_groups):
            k = keys_ref[j, pl.ds(r * SEL_ROWS, SEL_ROWS), :]
            for w in range(lane_groups):
                cnt = cnt + jnp.where(pred(k[:, w * LANES:(w + 1) * LANES], c), 1.0, 0.0)
            return cnt

        cnt = jnp.zeros((SEL_ROWS, LANES), F32)
        if isinstance(nfull, int):
            for j in range(nfull):
                cnt = add_block(j, cnt, t // LANES)
        else:
            cnt = lax.fori_loop(0, nfull, lambda j, cnt: add_block(j, cnt, t // LANES), cnt)
        return add_block(nfull, cnt, -(-(r + 1) * SEL_ROWS // LANES))

    def row_total(cnt):
        return jnp.broadcast_to(jnp.sum(cnt, axis=1, keepdims=True), cnt.shape)

    def count(pred, ref_val, nfull):
        return jnp.concatenate(
            [row_total(group_count(pred, ref_val[r * SEL_ROWS:(r + 1) * SEL_ROWS], r, nfull))
             for r in range(ngroups)], axis=0)

    def ge(k, c):
        return k >= c

    def select(nfull):
        head = (ngroups - 1) * SEL_ROWS

        def resolve(thr_last, cand_last, pending):
            return jnp.where(row_total(pending) >= float(topk), cand_last, thr_last)

        def bit_body(it, carry):
            thr_head, thr_last, cand_last, pending = carry
            bit = jnp.left_shift(jnp.int32(1), 31 - it)
            thr_last = resolve(thr_last, cand_last, pending)
            cand_head = thr_head + bit
            cand_last = thr_last + bit
            keep = count_head(cand_head) >= float(topk)
            return (jnp.where(keep, cand_head, thr_head), thr_last, cand_last,
                    group_count(ge, cand_last, ngroups - 1, nfull))

        def count_head(cand_head):
            return jnp.concatenate(
                [row_total(group_count(ge, cand_head[r * SEL_ROWS:(r + 1) * SEL_ROWS], r, nfull))
                 for r in range(ngroups - 1)], axis=0)

        thr_ref[...] = jnp.full(thr_ref.shape, INT_MIN, I32)
        tie_ref[...] = jnp.zeros(tie_ref.shape, F32)
        start_last = thr_ref[head:, :LANES]
        thr_head, thr_last, cand_last, pending = lax.fori_loop(
            0, 32, bit_body, (thr_ref[:head, :LANES], start_last, start_last, tie_ref[head:]))
        thr = jnp.concatenate([thr_head, resolve(thr_last, cand_last, pending)], axis=0)
        excess = (count(lambda k, c: k >= c, thr, nfull) > float(topk)) & (thr > NEG_KEY)
        thr_ref[...] = jnp.concatenate([thr] * (t // LANES), axis=1)
        tie_ref[...] = jnp.where(excess, 1.0, 0.0)

    for n in range(keys_ref.shape[0]):
        pl.when(i == n)(functools.partial(select, n))

    @pl.when(jnp.max(tie_ref[...]) > 0.0)
    def _():
        tri = (row <= col).astype(BF16)
        thr2 = thr_ref[...]
        excess = tie_ref[...] > 0.0
        need = float(topk) - count(lambda k, c: k > c, thr2[:, :LANES], i)
        need2 = jnp.concatenate([need] * (t // LANES), axis=1)
        excess2 = jnp.concatenate([excess] * (t // LANES), axis=1)

        def body(j, seen):
            k = keys_ref[j]
            eq = k == thr2
            eqf = jnp.where(eq, 1.0, 0.0)
            rank = seen + _dot(eqf.astype(BF16), tri)
            keys_ref[j] = jnp.where(eq & (rank > need2) & excess2, INT_MIN, k)
            return seen + jnp.sum(eqf, axis=1, keepdims=True)

        lax.fori_loop(0, i + 1, body, jnp.zeros((t, 1), F32))

    thr_ref[...] = jnp.maximum(thr_ref[...], NEG_KEY + 1)

    q = q_ref[0]
    qs = []
    for g in range(B_KV_HEADS):
        parts = []
        for r in range(B_GROUP):
            hq = q[:, (g * B_GROUP + r) * HEAD_DIM:(g * B_GROUP + r + 1) * HEAD_DIM]
            z = jnp.zeros_like(hq)
            parts.append(jnp.concatenate([hq, z] if g == 0 else [z, hq], axis=1))
        qs.append(jnp.concatenate(parts, axis=0))

    def logits(j, g, near):
        k = kv_ref[0, pl.ds(pl.multiple_of(j * t, t), t), :LANES]
        madd = jnp.where(keys_ref[j] >= thr_ref[...], 0.0, MASKED)
        s = _dot_nt(qs[g], k).reshape(B_GROUP, t, t)
        if near is None:
            s = s + madd[None]
        else:
            s = s + (bias_ref[g * B_GROUP:(g + 1) * B_GROUP, near] + madd[None])
        return s.reshape(B_GROUP * t, t)

    def for_blocks(fn):
        def far(j, c):
            fn(j, None)
            return c

        lax.fori_loop(0, i - 1, far, 0)

        @pl.when(i > 0)
        def _():
            fn(i - 1, 1)

        fn(i, 0)

    def accumulate(ms):
        acc_ref[...] = jnp.zeros(acc_ref.shape, F32)

        def step(j, near):
            v = kv_ref[0, pl.ds(pl.multiple_of(j * t, t), t), LANES:]
            vx = jnp.concatenate([v, jnp.ones((t, LANES), BF16)], axis=1)
            for g in range(B_KV_HEADS):
                p = jnp.exp2(logits(j, g, near) - ms[g]).astype(BF16)
                acc_ref[g] += _dot(p, vx)

        for_blocks(step)

    m0 = bound_ref[0, 0:1, 0:1]
    accumulate([m0, m0])

    @pl.when(jnp.min(acc_ref[:, :, LANES:LANES + 1]) < L_TINY)
    def _():
        acc_ref[...] = jnp.full(acc_ref.shape, MASKED, F32)

        def step(j, near):
            w = acc_ref.shape[2]
            for g in range(B_KV_HEADS):
                s = logits(j, g, near)
                for c in range(t // w):
                    acc_ref[g] = jnp.maximum(acc_ref[g], s[:, c * w:(c + 1) * w])

        for_blocks(step)
        accumulate([jnp.max(acc_ref[g], axis=1, keepdims=True) for g in range(B_KV_HEADS)])

    outs = []
    for g in range(B_KV_HEADS):
        a = acc_ref[g]
        o = a[:, g * HEAD_DIM:(g + 1) * HEAD_DIM] / a[:, LANES:LANES + HEAD_DIM]
        for r in range(B_GROUP):
            outs.append(o[r * t:(r + 1) * t])
    o_ref[0] = jnp.concatenate(outs, axis=1).astype(BF16)


def _dsa_call(bound, iq, iw, ik, bq, bkv, bias, t, topk):
    bsz, s, _ = bq.shape
    nq = s // t
    assert t % SEL_ROWS == 0 and t % (2 * LANES) == 0
    return pl.pallas_call(
        functools.partial(_dsa_kernel, t=t, topk=topk),
        grid=(bsz, nq),
        in_specs=[pl.BlockSpec((1, 8, LANES), lambda b, i: (1, 0, 0)),
                  pl.BlockSpec((1, t, iq.shape[2]), lambda b, i: (b, i, 0)),
                  pl.BlockSpec((1, t, iw.shape[2]), lambda b, i: (b, i, 0)),
                  pl.BlockSpec((1, s, ik.shape[2]), lambda b, i: (b, 0, 0)),
                  pl.BlockSpec((1, t, bq.shape[2]), lambda b, i: (b, i, 0)),
                  pl.BlockSpec((1, s, bkv.shape[2]), lambda b, i: (b, 0, 0)),
                  pl.BlockSpec(bias.shape, lambda b, i: (0, 0, 0, 0),
                               pipeline_mode=pl.Buffered(1))],
        out_specs=pl.BlockSpec((1, t, B_HEADS * HEAD_DIM), lambda b, i: (b, i, 0)),
        out_shape=jax.ShapeDtypeStruct((bsz, s, B_HEADS * HEAD_DIM), BF16),
        scratch_shapes=[pltpu.VMEM((nq, t, t), I32),
                        pltpu.VMEM((IDX_HEADS, t, LANES), BF16),
                        pltpu.VMEM((IDX_HEADS, t, LANES), F32),
                        pltpu.VMEM((t, t), I32),
                        pltpu.VMEM((t, LANES), F32),
                        pltpu.VMEM((B_KV_HEADS, B_GROUP * t, 2 * LANES), F32)],
        compiler_params=pltpu.CompilerParams(dimension_semantics=("arbitrary", "arbitrary"),
                                             vmem_limit_bytes=VMEM_LIMIT),
        name="dsa_attn",
    )(bound, iq, iw, ik, bq, bkv, bias)


def _ffn_kernel(x_ref, oa_ref, ob_ref, xn_ref, oan_ref, obn_ref, mod_ref, g_ref, wo_ref, wgv_ref,
                cwg_ref, cwv_ref, cbg_ref, cbv_ref, wd_ref, o_ref, carry_ref, h_ref, y_ref, hn_ref,
                yn_ref, ua_ref, ub_ref, a_ref, *, nf):
    b = pl.program_id(0)
    i = pl.program_id(1)
    ts = x_ref.shape[1]
    mod = mod_ref[b]

    def residual_and_norm(xr, oar, obr, m, x1_out, h_out):
        x1 = xr[0] + m[2:3] * (_dot(oar[0], wo_ref[0]) + _dot(obr[0], wo_ref[1]))
        h_out[...] = _modulated_norm(x1, g_ref[...], m[4:5], m[3:4]).astype(BF16)
        x1_out[...] = x1

    @pl.when((b == 0) & (i == 0))
    def _():
        residual_and_norm(x_ref, oa_ref, ob_ref, mod, yn_ref, hn_ref)

    h_ref[...] = hn_ref[...]
    y_ref[...] = yn_ref[...]

    @pl.when(i == 0)
    def _():
        carry_ref[...] = jnp.zeros(carry_ref.shape, F32)

    def up(f, buf_ref):
        fc = a_ref.shape[2]
        dff = wgv_ref.shape[1] // 2
        h = h_ref[...]
        buf_ref[0, 8:8 + ts] = _dot(h, wgv_ref[:, f * fc:(f + 1) * fc])
        buf_ref[1, 8:8 + ts] = _dot(h, wgv_ref[:, dff + f * fc:dff + (f + 1) * fc])

    def conv(buf_ref, part, prev8, cw, cb):
        buf_ref[part, 0:8] = prev8
        return (cb + cw[2:3] * buf_ref[part, 8:8 + ts] + cw[1:2] * buf_ref[part, 7:7 + ts]
                + cw[0:1] * buf_ref[part, 6:6 + ts])

    def gate(f, buf_ref):
        yg = conv(buf_ref, 0, carry_ref[f, 0], cwg_ref[f], cbg_ref[f])
        yv = conv(buf_ref, 1, carry_ref[f, 1], cwv_ref[f], cbv_ref[f])
        carry_ref[f, 0] = buf_ref[0, ts:ts + 8]
        carry_ref[f, 1] = buf_ref[1, ts:ts + 8]
        a_ref[f] = ((yg / (1.0 + jnp.exp(-yg))) * yv).astype(BF16)

    def stage(f, cur_ref, nxt_ref):
        gate(f, cur_ref)
        up(f + 1, nxt_ref)

    up(0, ua_ref)

    assert nf % 2 == 1
    for k in range((nf - 1) // 2):
        stage(2 * k, ua_ref, ub_ref)
        stage(2 * k + 1, ub_ref, ua_ref)
    gate(nf - 1, ua_ref)
    nxt = b * pl.num_programs(1) + i + 1
    bn = jnp.minimum(nxt // pl.num_programs(1), pl.num_programs(0) - 1)
    residual_and_norm(xn_ref, oan_ref, obn_ref, mod_ref[bn], yn_ref, hn_ref)
    a = jnp.concatenate([a_ref[f] for f in range(nf)], axis=1)
    o_ref[0] = y_ref[...] + mod[5:6] * _dot(a, wd_ref[...])


def _ffn_call(x, oa, ob, mod, g_ffn, w_out, w_up, conv_w, conv_b, w_down, ts, fc):
    bsz, s, d = x.shape
    dff = w_down.shape[0]
    nf = dff // fc
    na = oa.shape[2]
    wo = w_out.astype(BF16).reshape(2, na, d)
    wgv = w_up.astype(BF16)
    cwg = conv_w[:, :dff].reshape(CONV_W, nf, fc).transpose(1, 0, 2)
    cwv = conv_w[:, dff:].reshape(CONV_W, nf, fc).transpose(1, 0, 2)
    cbg = conv_b[:dff].reshape(nf, 1, fc)
    cbv = conv_b[dff:].reshape(nf, 1, fc)
    wd = w_down.astype(BF16)

    def full(a):
        return pl.BlockSpec(a.shape, lambda b, i: (0,) * a.ndim, pipeline_mode=pl.Buffered(1))

    nt = s // ts

    def tok(n):
        return pl.BlockSpec((1, ts, n), lambda b, i: (b, i, 0))

    def tok_next(n):
        def index(b, i):
            g = jnp.minimum(b * nt + i + 1, bsz * nt - 1)
            return (g // nt, g % nt, 0)
        return pl.BlockSpec((1, ts, n), index)

    consts = [wo, wgv, cwg, cwv, cbg, cbv, wd]
    return pl.pallas_call(
        functools.partial(_ffn_kernel, nf=nf),
        grid=(bsz, nt),
        in_specs=[tok(d), tok(na), tok(ob.shape[2]), tok_next(d), tok_next(na),
                  tok_next(ob.shape[2]), full(mod), full(g_ffn)]
                 + [full(a) for a in consts],
        out_specs=tok(d),
        out_shape=jax.ShapeDtypeStruct((bsz, s, d), F32),
        scratch_shapes=[pltpu.VMEM((nf, 2, 8, fc), F32), pltpu.VMEM((ts, d), BF16),
                        pltpu.VMEM((ts, d), F32), pltpu.VMEM((ts, d), BF16),
                        pltpu.VMEM((ts, d), F32), pltpu.VMEM((2, ts + 8, fc), F32),
                        pltpu.VMEM((2, ts + 8, fc), F32), pltpu.VMEM((nf, ts, fc), BF16)],
        compiler_params=pltpu.CompilerParams(dimension_semantics=("arbitrary", "arbitrary"),
                                             vmem_limit_bytes=VMEM_LIMIT),
        name="outproj_ffn",
    )(x, oa, ob, x, oa, ob, mod, g_ffn, *consts)


def kernel(x, c, w_ada, b_ada, g_attn, w_in, q_norm_a, k_norm_a, q_norm_b, k_norm_b, lam_vecs,
           subln_a, w_out, g_ffn, w_up, conv_w, conv_b, w_down, rel_bias):
    bsz, s, d = x.shape
    t = min(ATT_T, s)
    ta = min(DIFF_T, s)
    ts = min(512, s)
    topk = min(TOPK_MAX, s // 4)
    scale = HEAD_DIM ** -0.5
    for l in range(w_ada.shape[0]):
        mod = _ada_call(c, w_ada[l], b_ada[l]).reshape(bsz, 6, d)
        gains = jnp.stack([q_norm_a[l], k_norm_a[l], q_norm_b[l], k_norm_b[l]])
        bias_a, bound = _bias_call(rel_bias, gains, ta, 0, A_HEADS)
        bias_b, _ = _bias_call(rel_bias, gains, t, A_HEADS, B_HEADS)
        gqa = (jnp.tile(q_norm_a[l], 2 * A_HEADS) * (scale * LOG2E))[None]
        gka = jnp.tile(k_norm_a[l], 2 * A_HEADS)[None]
        gqb = (jnp.tile(q_norm_b[l], B_HEADS) * (scale * LOG2E))[None]
        gkb = jnp.tile(k_norm_b[l], B_KV_HEADS)[None]
        aq, ak, av, bq, bkv, iq, ik, iw = _inproj_call(
            x, mod, g_attn[l][None], w_in[l], gqa, gka, gqb, gkb, ts)
        lam_init = 0.8 - 0.6 * math.exp(-0.3 * l)
        o_a = _diff_call(lam_vecs[l], subln_a[l][None], bound, aq, ak, av, bias_a, ta, lam_init)
        o_b = _dsa_call(bound, iq, iw, ik, bq, bkv, bias_b, t, topk)
        x = _ffn_call(x, o_a, o_b, mod, g_ffn[l][None], w_out[l], w_up[l], conv_w[l], conv_b[l],
                      w_down[l], ts, 256)
    return x
```

```python
import functools
import math

import jax
import jax.numpy as jnp
from jax import lax
from jax.experimental import pallas as pl
from jax.experimental.pallas import tpu as pltpu

F32 = jnp.float32
BF16 = jnp.bfloat16
I32 = jnp.int32

HEAD_DIM = 64
A_HEADS = 4
A_VDIM = 2 * HEAD_DIM
B_HEADS = 8
B_KV_HEADS = 2
B_GROUP = B_HEADS // B_KV_HEADS
IDX_HEADS = 8
IDX_DIM = 64
TOPK_MAX = 256
N_BUCKETS = 32
MAX_DISTANCE = 128
CONV_W = 3
EPS = 1e-6

LANES = 128
ATT_T = 512
SEL_ROWS = 128
DIFF_T = 512
FAR_BLOCKS = 2
MASKED = -1e30
LOG2E = math.log2(math.e)
L_TINY = 2.0 ** -60
NEG_KEY = -2139095041
INT_MIN = -2147483648
VMEM_LIMIT = 56 * 1024 * 1024


def _dot(a, b):
    return jnp.dot(a, b, preferred_element_type=F32)


def _dot_nt(a, b):
    return lax.dot_general(a, b, (((1,), (1,)), ((), ())), preferred_element_type=F32)


def _split_bf16(a):
    hi = a.astype(BF16)
    lo = (a - hi.astype(F32)).astype(BF16)
    return hi, lo


def _ada_kernel(c_ref, w_ref, b_ref, o_ref):
    c = c_ref[...]
    a = c / (1.0 + jnp.exp(-c))
    a_hi, a_lo = _split_bf16(a)
    w_hi, w_lo = _split_bf16(w_ref[...])
    o_ref[...] = _dot(a_hi, w_hi) + _dot(a_hi, w_lo) + _dot(a_lo, w_hi) + b_ref[...]


def _ada_call(c, w, b):
    bsz, d = c.shape
    n = w.shape[1]
    tn = 1024
    return pl.pallas_call(
        _ada_kernel,
        grid=(n // tn,),
        in_specs=[pl.BlockSpec((bsz, d), lambda j: (0, 0)),
                  pl.BlockSpec((d, tn), lambda j: (0, j)),
                  pl.BlockSpec((1, tn), lambda j: (0, j))],
        out_specs=pl.BlockSpec((bsz, tn), lambda j: (0, j)),
        out_shape=jax.ShapeDtypeStruct((bsz, n), F32),
        compiler_params=pltpu.CompilerParams(dimension_semantics=("arbitrary",),
                                             vmem_limit_bytes=VMEM_LIMIT),
        name="ada_mod",
    )(c, w, b.reshape(1, n))


def _bias_kernel(tab_ref, tabv_ref, gains_ref, o_ref, bound_ref, *, t, h0):
    h = pl.program_id(0) + h0
    max_exact = N_BUCKETS // 2
    b = MAX_DISTANCE
    row = lax.broadcasted_iota(I32, (b, b), 0)
    col = lax.broadcasted_iota(I32, (b, b), 1)
    last = tab_ref[N_BUCKETS - 1, h]
    base = []
    for d in range(2):
        n = jnp.maximum(row - col + d * b, 0)
        nf = jnp.maximum(n, 1).astype(F32)
        large = max_exact + (jnp.log(nf / max_exact) / math.log(MAX_DISTANCE / max_exact)
                             * (N_BUCKETS - max_exact)).astype(I32)
        large = jnp.minimum(large, N_BUCKETS - 1)
        bucket = jnp.where(n < max_exact, n, large)
        acc = jnp.zeros((b, b), F32)
        for k in range(N_BUCKETS - 1):
            acc = jnp.where(bucket == k, (tab_ref[k, h] - last) * LOG2E, acc)
        base.append(acc)
    by_distance = {0: jnp.where(row >= col, base[0], MASKED), 1: base[1]}
    masked = jnp.full((b, b), MASKED, F32)
    zero = jnp.zeros((b, b), F32)
    nb = t // b
    for d in range(2):
        o_ref[0, d] = jnp.concatenate(
            [jnp.concatenate([masked if d * nb + a - c < 0 else by_distance.get(d * nb + a - c, zero)
                              for c in range(nb)], axis=1) for a in range(nb)], axis=0)
    tab = tabv_ref[...]
    rel = (tab - tab[N_BUCKETS - 1:N_BUCKETS, :]) * LOG2E
    hcol = lax.broadcasted_iota(I32, rel.shape, 1)
    g = jnp.max(jnp.abs(gains_ref[...]), axis=1, keepdims=True)
    for grp in range(2):
        in_grp = (hcol < A_HEADS) if grp == 0 else (hcol >= A_HEADS)
        bmax = jnp.max(jnp.where(in_grp, rel, 0.0), keepdims=True)
        qk = g[2 * grp:2 * grp + 1] * g[2 * grp + 1:2 * grp + 2] * (HEAD_DIM ** 0.5 * LOG2E)
        bound_ref[grp] = jnp.broadcast_to(qk + bmax, bound_ref.shape[1:])


def _bias_call(rel_bias, gains, t, h0, nh):
    return pl.pallas_call(
        functools.partial(_bias_kernel, t=t, h0=h0),
        grid=(nh,),
        in_specs=[pl.BlockSpec(memory_space=pltpu.SMEM),
                  pl.BlockSpec(rel_bias.shape, lambda h: (0, 0)),
                  pl.BlockSpec(gains.shape, lambda h: (0, 0))],
        out_specs=[pl.BlockSpec((1, 2, t, t), lambda h: (h, 0, 0, 0)),
                   pl.BlockSpec((2, 8, LANES), lambda h: (0, 0, 0))],
        out_shape=[jax.ShapeDtypeStruct((nh, 2, t, t), F32),
                   jax.ShapeDtypeStruct((2, 8, LANES), F32)],
        compiler_params=pltpu.CompilerParams(dimension_semantics=("arbitrary",)),
        name="bias_tiles",
    )(rel_bias, rel_bias, gains)


def _modulated_norm(x, g, scale, shift):
    y = x * lax.rsqrt(jnp.mean(x * x, axis=-1, keepdims=True) + EPS)
    return (y * g) * (1.0 + scale) + shift


def _head_norm(y, gain, bd):
    y2 = (y * y).astype(BF16)
    w = bd.shape[0]
    ss = jnp.concatenate([_dot(y2[:, c:c + w], bd) for c in range(0, y.shape[1], w)], axis=1)
    return y * lax.rsqrt(ss * (1.0 / HEAD_DIM) + EPS) * gain


def _inproj_kernel(x_ref, mod_ref, g_ref, waq_ref, wak_ref, wav_ref, wbq_ref, wbkv_ref, wiq_ref,
                   wik_ref, gqa_ref, gka_ref, gqb_ref, gkb_ref, bd_ref,
                   aq_ref, ak_ref, av_ref, bq_ref, bkv_ref, iq_ref, ik_ref, iw_ref):
    x = x_ref[0]
    h = _modulated_norm(x, g_ref[...], mod_ref[0, 1:2, :], mod_ref[0, 0:1, :]).astype(BF16)
    bd = bd_ref[...]
    aq_ref[0] = _head_norm(_dot(h, waq_ref[...]), gqa_ref[...], bd).astype(BF16)
    ak_ref[0] = _head_norm(_dot(h, wak_ref[...]), gka_ref[...], bd).astype(BF16)
    av_ref[0] = _dot(h, wav_ref[...]).astype(BF16)
    bq_ref[0] = _head_norm(_dot(h, wbq_ref[...]), gqb_ref[...], bd).astype(BF16)
    kv = _dot(h, wbkv_ref[...])
    nk = B_KV_HEADS * HEAD_DIM
    bk = _head_norm(kv[:, :nk], gkb_ref[...], bd[:nk, :nk])
    bkv_ref[0] = jnp.concatenate([bk, kv[:, nk:]], axis=1).astype(BF16)
    iq_ref[0] = _dot(h, wiq_ref[...]).astype(BF16)
    ikw = _dot(h, wik_ref[...])
    ik_ref[0] = ikw[:, :2 * IDX_DIM].astype(BF16)
    iw_ref[0] = ikw[:, 2 * IDX_DIM:2 * IDX_DIM + IDX_HEADS] * ((IDX_HEADS * IDX_DIM) ** -0.5)


def _inproj_call(x, mod, g_attn, w_in, gqa, gka, gqb, gkb, ts):
    bsz, s, d = x.shape
    na = A_HEADS * 2 * HEAD_DIM
    nb = B_HEADS * HEAD_DIM
    nkv = B_KV_HEADS * HEAD_DIM
    ni = IDX_HEADS * IDX_DIM
    o = 0
    parts = []
    for width in (na, na, A_HEADS * A_VDIM, nb, 2 * nkv, ni, IDX_DIM + IDX_HEADS):
        parts.append(w_in[:, o:o + width].astype(BF16))
        o += width
    wik = jnp.concatenate([parts[-1][:, :IDX_DIM], parts[-1]], axis=1)
    parts[-1] = jnp.pad(wik, ((0, 0), (0, 2 * LANES - wik.shape[1])))
    grp = jnp.arange(2 * LANES) // HEAD_DIM
    bd = (grp[:, None] == grp[None, :]).astype(BF16)

    def full(a):
        return pl.BlockSpec(a.shape, lambda b, i: (0,) * a.ndim)

    def tok(n):
        return pl.BlockSpec((1, ts, n), lambda b, i: (b, i, 0))

    consts = parts + [gqa, gka, gqb, gkb, bd]
    out_w = [(na, BF16), (na, BF16), (A_HEADS * A_VDIM, BF16), (nb, BF16), (2 * nkv, BF16),
             (ni, BF16), (2 * IDX_DIM, BF16), (IDX_HEADS, F32)]
    return pl.pallas_call(
        _inproj_kernel,
        grid=(bsz, s // ts),
        in_specs=[tok(d), pl.BlockSpec((1, 6, d), lambda b, i: (b, 0, 0)), full(g_attn)]
                 + [full(a) for a in consts],
        out_specs=[tok(n) for n, _ in out_w],
        out_shape=[jax.ShapeDtypeStruct((bsz, s, n), dt) for n, dt in out_w],
        compiler_params=pltpu.CompilerParams(dimension_semantics=("arbitrary", "arbitrary"),
                                             vmem_limit_bytes=VMEM_LIMIT),
        name="in_proj",
    )(x, mod, g_attn, *consts)


def _diff_kernel(lam_ref, g_ref, bound_ref, q_ref, k_ref, v_ref, bias_ref, o_ref, acc_ref, *,
                 t, lam_init):
    i = pl.program_id(2)
    q = q_ref[0]
    lane = lax.broadcasted_iota(I32, q.shape, 1)
    zero = jnp.zeros_like(q)
    qq = jnp.concatenate([jnp.where(lane < HEAD_DIM, q, zero),
                          jnp.where(lane >= HEAD_DIM, q, zero)], axis=0)

    def logits(start, width):
        k = k_ref[0, pl.ds(pl.multiple_of(start, t), width), :]
        return _dot_nt(qq, k)

    def v_ones(start, width):
        v = v_ref[0, pl.ds(pl.multiple_of(start, t), width), :]
        return jnp.concatenate([v, jnp.ones((width, LANES), BF16)], axis=1)

    def near_logits():
        b = jnp.concatenate([bias_ref[0, 1], bias_ref[0, 0]], axis=1)
        return logits((i - 1) * t, 2 * t) + jnp.concatenate([b, b], axis=0)

    def first_logits():
        b = bias_ref[0, 0]
        return logits(0, t) + jnp.concatenate([b, b], axis=0)

    def for_far_blocks(fn):
        nfar = jnp.maximum(i - 1, 0)
        nsup = nfar // FAR_BLOCKS

        def sup(j, c):
            fn(j * (FAR_BLOCKS * t), FAR_BLOCKS * t)
            return c

        def single(j, c):
            fn(j * t, t)
            return c

        lax.fori_loop(0, nsup, sup, 0)
        lax.fori_loop(nsup * FAR_BLOCKS, nfar, single, 0)

    def accumulate(m):
        acc_ref[...] = jnp.zeros(acc_ref.shape, F32)

        def far(start, width):
            p = jnp.exp2(logits(start, width) - m).astype(BF16)
            acc_ref[...] += _dot(p, v_ones(start, width))

        for_far_blocks(far)

        @pl.when(i > 0)
        def _():
            p = jnp.exp2(near_logits() - m).astype(BF16)
            acc_ref[...] += _dot(p, v_ones((i - 1) * t, 2 * t))

        @pl.when(i == 0)
        def _():
            p = jnp.exp2(first_logits() - m).astype(BF16)
            acc_ref[...] += _dot(p, v_ones(0, t))

    accumulate(bound_ref[0, 0:1, 0:1])

    @pl.when(jnp.min(acc_ref[:, LANES:LANES + 1]) < L_TINY)
    def _():
        acc_ref[...] = jnp.full(acc_ref.shape, MASKED, F32)

        def fold_max(s):
            w = acc_ref.shape[1]
            for c in range(s.shape[1] // w):
                acc_ref[...] = jnp.maximum(acc_ref[...], s[:, c * w:(c + 1) * w])

        for_far_blocks(lambda start, width: fold_max(logits(start, width)))

        @pl.when(i > 0)
        def _():
            fold_max(near_logits())

        @pl.when(i == 0)
        def _():
            fold_max(first_logits())

        accumulate(jnp.max(acc_ref[...], axis=1, keepdims=True))

    lv = lam_ref[...]
    lam = (jnp.exp(jnp.sum(lv[0:1] * lv[1:2], axis=1, keepdims=True))
           - jnp.exp(jnp.sum(lv[2:3] * lv[3:4], axis=1, keepdims=True)) + lam_init)
    acc = acc_ref[...]
    o = acc[:, :LANES] / acc[:, LANES:]
    o = o[:t] - lam * o[t:]
    o = o * lax.rsqrt(jnp.mean(o * o, axis=-1, keepdims=True) + EPS)
    o_ref[0] = (o * g_ref[...] * (1.0 - lam_init)).astype(BF16)


def _diff_call(lam_vecs, subln, bound, aq, ak, av, bias, t, lam_init):
    bsz, s, _ = aq.shape
    assert t % (2 * LANES) == 0
    return pl.pallas_call(
        functools.partial(_diff_kernel, t=t, lam_init=lam_init),
        grid=(bsz, A_HEADS, s // t),
        in_specs=[pl.BlockSpec(lam_vecs.shape, lambda b, h, i: (0, 0)),
                  pl.BlockSpec(subln.shape, lambda b, h, i: (0, 0)),
                  pl.BlockSpec((1, 8, LANES), lambda b, h, i: (0, 0, 0)),
                  pl.BlockSpec((1, t, A_VDIM), lambda b, h, i: (b, i, h)),
                  pl.BlockSpec((1, s, A_VDIM), lambda b, h, i: (b, 0, h)),
                  pl.BlockSpec((1, s, A_VDIM), lambda b, h, i: (b, 0, h)),
                  pl.BlockSpec((1, 2, t, t), lambda b, h, i: (h, 0, 0, 0))],
        out_specs=pl.BlockSpec((1, t, A_VDIM), lambda b, h, i: (b, i, h)),
        out_shape=jax.ShapeDtypeStruct((bsz, s, A_HEADS * A_VDIM), BF16),
        scratch_shapes=[pltpu.VMEM((2 * t, 2 * LANES), F32)],
        compiler_params=pltpu.CompilerParams(
            dimension_semantics=("arbitrary", "arbitrary", "arbitrary"),
            vmem_limit_bytes=VMEM_LIMIT),
        name="diff_attn",
    )(lam_vecs, subln, bound, aq, ak, av, bias)


def _sortable_key(x):
    b = pltpu.bitcast(x, I32)
    return b ^ ((b >> 31) & 0x7FFFFFFF)


def _dsa_kernel(bound_ref, iq_ref, iw_ref, ik_ref, q_ref, kv_ref, bias_ref, o_ref,
                keys_ref, iqz_ref, wb_ref, thr_ref, tie_ref, acc_ref, *, t, topk):
    i = pl.program_id(1)
    row = lax.broadcasted_iota(I32, (t, t), 0)
    col = lax.broadcasted_iota(I32, (t, t), 1)

    iq = iq_ref[0]
    iw = iw_ref[0]
    lane = lax.broadcasted_iota(I32, (t, LANES), 1)
    for h in range(IDX_HEADS):
        pair = iq[:, (h // 2) * LANES:(h // 2 + 1) * LANES]
        mine = (lane < IDX_DIM) if h % 2 == 0 else (lane >= IDX_DIM)
        iqz_ref[h] = jnp.where(mine, pair, jnp.zeros_like(pair))
        wb_ref[h] = jnp.broadcast_to(iw[:, h:h + 1], (t, LANES))

    def score_block(j, causal):
        kc = ik_ref[0, pl.ds(pl.multiple_of(j * t, t), t), :]
        acc = jnp.zeros((t, t), F32)
        for h in range(IDX_HEADS):
            wb = wb_ref[h]
            wb = jnp.concatenate([wb] * (t // LANES), axis=1)
            acc = acc + wb * jnp.maximum(_dot_nt(iqz_ref[h], kc), 0.0)
        if causal:
            acc = jnp.where(row >= col, acc, -jnp.inf)
        keys_ref[j] = _sortable_key(acc)

    def score_body(j, carry):
        score_block(j, False)
        return carry

    lax.fori_loop(0, i, score_body, 0)
    score_block(i, True)

    ngroups = t // SEL_ROWS

    def group_count(pred, c, r, nfull):
        def add_block(j, cnt, lane_groups):
            k = keys_ref[j, pl.ds(r * SEL_ROWS, SEL_ROWS), :]
            for w in range(lane_groups):
                cnt = cnt + jnp.where(pred(k[:, w * LANES:(w + 1) * LANES], c), 1.0, 0.0)
            return cnt

        cnt = jnp.zeros((SEL_ROWS, LANES), F32)
        if isinstance(nfull, int):
            for j in range(nfull):
                cnt = add_block(j, cnt, t // LANES)
        else:
            cnt = lax.fori_loop(0, nfull, lambda j, cnt: add_block(j, cnt, t // LANES), cnt)
        return add_block(nfull, cnt, -(-(r + 1) * SEL_ROWS // LANES))

    def row_total(cnt):
        return jnp.broadcast_to(jnp.sum(cnt, axis=1, keepdims=True), cnt.shape)

    def count(pred, ref_val, nfull):
        return jnp.concatenate(
            [row_total(group_count(pred, ref_val[r * SEL_ROWS:(r + 1) * SEL_ROWS], r, nfull))
             for r in range(ngroups)], axis=0)

    def ge(k, c):
        return k >= c

    def select(nfull):
        head = (ngroups - 1) * SEL_ROWS

        def resolve(thr_last, cand_last, pending):
            return jnp.where(row_total(pending) >= float(topk), cand_last, thr_last)

        def bit_body(it, carry):
            thr_head, thr_last, cand_last, pending = carry
            bit = jnp.left_shift(jnp.int32(1), 31 - it)
            thr_last = resolve(thr_last, cand_last, pending)
            cand_head = thr_head + bit
            cand_last = thr_last + bit
            keep = count_head(cand_head) >= float(topk)
            return (jnp.where(keep, cand_head, thr_head), thr_last, cand_last,
                    group_count(ge, cand_last, ngroups - 1, nfull))

        def count_head(cand_head):
            return jnp.concatenate(
                [row_total(group_count(ge, cand_head[r * SEL_ROWS:(r + 1) * SEL_ROWS], r, nfull))
                 for r in range(ngroups - 1)], axis=0)

        thr_ref[...] = jnp.full(thr_ref.shape, INT_MIN, I32)
        tie_ref[...] = jnp.zeros(tie_ref.shape, F32)
        start_last = thr_ref[head:, :LANES]
        thr_head, thr_last, cand_last, pending = lax.fori_loop(
            0, 32, bit_body, (thr_ref[:head, :LANES], start_last, start_last, tie_ref[head:]))
        thr = jnp.concatenate([thr_head, resolve(thr_last, cand_last, pending)], axis=0)
        excess = (count(lambda k, c: k >= c, thr, nfull) > float(topk)) & (thr > NEG_KEY)
        thr_ref[...] = jnp.concatenate([thr] * (t // LANES), axis=1)
        tie_ref[...] = jnp.where(excess, 1.0, 0.0)

    for n in range(keys_ref.shape[0]):
        pl.when(i == n)(functools.partial(select, n))

    @pl.when(jnp.max(tie_ref[...]) > 0.0)
    def _():
        tri = (row <= col).astype(BF16)
        thr2 = thr_ref[...]
        excess = tie_ref[...] > 0.0
        need = float(topk) - count(lambda k, c: k > c, thr2[:, :LANES], i)
        need2 = jnp.concatenate([need] * (t // LANES), axis=1)
        excess2 = jnp.concatenate([excess] * (t // LANES), axis=1)

        def body(j, seen):
            k = keys_ref[j]
            eq = k == thr2
            eqf = jnp.where(eq, 1.0, 0.0)
            rank = seen + _dot(eqf.astype(BF16), tri)
            keys_ref[j] = jnp.where(eq & (rank > need2) & excess2, INT_MIN, k)
            return seen + jnp.sum(eqf, axis=1, keepdims=True)

        lax.fori_loop(0, i + 1, body, jnp.zeros((t, 1), F32))

    thr_ref[...] = jnp.maximum(thr_ref[...], NEG_KEY + 1)

    q = q_ref[0]
    qs = []
    for g in range(B_KV_HEADS):
        parts = []
        for r in range(B_GROUP):
            hq = q[:, (g * B_GROUP + r) * HEAD_DIM:(g * B_GROUP + r + 1) * HEAD_DIM]
            z = jnp.zeros_like(hq)
            parts.append(jnp.concatenate([hq, z] if g == 0 else [z, hq], axis=1))
        qs.append(jnp.concatenate(parts, axis=0))

    def logits(j, g, near):
        k = kv_ref[0, pl.ds(pl.multiple_of(j * t, t), t), :LANES]
        madd = jnp.where(keys_ref[j] >= thr_ref[...], 0.0, MASKED)
        s = _dot_nt(qs[g], k).reshape(B_GROUP, t, t)
        if near is None:
            s = s + madd[None]
        else:
            s = s + (bias_ref[g * B_GROUP:(g + 1) * B_GROUP, near] + madd[None])
        return s.reshape(B_GROUP * t, t)

    def for_blocks(fn):
        def far(j, c):
            fn(j, None)
            return c

        lax.fori_loop(0, i - 1, far, 0)

        @pl.when(i > 0)
        def _():
            fn(i - 1, 1)

        fn(i, 0)

    def accumulate(ms):
        acc_ref[...] = jnp.zeros(acc_ref.shape, F32)

        def step(j, near):
            v = kv_ref[0, pl.ds(pl.multiple_of(j * t, t), t), LANES:]
            vx = jnp.concatenate([v, jnp.ones((t, LANES), BF16)], axis=1)
            for g in range(B_KV_HEADS):
                p = jnp.exp2(logits(j, g, near) - ms[g]).astype(BF16)
                acc_ref[g] += _dot(p, vx)

        for_blocks(step)

    m0 = bound_ref[0, 0:1, 0:1]
    accumulate([m0, m0])

    @pl.when(jnp.min(acc_ref[:, :, LANES:LANES + 1]) < L_TINY)
    def _():
        acc_ref[...] = jnp.full(acc_ref.shape, MASKED, F32)

        def step(j, near):
            w = acc_ref.shape[2]
            for g in range(B_KV_HEADS):
                s = logits(j, g, near)
                for c in range(t // w):
                    acc_ref[g] = jnp.maximum(acc_ref[g], s[:, c * w:(c + 1) * w])

        for_blocks(step)
        accumulate([jnp.max(acc_ref[g], axis=1, keepdims=True) for g in range(B_KV_HEADS)])

    outs = []
    for g in range(B_KV_HEADS):
        a = acc_ref[g]
        o = a[:, g * HEAD_DIM:(g + 1) * HEAD_DIM] / a[:, LANES:LANES + HEAD_DIM]
        for r in range(B_GROUP):
            outs.append(o[r * t:(r + 1) * t])
    o_ref[0] = jnp.concatenate(outs, axis=1).astype(BF16)


def _dsa_call(bound, iq, iw, ik, bq, bkv, bias, t, topk):
    bsz, s, _ = bq.shape
    nq = s // t
    assert t % SEL_ROWS == 0 and t % (2 * LANES) == 0
    return pl.pallas_call(
        functools.partial(_dsa_kernel, t=t, topk=topk),
        grid=(bsz, nq),
        in_specs=[pl.BlockSpec((1, 8, LANES), lambda b, i: (1, 0, 0)),
                  pl.BlockSpec((1, t, iq.shape[2]), lambda b, i: (b, i, 0)),
                  pl.BlockSpec((1, t, iw.shape[2]), lambda b, i: (b, i, 0)),
                  pl.BlockSpec((1, s, ik.shape[2]), lambda b, i: (b, 0, 0)),
                  pl.BlockSpec((1, t, bq.shape[2]), lambda b, i: (b, i, 0)),
                  pl.BlockSpec((1, s, bkv.shape[2]), lambda b, i: (b, 0, 0)),
                  pl.BlockSpec(bias.shape, lambda b, i: (0, 0, 0, 0),
                               pipeline_mode=pl.Buffered(1))],
        out_specs=pl.BlockSpec((1, t, B_HEADS * HEAD_DIM), lambda b, i: (b, i, 0)),
        out_shape=jax.ShapeDtypeStruct((bsz, s, B_HEADS * HEAD_DIM), BF16),
        scratch_shapes=[pltpu.VMEM((nq, t, t), I32),
                        pltpu.VMEM((IDX_HEADS, t, LANES), BF16),
                        pltpu.VMEM((IDX_HEADS, t, LANES), F32),
                        pltpu.VMEM((t, t), I32),
                        pltpu.VMEM((t, LANES), F32),
                        pltpu.VMEM((B_KV_HEADS, B_GROUP * t, 2 * LANES), F32)],
        compiler_params=pltpu.CompilerParams(dimension_semantics=("arbitrary", "arbitrary"),
                                             vmem_limit_bytes=VMEM_LIMIT),
        name="dsa_attn",
    )(bound, iq, iw, ik, bq, bkv, bias)


def _ffn_kernel(x_ref, oa_ref, ob_ref, xn_ref, oan_ref, obn_ref, mod_ref, g_ref, wo_ref, wgv_ref,
                cwg_ref, cwv_ref, cbg_ref, cbv_ref, wd_ref, o_ref, carry_ref, h_ref, y_ref, hn_ref,
                yn_ref, ua_ref, ub_ref, a_ref, *, nf):
    b = pl.program_id(0)
    i = pl.program_id(1)
    ts = x_ref.shape[1]
    mod = mod_ref[b]

    def residual_and_norm(xr, oar, obr, m, x1_out, h_out):
        x1 = xr[0] + m[2:3] * (_dot(oar[0], wo_ref[0]) + _dot(obr[0], wo_ref[1]))
        h_out[...] = _modulated_norm(x1, g_ref[...], m[4:5], m[3:4]).astype(BF16)
        x1_out[...] = x1

    @pl.when((b == 0) & (i == 0))
    def _():
        residual_and_norm(x_ref, oa_ref, ob_ref, mod, yn_ref, hn_ref)

    h_ref[...] = hn_ref[...]
    y_ref[...] = yn_ref[...]

    @pl.when(i == 0)
    def _():
        carry_ref[...] = jnp.zeros(carry_ref.shape, F32)

    def up(f, buf_ref):
        fc = a_ref.shape[2]
        dff = wgv_ref.shape[1] // 2
        h = h_ref[...]
        buf_ref[0, 8:8 + ts] = _dot(h, wgv_ref[:, f * fc:(f + 1) * fc])
        buf_ref[1, 8:8 + ts] = _dot(h, wgv_ref[:, dff + f * fc:dff + (f + 1) * fc])

    def conv(buf_ref, part, prev8, cw, cb):
        buf_ref[part, 0:8] = prev8
        return (cb + cw[2:3] * buf_ref[part, 8:8 + ts] + cw[1:2] * buf_ref[part, 7:7 + ts]
                + cw[0:1] * buf_ref[part, 6:6 + ts])

    def gate(f, buf_ref):
        yg = conv(buf_ref, 0, carry_ref[f, 0], cwg_ref[f], cbg_ref[f])
        yv = conv(buf_ref, 1, carry_ref[f, 1], cwv_ref[f], cbv_ref[f])
        carry_ref[f, 0] = buf_ref[0, ts:ts + 8]
        carry_ref[f, 1] = buf_ref[1, ts:ts + 8]
        a_ref[f] = ((yg / (1.0 + jnp.exp(-yg))) * yv).astype(BF16)

    def stage(f, cur_ref, nxt_ref):
        gate(f, cur_ref)
        up(f + 1, nxt_ref)

    up(0, ua_ref)

    assert nf % 2 == 1
    for k in range((nf - 1) // 2):
        stage(2 * k, ua_ref, ub_ref)
        stage(2 * k + 1, ub_ref, ua_ref)
    gate(nf - 1, ua_ref)
    nxt = b * pl.num_programs(1) + i + 1
    bn = jnp.minimum(nxt // pl.num_programs(1), pl.num_programs(0) - 1)
    residual_and_norm(xn_ref, oan_ref, obn_ref, mod_ref[bn], yn_ref, hn_ref)
    a = jnp.concatenate([a_ref[f] for f in range(nf)], axis=1)
    o_ref[0] = y_ref[...] + mod[5:6] * _dot(a, wd_ref[...])


def _ffn_call(x, oa, ob, mod, g_ffn, w_out, w_up, conv_w, conv_b, w_down, ts, fc):
    bsz, s, d = x.shape
    dff = w_down.shape[0]
    nf = dff // fc
    na = oa.shape[2]
    wo = w_out.astype(BF16).reshape(2, na, d)
    wgv = w_up.astype(BF16)
    cwg = conv_w[:, :dff].reshape(CONV_W, nf, fc).transpose(1, 0, 2)
    cwv = conv_w[:, dff:].reshape(CONV_W, nf, fc).transpose(1, 0, 2)
    cbg = conv_b[:dff].reshape(nf, 1, fc)
    cbv = conv_b[dff:].reshape(nf, 1, fc)
    wd = w_down.astype(BF16)

    def full(a):
        return pl.BlockSpec(a.shape, lambda b, i: (0,) * a.ndim, pipeline_mode=pl.Buffered(1))

    nt = s // ts

    def tok(n):
        return pl.BlockSpec((1, ts, n), lambda b, i: (b, i, 0))

    def tok_next(n):
        def index(b, i):
            g = jnp.minimum(b * nt + i + 1, bsz * nt - 1)
            return (g // nt, g % nt, 0)
        return pl.BlockSpec((1, ts, n), index)

    def first(n):
        return pl.BlockSpec((1, ts, n), lambda b, i: (0, 0, 0))

    consts = [wo, wgv, cwg, cwv, cbg, cbv, wd]
    return pl.pallas_call(
        functools.partial(_ffn_kernel, nf=nf),
        grid=(bsz, nt),
        in_specs=[first(d), first(na), first(ob.shape[2]), tok_next(d), tok_next(na),
                  tok_next(ob.shape[2]), full(mod), full(g_ffn)]
                 + [full(a) for a in consts],
        out_specs=tok(d),
        out_shape=jax.ShapeDtypeStruct((bsz, s, d), F32),
        scratch_shapes=[pltpu.VMEM((nf, 2, 8, fc), F32), pltpu.VMEM((ts, d), BF16),
                        pltpu.VMEM((ts, d), F32), pltpu.VMEM((ts, d), BF16),
                        pltpu.VMEM((ts, d), F32), pltpu.VMEM((2, ts + 8, fc), F32),
                        pltpu.VMEM((2, ts + 8, fc), F32), pltpu.VMEM((nf, ts, fc), BF16)],
        compiler_params=pltpu.CompilerParams(dimension_semantics=("arbitrary", "arbitrary"),
                                             vmem_limit_bytes=VMEM_LIMIT),
        name="outproj_ffn",
    )(x, oa, ob, x, oa, ob, mod, g_ffn, *consts)


def kernel(x, c, w_ada, b_ada, g_attn, w_in, q_norm_a, k_norm_a, q_norm_b, k_norm_b, lam_vecs,
           subln_a, w_out, g_ffn, w_up, conv_w, conv_b, w_down, rel_bias):
    bsz, s, d = x.shape
    t = min(ATT_T, s)
    ta = min(DIFF_T, s)
    ts = min(512, s)
    topk = min(TOPK_MAX, s // 4)
    scale = HEAD_DIM ** -0.5
    for l in range(w_ada.shape[0]):
        mod = _ada_call(c, w_ada[l], b_ada[l]).reshape(bsz, 6, d)
        gains = jnp.stack([q_norm_a[l], k_norm_a[l], q_norm_b[l], k_norm_b[l]])
        bias_a, bound = _bias_call(rel_bias, gains, ta, 0, A_HEADS)
        bias_b, _ = _bias_call(rel_bias, gains, t, A_HEADS, B_HEADS)
        gqa = (jnp.tile(q_norm_a[l], 2 * A_HEADS) * (scale * LOG2E))[None]
        gka = jnp.tile(k_norm_a[l], 2 * A_HEADS)[None]
        gqb = (jnp.tile(q_norm_b[l], B_HEADS) * (scale * LOG2E))[None]
        gkb = jnp.tile(k_norm_b[l], B_KV_HEADS)[None]
        aq, ak, av, bq, bkv, iq, ik, iw = _inproj_call(
            x, mod, g_attn[l][None], w_in[l], gqa, gka, gqb, gkb, ts)
        lam_init = 0.8 - 0.6 * math.exp(-0.3 * l)
        o_a = _diff_call(lam_vecs[l], subln_a[l][None], bound, aq, ak, av, bias_a, ta, lam_init)
        o_b = _dsa_call(bound, iq, iw, ik, bq, bkv, bias_b, t, topk)
        x = _ffn_call(x, o_a, o_b, mod, g_ffn[l][None], w_out[l], w_up[l], conv_w[l], conv_b[l],
                      w_down[l], ts, 256)
    return x
```

```python
import functools
import math

import jax
import jax.numpy as jnp
from jax import lax
from jax.experimental import pallas as pl
from jax.experimental.pallas import tpu as pltpu

F32 = jnp.float32
BF16 = jnp.bfloat16
I32 = jnp.int32

HEAD_DIM = 64
A_HEADS = 4
A_VDIM = 2 * HEAD_DIM
B_HEADS = 8
B_KV_HEADS = 2
B_GROUP = B_HEADS // B_KV_HEADS
IDX_HEADS = 8
IDX_DIM = 64
TOPK_MAX = 256
N_BUCKETS = 32
MAX_DISTANCE = 128
CONV_W = 3
EPS = 1e-6

LANES = 128
ATT_T = 512
SEL_ROWS = 128
DIFF_T = 512
FAR_BLOCKS = 2
MASKED = -1e30
LOG2E = math.log2(math.e)
L_TINY = 2.0 ** -60
NEG_KEY = -2139095041
INT_MIN = -2147483648
VMEM_LIMIT = 56 * 1024 * 1024


def _dot(a, b):
    return jnp.dot(a, b, preferred_element_type=F32)


def _dot_nt(a, b):
    return lax.dot_general(a, b, (((1,), (1,)), ((), ())), preferred_element_type=F32)


def _split_bf16(a):
    hi = a.astype(BF16)
    lo = (a - hi.astype(F32)).astype(BF16)
    return hi, lo


def _ada_kernel(c_ref, w_ref, b_ref, o_ref):
    c = c_ref[...]
    a = c / (1.0 + jnp.exp(-c))
    a_hi, a_lo = _split_bf16(a)
    w_hi, w_lo = _split_bf16(w_ref[...])
    o_ref[...] = _dot(a_hi, w_hi) + _dot(a_hi, w_lo) + _dot(a_lo, w_hi) + b_ref[...]


def _ada_call(c, w, b):
    bsz, d = c.shape
    n = w.shape[1]
    tn = 1024
    return pl.pallas_call(
        _ada_kernel,
        grid=(n // tn,),
        in_specs=[pl.BlockSpec((bsz, d), lambda j: (0, 0)),
                  pl.BlockSpec((d, tn), lambda j: (0, j)),
                  pl.BlockSpec((1, tn), lambda j: (0, j))],
        out_specs=pl.BlockSpec((bsz, tn), lambda j: (0, j)),
        out_shape=jax.ShapeDtypeStruct((bsz, n), F32),
        compiler_params=pltpu.CompilerParams(dimension_semantics=("arbitrary",),
                                             vmem_limit_bytes=VMEM_LIMIT),
        name="ada_mod",
    )(c, w, b.reshape(1, n))


def _bias_kernel(tab_ref, tabv_ref, gains_ref, o_ref, bound_ref, *, t, h0):
    h = pl.program_id(0) + h0
    max_exact = N_BUCKETS // 2
    b = MAX_DISTANCE
    row = lax.broadcasted_iota(I32, (b, b), 0)
    col = lax.broadcasted_iota(I32, (b, b), 1)
    last = tab_ref[N_BUCKETS - 1, h]
    base = []
    for d in range(2):
        n = jnp.maximum(row - col + d * b, 0)
        nf = jnp.maximum(n, 1).astype(F32)
        large = max_exact + (jnp.log(nf / max_exact) / math.log(MAX_DISTANCE / max_exact)
                             * (N_BUCKETS - max_exact)).astype(I32)
        large = jnp.minimum(large, N_BUCKETS - 1)
        bucket = jnp.where(n < max_exact, n, large)
        acc = jnp.zeros((b, b), F32)
        for k in range(N_BUCKETS - 1):
            acc = jnp.where(bucket == k, (tab_ref[k, h] - last) * LOG2E, acc)
        base.append(acc)
    by_distance = {0: jnp.where(row >= col, base[0], MASKED), 1: base[1]}
    masked = jnp.full((b, b), MASKED, F32)
    zero = jnp.zeros((b, b), F32)
    nb = t // b
    for d in range(2):
        o_ref[0, d] = jnp.concatenate(
            [jnp.concatenate([masked if d * nb + a - c < 0 else by_distance.get(d * nb + a - c, zero)
                              for c in range(nb)], axis=1) for a in range(nb)], axis=0)
    tab = tabv_ref[...]
    rel = (tab - tab[N_BUCKETS - 1:N_BUCKETS, :]) * LOG2E
    hcol = lax.broadcasted_iota(I32, rel.shape, 1)
    g = jnp.max(jnp.abs(gains_ref[...]), axis=1, keepdims=True)
    for grp in range(2):
        in_grp = (hcol < A_HEADS) if grp == 0 else (hcol >= A_HEADS)
        bmax = jnp.max(jnp.where(in_grp, rel, 0.0), keepdims=True)
        qk = g[2 * grp:2 * grp + 1] * g[2 * grp + 1:2 * grp + 2] * (HEAD_DIM ** 0.5 * LOG2E)
        bound_ref[grp] = jnp.broadcast_to(qk + bmax, bound_ref.shape[1:])


def _bias_call(rel_bias, gains, t, h0, nh):
    return pl.pallas_call(
        functools.partial(_bias_kernel, t=t, h0=h0),
        grid=(nh,),
        in_specs=[pl.BlockSpec(memory_space=pltpu.SMEM),
                  pl.BlockSpec(rel_bias.shape, lambda h: (0, 0)),
                  pl.BlockSpec(gains.shape, lambda h: (0, 0))],
        out_specs=[pl.BlockSpec((1, 2, t, t), lambda h: (h, 0, 0, 0)),
                   pl.BlockSpec((2, 8, LANES), lambda h: (0, 0, 0))],
        out_shape=[jax.ShapeDtypeStruct((nh, 2, t, t), F32),
                   jax.ShapeDtypeStruct((2, 8, LANES), F32)],
        compiler_params=pltpu.CompilerParams(dimension_semantics=("arbitrary",)),
        name="bias_tiles",
    )(rel_bias, rel_bias, gains)


def _modulated_norm(x, g, scale, shift):
    y = x * lax.rsqrt(jnp.mean(x * x, axis=-1, keepdims=True) + EPS)
    return (y * g) * (1.0 + scale) + shift


def _head_norm(y, gain, bd):
    y2 = (y * y).astype(BF16)
    w = bd.shape[0]
    ss = jnp.concatenate([_dot(y2[:, c:c + w], bd) for c in range(0, y.shape[1], w)], axis=1)
    return y * lax.rsqrt(ss * (1.0 / HEAD_DIM) + EPS) * gain


def _inproj_kernel(x_ref, mod_ref, g_ref, waq_ref, wak_ref, wav_ref, wbq_ref, wbkv_ref, wiq_ref,
                   wik_ref, gqa_ref, gka_ref, gqb_ref, gkb_ref, bd_ref,
                   aq_ref, ak_ref, av_ref, bq_ref, bkv_ref, iq_ref, ik_ref, iw_ref):
    x = x_ref[0]
    h = _modulated_norm(x, g_ref[...], mod_ref[0, 1:2, :], mod_ref[0, 0:1, :]).astype(BF16)
    bd = bd_ref[...]
    aq_ref[0] = _head_norm(_dot(h, waq_ref[...]), gqa_ref[...], bd).astype(BF16)
    ak_ref[0] = _head_norm(_dot(h, wak_ref[...]), gka_ref[...], bd).astype(BF16)
    av_ref[0] = _dot(h, wav_ref[...]).astype(BF16)
    bq_ref[0] = _head_norm(_dot(h, wbq_ref[...]), gqb_ref[...], bd).astype(BF16)
    kv = _dot(h, wbkv_ref[...])
    nk = B_KV_HEADS * HEAD_DIM
    bk = _head_norm(kv[:, :nk], gkb_ref[...], bd[:nk, :nk])
    bkv_ref[0] = jnp.concatenate([bk, kv[:, nk:]], axis=1).astype(BF16)
    iq_ref[0] = _dot(h, wiq_ref[...]).astype(BF16)
    ikw = _dot(h, wik_ref[...])
    ik_ref[0] = ikw[:, :2 * IDX_DIM].astype(BF16)
    iw_ref[0] = ikw[:, 2 * IDX_DIM:2 * IDX_DIM + IDX_HEADS] * ((IDX_HEADS * IDX_DIM) ** -0.5)


def _inproj_call(x, mod, g_attn, w_in, gqa, gka, gqb, gkb, ts):
    bsz, s, d = x.shape
    na = A_HEADS * 2 * HEAD_DIM
    nb = B_HEADS * HEAD_DIM
    nkv = B_KV_HEADS * HEAD_DIM
    ni = IDX_HEADS * IDX_DIM
    o = 0
    parts = []
    for width in (na, na, A_HEADS * A_VDIM, nb, 2 * nkv, ni, IDX_DIM + IDX_HEADS):
        parts.append(w_in[:, o:o + width].astype(BF16))
        o += width
    wik = jnp.concatenate([parts[-1][:, :IDX_DIM], parts[-1]], axis=1)
    parts[-1] = jnp.pad(wik, ((0, 0), (0, 2 * LANES - wik.shape[1])))
    grp = jnp.arange(2 * LANES) // HEAD_DIM
    bd = (grp[:, None] == grp[None, :]).astype(BF16)

    def full(a):
        return pl.BlockSpec(a.shape, lambda b, i: (0,) * a.ndim)

    def tok(n):
        return pl.BlockSpec((1, ts, n), lambda b, i: (b, i, 0))

    consts = parts + [gqa, gka, gqb, gkb, bd]
    out_w = [(na, BF16), (na, BF16), (A_HEADS * A_VDIM, BF16), (nb, BF16), (2 * nkv, BF16),
             (ni, BF16), (2 * IDX_DIM, BF16), (IDX_HEADS, F32)]
    return pl.pallas_call(
        _inproj_kernel,
        grid=(bsz, s // ts),
        in_specs=[tok(d), pl.BlockSpec((1, 6, d), lambda b, i: (b, 0, 0)), full(g_attn)]
                 + [full(a) for a in consts],
        out_specs=[tok(n) for n, _ in out_w],
        out_shape=[jax.ShapeDtypeStruct((bsz, s, n), dt) for n, dt in out_w],
        compiler_params=pltpu.CompilerParams(dimension_semantics=("arbitrary", "arbitrary"),
                                             vmem_limit_bytes=VMEM_LIMIT),
        name="in_proj",
    )(x, mod, g_attn, *consts)


def _diff_kernel(lam_ref, g_ref, bound_ref, q_ref, k_ref, v_ref, bias_ref, o_ref, acc_ref, *,
                 t, lam_init):
    i = pl.program_id(2)
    q = q_ref[0]
    lane = lax.broadcasted_iota(I32, q.shape, 1)
    zero = jnp.zeros_like(q)
    qq = jnp.concatenate([jnp.where(lane < HEAD_DIM, q, zero),
                          jnp.where(lane >= HEAD_DIM, q, zero)], axis=0)

    def logits(start, width):
        k = k_ref[0, pl.ds(pl.multiple_of(start, t), width), :]
        return _dot_nt(qq, k)

    def v_ones(start, width):
        v = v_ref[0, pl.ds(pl.multiple_of(start, t), width), :]
        return jnp.concatenate([v, jnp.ones((width, LANES), BF16)], axis=1)

    def near_logits():
        b = jnp.concatenate([bias_ref[0, 1], bias_ref[0, 0]], axis=1)
        return logits((i - 1) * t, 2 * t) + jnp.concatenate([b, b], axis=0)

    def first_logits():
        b = bias_ref[0, 0]
        return logits(0, t) + jnp.concatenate([b, b], axis=0)

    def for_far_blocks(fn):
        nfar = jnp.maximum(i - 1, 0)
        nsup = nfar // FAR_BLOCKS

        def sup(j, c):
            fn(j * (FAR_BLOCKS * t), FAR_BLOCKS * t)
            return c

        def single(j, c):
            fn(j * t, t)
            return c

        lax.fori_loop(0, nsup, sup, 0)
        lax.fori_loop(nsup * FAR_BLOCKS, nfar, single, 0)

    def accumulate(m):
        @pl.when(i > 0)
        def _():
            p = jnp.exp2(near_logits() - m).astype(BF16)
            acc_ref[...] = _dot(p, v_ones((i - 1) * t, 2 * t))

        @pl.when(i == 0)
        def _():
            p = jnp.exp2(first_logits() - m).astype(BF16)
            acc_ref[...] = _dot(p, v_ones(0, t))

        def far(start, width):
            p = jnp.exp2(logits(start, width) - m).astype(BF16)
            acc_ref[...] += _dot(p, v_ones(start, width))

        for_far_blocks(far)

    accumulate(bound_ref[0, 0:1, 0:1])

    @pl.when(jnp.min(acc_ref[:, LANES:LANES + 1]) < L_TINY)
    def _():
        acc_ref[...] = jnp.full(acc_ref.shape, MASKED, F32)

        def fold_max(s):
            w = acc_ref.shape[1]
            for c in range(s.shape[1] // w):
                acc_ref[...] = jnp.maximum(acc_ref[...], s[:, c * w:(c + 1) * w])

        for_far_blocks(lambda start, width: fold_max(logits(start, width)))

        @pl.when(i > 0)
        def _():
            fold_max(near_logits())

        @pl.when(i == 0)
        def _():
            fold_max(first_logits())

        accumulate(jnp.max(acc_ref[...], axis=1, keepdims=True))

    lv = lam_ref[...]
    lam = (jnp.exp(jnp.sum(lv[0:1] * lv[1:2], axis=1, keepdims=True))
           - jnp.exp(jnp.sum(lv[2:3] * lv[3:4], axis=1, keepdims=True)) + lam_init)
    acc = acc_ref[...]
    o = acc[:, :LANES] / acc[:, LANES:]
    o = o[:t] - lam * o[t:]
    o = o * lax.rsqrt(jnp.mean(o * o, axis=-1, keepdims=True) + EPS)
    o_ref[0] = (o * g_ref[...] * (1.0 - lam_init)).astype(BF16)


def _diff_call(lam_vecs, subln, bound, aq, ak, av, bias, t, lam_init):
    bsz, s, _ = aq.shape
    assert t % (2 * LANES) == 0
    return pl.pallas_call(
        functools.partial(_diff_kernel, t=t, lam_init=lam_init),
        grid=(bsz, A_HEADS, s // t),
        in_specs=[pl.BlockSpec(lam_vecs.shape, lambda b, h, i: (0, 0)),
                  pl.BlockSpec(subln.shape, lambda b, h, i: (0, 0)),
                  pl.BlockSpec((1, 8, LANES), lambda b, h, i: (0, 0, 0)),
                  pl.BlockSpec((1, t, A_VDIM), lambda b, h, i: (b, i, h)),
                  pl.BlockSpec((1, s, A_VDIM), lambda b, h, i: (b, 0, h)),
                  pl.BlockSpec((1, s, A_VDIM), lambda b, h, i: (b, 0, h)),
                  pl.BlockSpec((1, 2, t, t), lambda b, h, i: (h, 0, 0, 0))],
        out_specs=pl.BlockSpec((1, t, A_VDIM), lambda b, h, i: (b, i, h)),
        out_shape=jax.ShapeDtypeStruct((bsz, s, A_HEADS * A_VDIM), BF16),
        scratch_shapes=[pltpu.VMEM((2 * t, 2 * LANES), F32)],
        compiler_params=pltpu.CompilerParams(
            dimension_semantics=("arbitrary", "arbitrary", "arbitrary"),
            vmem_limit_bytes=VMEM_LIMIT),
        name="diff_attn",
    )(lam_vecs, subln, bound, aq, ak, av, bias)


def _sortable_key(x):
    b = pltpu.bitcast(x, I32)
    return b ^ ((b >> 31) & 0x7FFFFFFF)


def _dsa_kernel(bound_ref, iq_ref, iw_ref, ik_ref, q_ref, kv_ref, bias_ref, o_ref,
                keys_ref, iqz_ref, wb_ref, thr_ref, tie_ref, acc_ref, *, t, topk):
    i = pl.program_id(1)
    row = lax.broadcasted_iota(I32, (t, t), 0)
    col = lax.broadcasted_iota(I32, (t, t), 1)

    iq = iq_ref[0]
    iw = iw_ref[0]
    lane = lax.broadcasted_iota(I32, (t, LANES), 1)
    for h in range(IDX_HEADS):
        pair = iq[:, (h // 2) * LANES:(h // 2 + 1) * LANES]
        mine = (lane < IDX_DIM) if h % 2 == 0 else (lane >= IDX_DIM)
        iqz_ref[h] = jnp.where(mine, pair, jnp.zeros_like(pair))
        wb_ref[h] = jnp.broadcast_to(iw[:, h:h + 1], (t, LANES))

    def score_block(j, causal):
        kc = ik_ref[0, pl.ds(pl.multiple_of(j * t, t), t), :]
        acc = jnp.zeros((t, t), F32)
        for h in range(IDX_HEADS):
            wb = wb_ref[h]
            wb = jnp.concatenate([wb] * (t // LANES), axis=1)
            acc = acc + wb * jnp.maximum(_dot_nt(iqz_ref[h], kc), 0.0)
        if causal:
            acc = jnp.where(row >= col, acc, -jnp.inf)
        keys_ref[j] = _sortable_key(acc)

    def score_body(j, carry):
        score_block(j, False)
        return carry

    lax.fori_loop(0, i, score_body, 0)
    score_block(i, True)

    ngroups = t // SEL_ROWS

    def group_count(pred, c, r, nfull):
        def add_block(j, cnt, lane_groups):
            k = keys_ref[j, pl.ds(r * SEL_ROWS, SEL_ROWS), :]
            for w in range(lane_groups):
                cnt = cnt + jnp.where(pred(k[:, w * LANES:(w + 1) * LANES], c), 1.0, 0.0)
            return cnt

        cnt = jnp.zeros((SEL_ROWS, LANES), F32)
        if isinstance(nfull, int):
            for j in range(nfull):
                cnt = add_block(j, cnt, t // LANES)
        else:
            cnt = lax.fori_loop(0, nfull, lambda j, cnt: add_block(j, cnt, t // LANES), cnt)
        return add_block(nfull, cnt, -(-(r + 1) * SEL_ROWS // LANES))

    def row_total(cnt):
        return jnp.broadcast_to(jnp.sum(cnt, axis=1, keepdims=True), cnt.shape)

    def count(pred, ref_val, nfull):
        return jnp.concatenate(
            [row_total(group_count(pred, ref_val[r * SEL_ROWS:(r + 1) * SEL_ROWS], r, nfull))
             for r in range(ngroups)], axis=0)

    def ge(k, c):
        return k >= c

    def select(nfull):
        head = (ngroups - 1) * SEL_ROWS

        def resolve(thr_last, cand_last, pending):
            return jnp.where(row_total(pending) >= float(topk), cand_last, thr_last)

        def bit_body(it, carry):
            thr_head, thr_last, cand_last, pending = carry
            bit = jnp.left_shift(jnp.int32(1), 31 - it)
            thr_last = resolve(thr_last, cand_last, pending)
            cand_head = thr_head + bit
            cand_last = thr_last + bit
            keep = count_head(cand_head) >= float(topk)
            return (jnp.where(keep, cand_head, thr_head), thr_last, cand_last,
                    group_count(ge, cand_last, ngroups - 1, nfull))

        def count_head(cand_head):
            return jnp.concatenate(
                [row_total(group_count(ge, cand_head[r * SEL_ROWS:(r + 1) * SEL_ROWS], r, nfull))
                 for r in range(ngroups - 1)], axis=0)

        thr_ref[...] = jnp.full(thr_ref.shape, INT_MIN, I32)
        tie_ref[...] = jnp.zeros(tie_ref.shape, F32)
        start_last = thr_ref[head:, :LANES]
        thr_head, thr_last, cand_last, pending = lax.fori_loop(
            0, 32, bit_body, (thr_ref[:head, :LANES], start_last, start_last, tie_ref[head:]))
        thr = jnp.concatenate([thr_head, resolve(thr_last, cand_last, pending)], axis=0)
        excess = (count(lambda k, c: k >= c, thr, nfull) > float(topk)) & (thr > NEG_KEY)
        thr_ref[...] = jnp.concatenate([thr] * (t // LANES), axis=1)
        tie_ref[...] = jnp.where(excess, 1.0, 0.0)

    for n in range(keys_ref.shape[0]):
        pl.when(i == n)(functools.partial(select, n))

    @pl.when(jnp.max(tie_ref[...]) > 0.0)
    def _():
        tri = (row <= col).astype(BF16)
        thr2 = thr_ref[...]
        excess = tie_ref[...] > 0.0
        need = float(topk) - count(lambda k, c: k > c, thr2[:, :LANES], i)
        need2 = jnp.concatenate([need] * (t // LANES), axis=1)
        excess2 = jnp.concatenate([excess] * (t // LANES), axis=1)

        def body(j, seen):
            k = keys_ref[j]
            eq = k == thr2
            eqf = jnp.where(eq, 1.0, 0.0)
            rank = seen + _dot(eqf.astype(BF16), tri)
            keys_ref[j] = jnp.where(eq & (rank > need2) & excess2, INT_MIN, k)
            return seen + jnp.sum(eqf, axis=1, keepdims=True)

        lax.fori_loop(0, i + 1, body, jnp.zeros((t, 1), F32))

    thr_ref[...] = jnp.maximum(thr_ref[...], NEG_KEY + 1)

    q = q_ref[0]
    qs = []
    for g in range(B_KV_HEADS):
        parts = []
        for r in range(B_GROUP):
            hq = q[:, (g * B_GROUP + r) * HEAD_DIM:(g * B_GROUP + r + 1) * HEAD_DIM]
            z = jnp.zeros_like(hq)
            parts.append(jnp.concatenate([hq, z] if g == 0 else [z, hq], axis=1))
        qs.append(jnp.concatenate(parts, axis=0))

    def logits(j, g, near):
        k = kv_ref[0, pl.ds(pl.multiple_of(j * t, t), t), :LANES]
        madd = jnp.where(keys_ref[j] >= thr_ref[...], 0.0, MASKED)
        s = _dot_nt(qs[g], k).reshape(B_GROUP, t, t)
        if near is None:
            s = s + madd[None]
        else:
            s = s + (bias_ref[g * B_GROUP:(g + 1) * B_GROUP, near] + madd[None])
        return s.reshape(B_GROUP * t, t)

    def for_blocks(fn):
        fn(i, 0)

        def far(j, c):
            fn(j, None)
            return c

        lax.fori_loop(0, i - 1, far, 0)

        @pl.when(i > 0)
        def _():
            fn(i - 1, 1)

    def accumulate(ms):
        def step(j, near):
            v = kv_ref[0, pl.ds(pl.multiple_of(j * t, t), t), LANES:]
            vx = jnp.concatenate([v, jnp.ones((t, LANES), BF16)], axis=1)
            for g in range(B_KV_HEADS):
                p = jnp.exp2(logits(j, g, near) - ms[g]).astype(BF16)
                if near == 0:
                    acc_ref[g] = _dot(p, vx)
                else:
                    acc_ref[g] += _dot(p, vx)

        for_blocks(step)

    m0 = bound_ref[0, 0:1, 0:1]
    accumulate([m0, m0])

    @pl.when(jnp.min(acc_ref[:, :, LANES:LANES + 1]) < L_TINY)
    def _():
        acc_ref[...] = jnp.full(acc_ref.shape, MASKED, F32)

        def step(j, near):
            w = acc_ref.shape[2]
            for g in range(B_KV_HEADS):
                s = logits(j, g, near)
                for c in range(t // w):
                    acc_ref[g] = jnp.maximum(acc_ref[g], s[:, c * w:(c + 1) * w])

        for_blocks(step)
        accumulate([jnp.max(acc_ref[g], axis=1, keepdims=True) for g in range(B_KV_HEADS)])

    outs = []
    for g in range(B_KV_HEADS):
        a = acc_ref[g]
        o = a[:, g * HEAD_DIM:(g + 1) * HEAD_DIM] / a[:, LANES:LANES + HEAD_DIM]
        for r in range(B_GROUP):
            outs.append(o[r * t:(r + 1) * t])
    o_ref[0] = jnp.concatenate(outs, axis=1).astype(BF16)


def _dsa_call(bound, iq, iw, ik, bq, bkv, bias, t, topk):
    bsz, s, _ = bq.shape
    nq = s // t
    assert t % SEL_ROWS == 0 and t % (2 * LANES) == 0
    return pl.pallas_call(
        functools.partial(_dsa_kernel, t=t, topk=topk),
        grid=(bsz, nq),
        in_specs=[pl.BlockSpec((1, 8, LANES), lambda b, i: (1, 0, 0)),
                  pl.BlockSpec((1, t, iq.shape[2]), lambda b, i: (b, i, 0)),
                  pl.BlockSpec((1, t, iw.shape[2]), lambda b, i: (b, i, 0)),
                  pl.BlockSpec((1, s, ik.shape[2]), lambda b, i: (b, 0, 0)),
                  pl.BlockSpec((1, t, bq.shape[2]), lambda b, i: (b, i, 0)),
                  pl.BlockSpec((1, s, bkv.shape[2]), lambda b, i: (b, 0, 0)),
                  pl.BlockSpec(bias.shape, lambda b, i: (0, 0, 0, 0),
                               pipeline_mode=pl.Buffered(1))],
        out_specs=pl.BlockSpec((1, t, B_HEADS * HEAD_DIM), lambda b, i: (b, i, 0)),
        out_shape=jax.ShapeDtypeStruct((bsz, s, B_HEADS * HEAD_DIM), BF16),
        scratch_shapes=[pltpu.VMEM((nq, t, t), I32),
                        pltpu.VMEM((IDX_HEADS, t, LANES), BF16),
                        pltpu.VMEM((IDX_HEADS, t, LANES), F32),
                        pltpu.VMEM((t, t), I32),
                        pltpu.VMEM((t, LANES), F32),
                        pltpu.VMEM((B_KV_HEADS, B_GROUP * t, 2 * LANES), F32)],
        compiler_params=pltpu.CompilerParams(dimension_semantics=("arbitrary", "arbitrary"),
                                             vmem_limit_bytes=VMEM_LIMIT),
        name="dsa_attn",
    )(bound, iq, iw, ik, bq, bkv, bias)


def _ffn_kernel(x_ref, oa_ref, ob_ref, xn_ref, oan_ref, obn_ref, mod_ref, g_ref, wo_ref, wgv_ref,
                cwg_ref, cwv_ref, cbg_ref, cbv_ref, wd_ref, o_ref, carry_ref, h_ref, y_ref, hn_ref,
                yn_ref, ua_ref, ub_ref, a_ref, *, nf):
    b = pl.program_id(0)
    i = pl.program_id(1)
    ts = x_ref.shape[1]
    mod = mod_ref[b]

    def residual_and_norm(xr, oar, obr, m, x1_out, h_out):
        x1 = xr[0] + m[2:3] * (_dot(oar[0], wo_ref[0]) + _dot(obr[0], wo_ref[1]))
        h_out[...] = _modulated_norm(x1, g_ref[...], m[4:5], m[3:4]).astype(BF16)
        x1_out[...] = x1

    @pl.when((b == 0) & (i == 0))
    def _():
        residual_and_norm(x_ref, oa_ref, ob_ref, mod, yn_ref, hn_ref)

    h_ref[...] = hn_ref[...]
    y_ref[...] = yn_ref[...]

    @pl.when(i == 0)
    def _():
        carry_ref[...] = jnp.zeros(carry_ref.shape, F32)

    def up(f, buf_ref):
        fc = a_ref.shape[2]
        dff = wgv_ref.shape[1] // 2
        h = h_ref[...]
        buf_ref[0, 8:8 + ts] = _dot(h, wgv_ref[:, f * fc:(f + 1) * fc])
        buf_ref[1, 8:8 + ts] = _dot(h, wgv_ref[:, dff + f * fc:dff + (f + 1) * fc])

    def conv(buf_ref, part, prev8, cw, cb):
        buf_ref[part, 0:8] = prev8
        return (cb + cw[2:3] * buf_ref[part, 8:8 + ts] + cw[1:2] * buf_ref[part, 7:7 + ts]
                + cw[0:1] * buf_ref[part, 6:6 + ts])

    def gate(f, buf_ref):
        yg = conv(buf_ref, 0, carry_ref[f, 0], cwg_ref[f], cbg_ref[f])
        yv = conv(buf_ref, 1, carry_ref[f, 1], cwv_ref[f], cbv_ref[f])
        carry_ref[f, 0] = buf_ref[0, ts:ts + 8]
        carry_ref[f, 1] = buf_ref[1, ts:ts + 8]
        a_ref[f] = ((yg / (1.0 + jnp.exp(-yg))) * yv).astype(BF16)

    def stage(f, cur_ref, nxt_ref):
        gate(f, cur_ref)
        up(f + 1, nxt_ref)

    up(0, ua_ref)

    assert nf % 2 == 1
    for k in range((nf - 1) // 2):
        stage(2 * k, ua_ref, ub_ref)
        stage(2 * k + 1, ub_ref, ua_ref)
    gate(nf - 1, ua_ref)
    nxt = b * pl.num_programs(1) + i + 1
    bn = jnp.minimum(nxt // pl.num_programs(1), pl.num_programs(0) - 1)
    residual_and_norm(xn_ref, oan_ref, obn_ref, mod_ref[bn], yn_ref, hn_ref)
    a = jnp.concatenate([a_ref[f] for f in range(nf)], axis=1)
    o_ref[0] = y_ref[...] + mod[5:6] * _dot(a, wd_ref[...])


def _ffn_call(x, oa, ob, mod, g_ffn, w_out, w_up, conv_w, conv_b, w_down, ts, fc):
    bsz, s, d = x.shape
    dff = w_down.shape[0]
    nf = dff // fc
    na = oa.shape[2]
    wo = w_out.astype(BF16).reshape(2, na, d)
    wgv = w_up.astype(BF16)
    cwg = conv_w[:, :dff].reshape(CONV_W, nf, fc).transpose(1, 0, 2)
    cwv = conv_w[:, dff:].reshape(CONV_W, nf, fc).transpose(1, 0, 2)
    cbg = conv_b[:dff].reshape(nf, 1, fc)
    cbv = conv_b[dff:].reshape(nf, 1, fc)
    wd = w_down.astype(BF16)

    def full(a):
        return pl.BlockSpec(a.shape, lambda b, i: (0,) * a.ndim, pipeline_mode=pl.Buffered(1))

    nt = s // ts

    def tok(n):
        return pl.BlockSpec((1, ts, n), lambda b, i: (b, i, 0))

    def tok_next(n):
        def index(b, i):
            g = jnp.minimum(b * nt + i + 1, bsz * nt - 1)
            return (g // nt, g % nt, 0)
        return pl.BlockSpec((1, ts, n), index)

    def first(n):
        return pl.BlockSpec((1, ts, n), lambda b, i: (0, 0, 0))

    consts = [wo, wgv, cwg, cwv, cbg, cbv, wd]
    return pl.pallas_call(
        functools.partial(_ffn_kernel, nf=nf),
        grid=(bsz, nt),
        in_specs=[first(d), first(na), first(ob.shape[2]), tok_next(d), tok_next(na),
                  tok_next(ob.shape[2]), full(mod), full(g_ffn)]
                 + [full(a) for a in consts],
        out_specs=tok(d),
        out_shape=jax.ShapeDtypeStruct((bsz, s, d), F32),
        scratch_shapes=[pltpu.VMEM((nf, 2, 8, fc), F32), pltpu.VMEM((ts, d), BF16),
                        pltpu.VMEM((ts, d), F32), pltpu.VMEM((ts, d), BF16),
                        pltpu.VMEM((ts, d), F32), pltpu.VMEM((2, ts + 8, fc), F32),
                        pltpu.VMEM((2, ts + 8, fc), F32), pltpu.VMEM((nf, ts, fc), BF16)],
        compiler_params=pltpu.CompilerParams(dimension_semantics=("arbitrary", "arbitrary"),
                                             vmem_limit_bytes=VMEM_LIMIT),
        name="outproj_ffn",
    )(x, oa, ob, x, oa, ob, mod, g_ffn, *consts)


def kernel(x, c, w_ada, b_ada, g_attn, w_in, q_norm_a, k_norm_a, q_norm_b, k_norm_b, lam_vecs,
           subln_a, w_out, g_ffn, w_up, conv_w, conv_b, w_down, rel_bias):
    bsz, s, d = x.shape
    t = min(ATT_T, s)
    ta = min(DIFF_T, s)
    ts = min(512, s)
    topk = min(TOPK_MAX, s // 4)
    scale = HEAD_DIM ** -0.5
    for l in range(w_ada.shape[0]):
        mod = _ada_call(c, w_ada[l], b_ada[l]).reshape(bsz, 6, d)
        gains = jnp.stack([q_norm_a[l], k_norm_a[l], q_norm_b[l], k_norm_b[l]])
        bias_a, bound = _bias_call(rel_bias, gains, ta, 0, A_HEADS)
        bias_b, _ = _bias_call(rel_bias, gains, t, A_HEADS, B_HEADS)
        gqa = (jnp.tile(q_norm_a[l], 2 * A_HEADS) * (scale * LOG2E))[None]
        gka = jnp.tile(k_norm_a[l], 2 * A_HEADS)[None]
        gqb = (jnp.tile(q_norm_b[l], B_HEADS) * (scale * LOG2E))[None]
        gkb = jnp.tile(k_norm_b[l], B_KV_HEADS)[None]
        aq, ak, av, bq, bkv, iq, ik, iw = _inproj_call(
            x, mod, g_attn[l][None], w_in[l], gqa, gka, gqb, gkb, ts)
        lam_init = 0.8 - 0.6 * math.exp(-0.3 * l)
        o_a = _diff_call(lam_vecs[l], subln_a[l][None], bound, aq, ak, av, bias_a, ta, lam_init)
        o_b = _dsa_call(bound, iq, iw, ik, bq, bkv, bias_b, t, topk)
        x = _ffn_call(x, o_a, o_b, mod, g_ffn[l][None], w_out[l], w_up[l], conv_w[l], conv_b[l],
                      w_down[l], ts, 256)
    return x
```

```python
import functools
import math

import jax
import jax.numpy as jnp
from jax import lax
from jax.experimental import pallas as pl
from jax.experimental.pallas import tpu as pltpu

F32 = jnp.float32
BF16 = jnp.bfloat16
I32 = jnp.int32

HEAD_DIM = 64
A_HEADS = 4
A_VDIM = 2 * HEAD_DIM
B_HEADS = 8
B_KV_HEADS = 2
B_GROUP = B_HEADS // B_KV_HEADS
IDX_HEADS = 8
IDX_DIM = 64
TOPK_MAX = 256
N_BUCKETS = 32
MAX_DISTANCE = 128
CONV_W = 3
EPS = 1e-6

LANES = 128
ATT_T = 512
SEL_ROWS = 128
DIFF_T = 512
FAR_BLOCKS = 2
MASKED = -1e30
LOG2E = math.log2(math.e)
L_TINY = 2.0 ** -60
NEG_KEY = -2139095041
INT_MIN = -2147483648
VMEM_LIMIT = 56 * 1024 * 1024


def _dot(a, b):
    return jnp.dot(a, b, preferred_element_type=F32)


def _dot_nt(a, b):
    return lax.dot_general(a, b, (((1,), (1,)), ((), ())), preferred_element_type=F32)


def _split_bf16(a):
    hi = a.astype(BF16)
    lo = (a - hi.astype(F32)).astype(BF16)
    return hi, lo


def _ada_kernel(c_ref, w_ref, b_ref, o_ref):
    c = c_ref[...]
    a = c / (1.0 + jnp.exp(-c))
    a_hi, a_lo = _split_bf16(a)
    w_hi, w_lo = _split_bf16(w_ref[...])
    o_ref[...] = _dot(a_hi, w_hi) + _dot(a_hi, w_lo) + _dot(a_lo, w_hi) + b_ref[...]


def _ada_call(c, w, b):
    bsz, d = c.shape
    n = w.shape[1]
    tn = 1024
    return pl.pallas_call(
        _ada_kernel,
        grid=(n // tn,),
        in_specs=[pl.BlockSpec((bsz, d), lambda j: (0, 0)),
                  pl.BlockSpec((d, tn), lambda j: (0, j)),
                  pl.BlockSpec((1, tn), lambda j: (0, j))],
        out_specs=pl.BlockSpec((bsz, tn), lambda j: (0, j)),
        out_shape=jax.ShapeDtypeStruct((bsz, n), F32),
        compiler_params=pltpu.CompilerParams(dimension_semantics=("arbitrary",),
                                             vmem_limit_bytes=VMEM_LIMIT),
        name="ada_mod",
    )(c, w, b.reshape(1, n))


def _bias_kernel(tab_ref, tabv_ref, gains_ref, o_ref, bound_ref, *, t, h0):
    h = pl.program_id(0) + h0
    max_exact = N_BUCKETS // 2
    b = MAX_DISTANCE
    row = lax.broadcasted_iota(I32, (b, b), 0)
    col = lax.broadcasted_iota(I32, (b, b), 1)
    last = tab_ref[N_BUCKETS - 1, h]
    base = []
    for d in range(2):
        n = jnp.maximum(row - col + d * b, 0)
        nf = jnp.maximum(n, 1).astype(F32)
        large = max_exact + (jnp.log(nf / max_exact) / math.log(MAX_DISTANCE / max_exact)
                             * (N_BUCKETS - max_exact)).astype(I32)
        large = jnp.minimum(large, N_BUCKETS - 1)
        bucket = jnp.where(n < max_exact, n, large)
        acc = jnp.zeros((b, b), F32)
        for k in range(N_BUCKETS - 1):
            acc = jnp.where(bucket == k, (tab_ref[k, h] - last) * LOG2E, acc)
        base.append(acc)
    by_distance = {0: jnp.where(row >= col, base[0], MASKED), 1: base[1]}
    masked = jnp.full((b, b), MASKED, F32)
    zero = jnp.zeros((b, b), F32)
    nb = t // b
    for d in range(2):
        o_ref[0, d] = jnp.concatenate(
            [jnp.concatenate([masked if d * nb + a - c < 0 else by_distance.get(d * nb + a - c, zero)
                              for c in range(nb)], axis=1) for a in range(nb)], axis=0)
    tab = tabv_ref[...]
    rel = (tab - tab[N_BUCKETS - 1:N_BUCKETS, :]) * LOG2E
    hcol = lax.broadcasted_iota(I32, rel.shape, 1)
    g = jnp.max(jnp.abs(gains_ref[...]), axis=1, keepdims=True)
    for grp in range(2):
        in_grp = (hcol < A_HEADS) if grp == 0 else (hcol >= A_HEADS)
        bmax = jnp.max(jnp.where(in_grp, rel, 0.0), keepdims=True)
        qk = g[2 * grp:2 * grp + 1] * g[2 * grp + 1:2 * grp + 2] * (HEAD_DIM ** 0.5 * LOG2E)
        bound_ref[grp] = jnp.broadcast_to(qk + bmax, bound_ref.shape[1:])


def _bias_call(rel_bias, gains, t, h0, nh):
    return pl.pallas_call(
        functools.partial(_bias_kernel, t=t, h0=h0),
        grid=(nh,),
        in_specs=[pl.BlockSpec(memory_space=pltpu.SMEM),
                  pl.BlockSpec(rel_bias.shape, lambda h: (0, 0)),
                  pl.BlockSpec(gains.shape, lambda h: (0, 0))],
        out_specs=[pl.BlockSpec((1, 2, t, t), lambda h: (h, 0, 0, 0)),
                   pl.BlockSpec((2, 8, LANES), lambda h: (0, 0, 0))],
        out_shape=[jax.ShapeDtypeStruct((nh, 2, t, t), F32),
                   jax.ShapeDtypeStruct((2, 8, LANES), F32)],
        compiler_params=pltpu.CompilerParams(dimension_semantics=("arbitrary",)),
        name="bias_tiles",
    )(rel_bias, rel_bias, gains)


def _modulated_norm(x, g, scale, shift):
    y = x * lax.rsqrt(jnp.mean(x * x, axis=-1, keepdims=True) + EPS)
    return (y * g) * (1.0 + scale) + shift


def _head_norm(y, gain, bd):
    y2 = (y * y).astype(BF16)
    w = bd.shape[0]
    ss = jnp.concatenate([_dot(y2[:, c:c + w], bd) for c in range(0, y.shape[1], w)], axis=1)
    return y * lax.rsqrt(ss * (1.0 / HEAD_DIM) + EPS) * gain


def _inproj_kernel(x_ref, mod_ref, g_ref, waq_ref, wak_ref, wav_ref, wbq_ref, wbkv_ref, wiq_ref,
                   wik_ref, gqa_ref, gka_ref, gqb_ref, gkb_ref, bd_ref,
                   aq_ref, ak_ref, av_ref, bq_ref, bkv_ref, iq_ref, ik_ref, iw_ref):
    x = x_ref[0]
    h = _modulated_norm(x, g_ref[...], mod_ref[0, 1:2, :], mod_ref[0, 0:1, :]).astype(BF16)
    bd = bd_ref[...]
    aq_ref[0] = _head_norm(_dot(h, waq_ref[...]), gqa_ref[...], bd).astype(BF16)
    ak_ref[0] = _head_norm(_dot(h, wak_ref[...]), gka_ref[...], bd).astype(BF16)
    av_ref[0] = _dot(h, wav_ref[...]).astype(BF16)
    bq_ref[0] = _head_norm(_dot(h, wbq_ref[...]), gqb_ref[...], bd).astype(BF16)
    kv = _dot(h, wbkv_ref[...])
    nk = B_KV_HEADS * HEAD_DIM
    bk = _head_norm(kv[:, :nk], gkb_ref[...], bd[:nk, :nk])
    bkv_ref[0] = jnp.concatenate([bk, kv[:, nk:]], axis=1).astype(BF16)
    iq_ref[0] = _dot(h, wiq_ref[...]).astype(BF16)
    ikw = _dot(h, wik_ref[...])
    ik_ref[0] = ikw[:, :2 * IDX_DIM].astype(BF16)
    iw_ref[0] = ikw[:, 2 * IDX_DIM:2 * IDX_DIM + IDX_HEADS] * ((IDX_HEADS * IDX_DIM) ** -0.5)


def _inproj_call(x, mod, g_attn, w_in, gqa, gka, gqb, gkb, ts):
    bsz, s, d = x.shape
    na = A_HEADS * 2 * HEAD_DIM
    nb = B_HEADS * HEAD_DIM
    nkv = B_KV_HEADS * HEAD_DIM
    ni = IDX_HEADS * IDX_DIM
    o = 0
    parts = []
    for width in (na, na, A_HEADS * A_VDIM, nb, 2 * nkv, ni, IDX_DIM + IDX_HEADS):
        parts.append(w_in[:, o:o + width].astype(BF16))
        o += width
    wik = jnp.concatenate([parts[-1][:, :IDX_DIM], parts[-1]], axis=1)
    parts[-1] = jnp.pad(wik, ((0, 0), (0, 2 * LANES - wik.shape[1])))
    grp = jnp.arange(2 * LANES) // HEAD_DIM
    bd = (grp[:, None] == grp[None, :]).astype(BF16)

    def full(a):
        return pl.BlockSpec(a.shape, lambda b, i: (0,) * a.ndim)

    def tok(n):
        return pl.BlockSpec((1, ts, n), lambda b, i: (b, i, 0))

    consts = parts + [gqa, gka, gqb, gkb, bd]
    out_w = [(na, BF16), (na, BF16), (A_HEADS * A_VDIM, BF16), (nb, BF16), (2 * nkv, BF16),
             (ni, BF16), (2 * IDX_DIM, BF16), (IDX_HEADS, F32)]
    return pl.pallas_call(
        _inproj_kernel,
        grid=(bsz, s // ts),
        in_specs=[tok(d), pl.BlockSpec((1, 6, d), lambda b, i: (b, 0, 0)), full(g_attn)]
                 + [full(a) for a in consts],
        out_specs=[tok(n) for n, _ in out_w],
        out_shape=[jax.ShapeDtypeStruct((bsz, s, n), dt) for n, dt in out_w],
        compiler_params=pltpu.CompilerParams(dimension_semantics=("arbitrary", "arbitrary"),
                                             vmem_limit_bytes=VMEM_LIMIT),
        name="in_proj",
    )(x, mod, g_attn, *consts)


def _diff_kernel(lam_ref, g_ref, bound_ref, q_ref, k_ref, v_ref, bias_ref, o_ref, acc_ref, *,
                 t, lam_init):
    i = pl.program_id(2)
    q = q_ref[0]
    lane = lax.broadcasted_iota(I32, q.shape, 1)
    zero = jnp.zeros_like(q)
    qq = jnp.concatenate([jnp.where(lane < HEAD_DIM, q, zero),
                          jnp.where(lane >= HEAD_DIM, q, zero)], axis=0)

    def logits(start, width):
        k = k_ref[0, pl.ds(pl.multiple_of(start, t), width), :]
        return _dot_nt(qq, k)

    def v_ones(start, width):
        v = v_ref[0, pl.ds(pl.multiple_of(start, t), width), :]
        return jnp.concatenate([v, jnp.ones((width, LANES), BF16)], axis=1)

    def near_logits():
        b = jnp.concatenate([bias_ref[0, 1], bias_ref[0, 0]], axis=1)
        return logits((i - 1) * t, 2 * t) + jnp.concatenate([b, b], axis=0)

    def first_logits():
        b = bias_ref[0, 0]
        return logits(0, t) + jnp.concatenate([b, b], axis=0)

    def for_far_blocks(fn):
        nfar = jnp.maximum(i - 1, 0)
        nsup = nfar // FAR_BLOCKS

        def sup(j, c):
            fn(j * (FAR_BLOCKS * t), FAR_BLOCKS * t)
            return c

        def single(j, c):
            fn(j * t, t)
            return c

        lax.fori_loop(0, nsup, sup, 0)
        lax.fori_loop(nsup * FAR_BLOCKS, nfar, single, 0)

    def accumulate(m):
        @pl.when(i > 0)
        def _():
            p = jnp.exp2(near_logits() - m).astype(BF16)
            acc_ref[...] = _dot(p, v_ones((i - 1) * t, 2 * t))

        @pl.when(i == 0)
        def _():
            p = jnp.exp2(first_logits() - m).astype(BF16)
            acc_ref[...] = _dot(p, v_ones(0, t))

        def far(start, width):
            p = jnp.exp2(logits(start, width) - m).astype(BF16)
            acc_ref[...] += _dot(p, v_ones(start, width))

        for_far_blocks(far)

    accumulate(bound_ref[0, 0:1, 0:1])

    @pl.when(jnp.min(acc_ref[:, LANES:LANES + 1]) < L_TINY)
    def _():
        acc_ref[...] = jnp.full(acc_ref.shape, MASKED, F32)

        def fold_max(s):
            w = acc_ref.shape[1]
            for c in range(s.shape[1] // w):
                acc_ref[...] = jnp.maximum(acc_ref[...], s[:, c * w:(c + 1) * w])

        for_far_blocks(lambda start, width: fold_max(logits(start, width)))

        @pl.when(i > 0)
        def _():
            fold_max(near_logits())

        @pl.when(i == 0)
        def _():
            fold_max(first_logits())

        accumulate(jnp.max(acc_ref[...], axis=1, keepdims=True))

    lv = lam_ref[...]
    lam = (jnp.exp(jnp.sum(lv[0:1] * lv[1:2], axis=1, keepdims=True))
           - jnp.exp(jnp.sum(lv[2:3] * lv[3:4], axis=1, keepdims=True)) + lam_init)
    acc = acc_ref[...]
    o = acc[:, :LANES] / acc[:, LANES:]
    o = o[:t] - lam * o[t:]
    o = o * lax.rsqrt(jnp.mean(o * o, axis=-1, keepdims=True) + EPS)
    o_ref[0] = (o * g_ref[...] * (1.0 - lam_init)).astype(BF16)


def _diff_call(lam_vecs, subln, bound, aq, ak, av, bias, t, lam_init):
    bsz, s, _ = aq.shape
    assert t % (2 * LANES) == 0
    return pl.pallas_call(
        functools.partial(_diff_kernel, t=t, lam_init=lam_init),
        grid=(bsz, A_HEADS, s // t),
        in_specs=[pl.BlockSpec(lam_vecs.shape, lambda b, h, i: (0, 0)),
                  pl.BlockSpec(subln.shape, lambda b, h, i: (0, 0)),
                  pl.BlockSpec((1, 8, LANES), lambda b, h, i: (0, 0, 0)),
                  pl.BlockSpec((1, t, A_VDIM), lambda b, h, i: (b, i, h)),
                  pl.BlockSpec((1, s, A_VDIM), lambda b, h, i: (b, 0, h)),
                  pl.BlockSpec((1, s, A_VDIM), lambda b, h, i: (b, 0, h)),
                  pl.BlockSpec((1, 2, t, t), lambda b, h, i: (h, 0, 0, 0))],
        out_specs=pl.BlockSpec((1, t, A_VDIM), lambda b, h, i: (b, i, h)),
        out_shape=jax.ShapeDtypeStruct((bsz, s, A_HEADS * A_VDIM), BF16),
        scratch_shapes=[pltpu.VMEM((2 * t, 2 * LANES), F32)],
        compiler_params=pltpu.CompilerParams(
            dimension_semantics=("arbitrary", "arbitrary", "arbitrary"),
            vmem_limit_bytes=VMEM_LIMIT),
        name="diff_attn",
    )(lam_vecs, subln, bound, aq, ak, av, bias)


def _sortable_key(x):
    b = pltpu.bitcast(x, I32)
    return b ^ ((b >> 31) & 0x7FFFFFFF)


def _dsa_kernel(bound_ref, iq_ref, iw_ref, ik_ref, q_ref, kv_ref, bias_ref, o_ref,
                keys_ref, iqz_ref, wb_ref, thr_ref, tie_ref, acc_ref, *, t, topk):
    i = pl.program_id(1)
    row = lax.broadcasted_iota(I32, (t, t), 0)
    col = lax.broadcasted_iota(I32, (t, t), 1)

    iq = iq_ref[0]
    iw = iw_ref[0]
    lane = lax.broadcasted_iota(I32, (t, LANES), 1)
    for h in range(IDX_HEADS):
        pair = iq[:, (h // 2) * LANES:(h // 2 + 1) * LANES]
        mine = (lane < IDX_DIM) if h % 2 == 0 else (lane >= IDX_DIM)
        iqz_ref[h] = jnp.where(mine, pair, jnp.zeros_like(pair))
        wb_ref[h] = jnp.broadcast_to(iw[:, h:h + 1], (t, LANES))

    def score_block(j, causal):
        kc = ik_ref[0, pl.ds(pl.multiple_of(j * t, t), t), :]
        acc = jnp.zeros((t, t), F32)
        for h in range(IDX_HEADS):
            wb = wb_ref[h]
            wb = jnp.concatenate([wb] * (t // LANES), axis=1)
            acc = acc + wb * jnp.maximum(_dot_nt(iqz_ref[h], kc), 0.0)
        if causal:
            acc = jnp.where(row >= col, acc, -jnp.inf)
        keys_ref[j] = _sortable_key(acc)

    def score_body(j, carry):
        score_block(j, False)
        return carry

    lax.fori_loop(0, i, score_body, 0)
    score_block(i, True)

    ngroups = t // SEL_ROWS

    def group_count(pred, c, r, nfull):
        def add_block(j, cnt, lane_groups):
            k = keys_ref[j, pl.ds(r * SEL_ROWS, SEL_ROWS), :]
            for w in range(lane_groups):
                cnt = cnt + jnp.where(pred(k[:, w * LANES:(w + 1) * LANES], c), 1.0, 0.0)
            return cnt

        cnt = jnp.zeros((SEL_ROWS, LANES), F32)
        if isinstance(nfull, int):
            for j in range(nfull):
                cnt = add_block(j, cnt, t // LANES)
        else:
            cnt = lax.fori_loop(0, nfull, lambda j, cnt: add_block(j, cnt, t // LANES), cnt)
        return add_block(nfull, cnt, -(-(r + 1) * SEL_ROWS // LANES))

    def row_total(cnt):
        return jnp.broadcast_to(jnp.sum(cnt, axis=1, keepdims=True), cnt.shape)

    def count(pred, ref_val, nfull):
        return jnp.concatenate(
            [row_total(group_count(pred, ref_val[r * SEL_ROWS:(r + 1) * SEL_ROWS], r, nfull))
             for r in range(ngroups)], axis=0)

    def ge(k, c):
        return k >= c

    def select(nfull):
        head = (ngroups - 1) * SEL_ROWS

        def resolve(thr_last, cand_last, pending):
            return jnp.where(row_total(pending) >= float(topk), cand_last, thr_last)

        def bit_body(it, carry):
            thr_head, thr_last, cand_last, pending = carry
            bit = jnp.left_shift(jnp.int32(1), 31 - it)
            thr_last = resolve(thr_last, cand_last, pending)
            cand_head = thr_head + bit
            cand_last = thr_last + bit
            keep = count_head(cand_head) >= float(topk)
            return (jnp.where(keep, cand_head, thr_head), thr_last, cand_last,
                    group_count(ge, cand_last, ngroups - 1, nfull))

        def count_head(cand_head):
            return jnp.concatenate(
                [row_total(group_count(ge, cand_head[r * SEL_ROWS:(r + 1) * SEL_ROWS], r, nfull))
                 for r in range(ngroups - 1)], axis=0)

        thr_ref[...] = jnp.full(thr_ref.shape, INT_MIN, I32)
        tie_ref[...] = jnp.zeros(tie_ref.shape, F32)
        start_last = thr_ref[head:, :LANES]
        thr_head, thr_last, cand_last, pending = lax.fori_loop(
            0, 32, bit_body, (thr_ref[:head, :LANES], start_last, start_last, tie_ref[head:]))
        thr = jnp.concatenate([thr_head, resolve(thr_last, cand_last, pending)], axis=0)
        excess = (count(lambda k, c: k >= c, thr, nfull) > float(topk)) & (thr > NEG_KEY)
        thr_ref[...] = jnp.concatenate([thr] * (t // LANES), axis=1)
        tie_ref[...] = jnp.where(excess, 1.0, 0.0)

    for n in range(keys_ref.shape[0]):
        pl.when(i == n)(functools.partial(select, n))

    @pl.when(jnp.max(tie_ref[...]) > 0.0)
    def _():
        tri = (row <= col).astype(BF16)
        thr2 = thr_ref[...]
        excess = tie_ref[...] > 0.0
        need = float(topk) - count(lambda k, c: k > c, thr2[:, :LANES], i)
        need2 = jnp.concatenate([need] * (t // LANES), axis=1)
        excess2 = jnp.concatenate([excess] * (t // LANES), axis=1)

        def body(j, seen):
            k = keys_ref[j]
            eq = k == thr2
            eqf = jnp.where(eq, 1.0, 0.0)
            rank = seen + _dot(eqf.astype(BF16), tri)
            keys_ref[j] = jnp.where(eq & (rank > need2) & excess2, INT_MIN, k)
            return seen + jnp.sum(eqf, axis=1, keepdims=True)

        lax.fori_loop(0, i + 1, body, jnp.zeros((t, 1), F32))

    thr_ref[...] = jnp.maximum(thr_ref[...], NEG_KEY + 1)

    q = q_ref[0]
    qs = []
    for g in range(B_KV_HEADS):
        parts = []
        for r in range(B_GROUP):
            hq = q[:, (g * B_GROUP + r) * HEAD_DIM:(g * B_GROUP + r + 1) * HEAD_DIM]
            z = jnp.zeros_like(hq)
            parts.append(jnp.concatenate([hq, z] if g == 0 else [z, hq], axis=1))
        qs.append(jnp.concatenate(parts, axis=0))

    def logits(j, g, near, h0=0, nh=B_GROUP):
        k = kv_ref[0, pl.ds(pl.multiple_of(j * t, t), t), :LANES]
        madd = jnp.where(keys_ref[j] >= thr_ref[...], 0.0, MASKED)
        s = _dot_nt(qs[g][h0 * t:(h0 + nh) * t], k).reshape(nh, t, t)
        if near is None:
            s = s + madd[None]
        else:
            first = g * B_GROUP + h0
            s = s + (bias_ref[first:first + nh, near] + madd[None])
        return s.reshape(nh * t, t)

    def for_blocks(fn):
        fn(i, 0)

        def far(j, c):
            fn(j, None)
            return c

        lax.fori_loop(0, i - 1, far, 0)

        @pl.when(i > 0)
        def _():
            fn(i - 1, 1)

    def accumulate(ms):
        def step(j, near):
            v = kv_ref[0, pl.ds(pl.multiple_of(j * t, t), t), LANES:]
            vx = jnp.concatenate([v, jnp.ones((t, LANES), BF16)], axis=1)
            hh = B_GROUP // 2
            for g in range(B_KV_HEADS):
                for h0 in range(0, B_GROUP, hh):
                    rs = slice(h0 * t, (h0 + hh) * t)
                    m = ms[g] if ms[g].shape[0] == 1 else ms[g][rs]
                    p = jnp.exp2(logits(j, g, near, h0, hh) - m).astype(BF16)
                    if near == 0:
                        acc_ref[g, rs] = _dot(p, vx)
                    else:
                        acc_ref[g, rs] += _dot(p, vx)

        for_blocks(step)

    m0 = bound_ref[0, 0:1, 0:1]
    accumulate([m0, m0])

    @pl.when(jnp.min(acc_ref[:, :, LANES:LANES + 1]) < L_TINY)
    def _():
        acc_ref[...] = jnp.full(acc_ref.shape, MASKED, F32)

        def step(j, near):
            w = acc_ref.shape[2]
            for g in range(B_KV_HEADS):
                s = logits(j, g, near)
                for c in range(t // w):
                    acc_ref[g] = jnp.maximum(acc_ref[g], s[:, c * w:(c + 1) * w])

        for_blocks(step)
        accumulate([jnp.max(acc_ref[g], axis=1, keepdims=True) for g in range(B_KV_HEADS)])

    outs = []
    for g in range(B_KV_HEADS):
        a = acc_ref[g]
        o = a[:, g * HEAD_DIM:(g + 1) * HEAD_DIM] / a[:, LANES:LANES + HEAD_DIM]
        for r in range(B_GROUP):
            outs.append(o[r * t:(r + 1) * t])
    o_ref[0] = jnp.concatenate(outs, axis=1).astype(BF16)


def _dsa_call(bound, iq, iw, ik, bq, bkv, bias, t, topk):
    bsz, s, _ = bq.shape
    nq = s // t
    assert t % SEL_ROWS == 0 and t % (2 * LANES) == 0
    return pl.pallas_call(
        functools.partial(_dsa_kernel, t=t, topk=topk),
        grid=(bsz, nq),
        in_specs=[pl.BlockSpec((1, 8, LANES), lambda b, i: (1, 0, 0)),
                  pl.BlockSpec((1, t, iq.shape[2]), lambda b, i: (b, i, 0)),
                  pl.BlockSpec((1, t, iw.shape[2]), lambda b, i: (b, i, 0)),
                  pl.BlockSpec((1, s, ik.shape[2]), lambda b, i: (b, 0, 0)),
                  pl.BlockSpec((1, t, bq.shape[2]), lambda b, i: (b, i, 0)),
                  pl.BlockSpec((1, s, bkv.shape[2]), lambda b, i: (b, 0, 0)),
                  pl.BlockSpec(bias.shape, lambda b, i: (0, 0, 0, 0),
                               pipeline_mode=pl.Buffered(1))],
        out_specs=pl.BlockSpec((1, t, B_HEADS * HEAD_DIM), lambda b, i: (b, i, 0)),
        out_shape=jax.ShapeDtypeStruct((bsz, s, B_HEADS * HEAD_DIM), BF16),
        scratch_shapes=[pltpu.VMEM((nq, t, t), I32),
                        pltpu.VMEM((IDX_HEADS, t, LANES), BF16),
                        pltpu.VMEM((IDX_HEADS, t, LANES), F32),
                        pltpu.VMEM((t, t), I32),
                        pltpu.VMEM((t, LANES), F32),
                        pltpu.VMEM((B_KV_HEADS, B_GROUP * t, 2 * LANES), F32)],
        compiler_params=pltpu.CompilerParams(dimension_semantics=("arbitrary", "arbitrary"),
                                             vmem_limit_bytes=VMEM_LIMIT),
        name="dsa_attn",
    )(bound, iq, iw, ik, bq, bkv, bias)


def _ffn_kernel(x_ref, oa_ref, ob_ref, xn_ref, oan_ref, obn_ref, mod_ref, g_ref, wo_ref, wgv_ref,
                cwg_ref, cwv_ref, cbg_ref, cbv_ref, wd_ref, o_ref, carry_ref, h_ref, y_ref, hn_ref,
                yn_ref, ua_ref, ub_ref, a_ref, *, nf):
    b = pl.program_id(0)
    i = pl.program_id(1)
    ts = x_ref.shape[1]
    mod = mod_ref[b]

    def residual_and_norm(xr, oar, obr, m, x1_out, h_out):
        x1 = xr[0] + m[2:3] * (_dot(oar[0], wo_ref[0]) + _dot(obr[0], wo_ref[1]))
        h_out[...] = _modulated_norm(x1, g_ref[...], m[4:5], m[3:4]).astype(BF16)
        x1_out[...] = x1

    @pl.when((b == 0) & (i == 0))
    def _():
        residual_and_norm(x_ref, oa_ref, ob_ref, mod, yn_ref, hn_ref)

    h_ref[...] = hn_ref[...]
    y_ref[...] = yn_ref[...]

    @pl.when(i == 0)
    def _():
        carry_ref[...] = jnp.zeros(carry_ref.shape, F32)

    def up(f, buf_ref):
        fc = a_ref.shape[2]
        dff = wgv_ref.shape[1] // 2
        h = h_ref[...]
        buf_ref[0, 8:8 + ts] = _dot(h, wgv_ref[:, f * fc:(f + 1) * fc])
        buf_ref[1, 8:8 + ts] = _dot(h, wgv_ref[:, dff + f * fc:dff + (f + 1) * fc])

    def conv(buf_ref, part, prev8, cw, cb):
        buf_ref[part, 0:8] = prev8
        return (cb + cw[2:3] * buf_ref[part, 8:8 + ts] + cw[1:2] * buf_ref[part, 7:7 + ts]
                + cw[0:1] * buf_ref[part, 6:6 + ts])

    def gate(f, buf_ref):
        yg = conv(buf_ref, 0, carry_ref[f, 0], cwg_ref[f], cbg_ref[f])
        yv = conv(buf_ref, 1, carry_ref[f, 1], cwv_ref[f], cbv_ref[f])
        carry_ref[f, 0] = buf_ref[0, ts:ts + 8]
        carry_ref[f, 1] = buf_ref[1, ts:ts + 8]
        a_ref[f] = ((yg / (1.0 + jnp.exp(-yg))) * yv).astype(BF16)

    def stage(f, cur_ref, nxt_ref):
        gate(f, cur_ref)
        up(f + 1, nxt_ref)

    up(0, ua_ref)

    assert nf % 2 == 1
    for k in range((nf - 1) // 2):
        stage(2 * k, ua_ref, ub_ref)
        stage(2 * k + 1, ub_ref, ua_ref)
    gate(nf - 1, ua_ref)
    nxt = b * pl.num_programs(1) + i + 1
    bn = jnp.minimum(nxt // pl.num_programs(1), pl.num_programs(0) - 1)
    residual_and_norm(xn_ref, oan_ref, obn_ref, mod_ref[bn], yn_ref, hn_ref)
    a = jnp.concatenate([a_ref[f] for f in range(nf)], axis=1)
    o_ref[0] = y_ref[...] + mod[5:6] * _dot(a, wd_ref[...])


def _ffn_call(x, oa, ob, mod, g_ffn, w_out, w_up, conv_w, conv_b, w_down, ts, fc):
    bsz, s, d = x.shape
    dff = w_down.shape[0]
    nf = dff // fc
    na = oa.shape[2]
    wo = w_out.astype(BF16).reshape(2, na, d)
    wgv = w_up.astype(BF16)
    cwg = conv_w[:, :dff].reshape(CONV_W, nf, fc).transpose(1, 0, 2)
    cwv = conv_w[:, dff:].reshape(CONV_W, nf, fc).transpose(1, 0, 2)
    cbg = conv_b[:dff].reshape(nf, 1, fc)
    cbv = conv_b[dff:].reshape(nf, 1, fc)
    wd = w_down.astype(BF16)

    def full(a):
        return pl.BlockSpec(a.shape, lambda b, i: (0,) * a.ndim, pipeline_mode=pl.Buffered(1))

    nt = s // ts

    def tok(n):
        return pl.BlockSpec((1, ts, n), lambda b, i: (b, i, 0))

    def tok_next(n):
        def index(b, i):
            g = jnp.minimum(b * nt + i + 1, bsz * nt - 1)
            return (g // nt, g % nt, 0)
        return pl.BlockSpec((1, ts, n), index)

    def first(n):
        return pl.BlockSpec((1, ts, n), lambda b, i: (0, 0, 0))

    consts = [wo, wgv, cwg, cwv, cbg, cbv, wd]
    return pl.pallas_call(
        functools.partial(_ffn_kernel, nf=nf),
        grid=(bsz, nt),
        in_specs=[first(d), first(na), first(ob.shape[2]), tok_next(d), tok_next(na),
                  tok_next(ob.shape[2]), full(mod), full(g_ffn)]
                 + [full(a) for a in consts],
        out_specs=tok(d),
        out_shape=jax.ShapeDtypeStruct((bsz, s, d), F32),
        scratch_shapes=[pltpu.VMEM((nf, 2, 8, fc), F32), pltpu.VMEM((ts, d), BF16),
                        pltpu.VMEM((ts, d), F32), pltpu.VMEM((ts, d), BF16),
                        pltpu.VMEM((ts, d), F32), pltpu.VMEM((2, ts + 8, fc), F32),
                        pltpu.VMEM((2, ts + 8, fc), F32), pltpu.VMEM((nf, ts, fc), BF16)],
        compiler_params=pltpu.CompilerParams(dimension_semantics=("arbitrary", "arbitrary"),
                                             vmem_limit_bytes=VMEM_LIMIT),
        name="outproj_ffn",
    )(x, oa, ob, x, oa, ob, mod, g_ffn, *consts)


def kernel(x, c, w_ada, b_ada, g_attn, w_in, q_norm_a, k_norm_a, q_norm_b, k_norm_b, lam_vecs,
           subln_a, w_out, g_ffn, w_up, conv_w, conv_b, w_down, rel_bias):
    bsz, s, d = x.shape
    t = min(ATT_T, s)
    ta = min(DIFF_T, s)
    ts = min(512, s)
    topk = min(TOPK_MAX, s // 4)
    scale = HEAD_DIM ** -0.5
    for l in range(w_ada.shape[0]):
        mod = _ada_call(c, w_ada[l], b_ada[l]).reshape(bsz, 6, d)
        gains = jnp.stack([q_norm_a[l], k_norm_a[l], q_norm_b[l], k_norm_b[l]])
        bias_a, bound = _bias_call(rel_bias, gains, ta, 0, A_HEADS)
        bias_b, _ = _bias_call(rel_bias, gains, t, A_HEADS, B_HEADS)
        gqa = (jnp.tile(q_norm_a[l], 2 * A_HEADS) * (scale * LOG2E))[None]
        gka = jnp.tile(k_norm_a[l], 2 * A_HEADS)[None]
        gqb = (jnp.tile(q_norm_b[l], B_HEADS) * (scale * LOG2E))[None]
        gkb = jnp.tile(k_norm_b[l], B_KV_HEADS)[None]
        aq, ak, av, bq, bkv, iq, ik, iw = _inproj_call(
            x, mod, g_attn[l][None], w_in[l], gqa, gka, gqb, gkb, ts)
        lam_init = 0.8 - 0.6 * math.exp(-0.3 * l)
        o_a = _diff_call(lam_vecs[l], subln_a[l][None], bound, aq, ak, av, bias_a, ta, lam_init)
        o_b = _dsa_call(bound, iq, iw, ik, bq, bkv, bias_b, t, topk)
        x = _ffn_call(x, o_a, o_b, mod, g_ffn[l][None], w_out[l], w_up[l], conv_w[l], conv_b[l],
                      w_down[l], ts, 256)
    return x
```
